```python
import jax, jax.numpy as jnp
from jax import lax
import numpy as np

D_MODEL = 1024
BATCH = 4
SEQ = 8192
DEPTH = 2

GRID_W = 64
HEAD_DIM = 64
NA_HEADS = D_MODEL // 256
NA_WIDTH = NA_HEADS * HEAD_DIM
NA_KR_MAX = 8
NA_KC = 16
MLA_HEADS = D_MODEL // 128
MLA_NOPE = 64
MLA_ROPE = 32
MLA_V = 64
MLA_Q_RANK = 384
MLA_KV_RANK = 256
MLA_WIDTH = MLA_HEADS * MLA_V
ROPE_THETA = 10000.0
Q_BLOCK = 128
CONV_WIDTH = D_MODEL // 4
CONV_K = 3
D_MIX = NA_WIDTH + MLA_WIDTH + CONV_WIDTH
D_IN = 3 * NA_WIDTH + MLA_Q_RANK + MLA_KV_RANK + MLA_ROPE + 3 * CONV_WIDTH
OUT_GROUPS = D_MIX // HEAD_DIM
D_FF = -(-8 * D_MODEL // (3 * 256)) * 256
EPS = 1e-6

kernel_name = 'hybrid_na_mla_shortconv_encoder'


def rms_norm(x, g):
    xf = x.astype(jnp.float32)
    y = xf * lax.rsqrt(jnp.mean(xf * xf, axis=-1, keepdims=True) + EPS)
    return (y * g.astype(jnp.float32)).astype(x.dtype)


def ada_norm(x, g, shift, scale):
    return rms_norm(x, g) * (1 + scale) + shift


def rope_tables(positions):
    inv = ROPE_THETA ** (-jnp.arange(0, MLA_ROPE, 2, dtype=jnp.float32) / MLA_ROPE)
    ang = positions.astype(jnp.float32)[..., None] * inv
    return jnp.cos(ang), jnp.sin(ang)


def apply_rope(x, cos, sin):
    xf = x.astype(jnp.float32)
    x1, x2 = jnp.split(xf, 2, axis=-1)
    return jnp.concatenate([x1 * cos - x2 * sin, x2 * cos + x1 * sin], axis=-1).astype(x.dtype)


def neighbourhood_attention(q, k, v, rpb):
    b, t, h, dh = q.shape
    rows = t // GRID_W
    kr = min(NA_KR_MAX, rows)
    kc = NA_KC
    qg = q.reshape(b, rows, GRID_W, h, dh)
    kg = k.reshape(b, rows, GRID_W, h, dh)
    vg = v.reshape(b, rows, GRID_W, h, dh)
    r = jnp.arange(rows)
    row_start = jnp.clip(r - kr // 2, 0, rows - kr)
    row_idx = row_start[:, None] + jnp.arange(kr)[None, :]
    k_band = kg[:, row_idx]
    v_band = vg[:, row_idx]
    s = jnp.einsum('brqhd,brjwhd->bhrqjw', qg, k_band).astype(jnp.float32) * (dh ** -0.5)
    cols = jnp.arange(GRID_W)
    col_start = jnp.clip(cols - kc // 2, 0, GRID_W - kc)
    col_ok = (cols[None, :] >= col_start[:, None]) & (cols[None, :] < col_start[:, None] + kc)
    dr_i = row_idx - r[:, None] + NA_KR_MAX - 1
    dc_i = jnp.clip(cols[None, :] - cols[:, None] + kc - 1, 0, 2 * kc - 2)
    bias = rpb[:, dr_i[:, None, :, None], dc_i[None, :, None, :]]
    s = jnp.where(col_ok[:, None, :], s + bias.astype(jnp.float32)[None], -jnp.inf)
    p = jax.nn.softmax(s.reshape(b, h, rows, GRID_W, kr * GRID_W), axis=-1)
    p = p.reshape(s.shape).astype(v.dtype)
    o = jnp.einsum('bhrqjw,brjwhd->brqhd', p, v_band)
    return o.reshape(b, t, h * dh)


def latent_attention(c_q, c_kv, k_rope, q_a_g, kv_a_g, w_uq, w_ukv, qn_g, kn_g, qr_g, kr_g, cos, sin):
    b, t, _ = c_q.shape
    q = (rms_norm(c_q, q_a_g) @ w_uq).reshape(b, t, MLA_HEADS, MLA_NOPE + MLA_ROPE)
    kv = (rms_norm(c_kv, kv_a_g) @ w_ukv).reshape(b, t, MLA_HEADS, MLA_NOPE + MLA_V)
    q_nope = rms_norm(q[..., :MLA_NOPE], qn_g)
    q_rope = apply_rope(rms_norm(q[..., MLA_NOPE:], qr_g), cos[:, :, None], sin[:, :, None])
    k_nope = rms_norm(kv[..., :MLA_NOPE], kn_g)
    v = kv[..., MLA_NOPE:]
    k_r = apply_rope(rms_norm(k_rope, kr_g), cos, sin)
    scale = (MLA_NOPE + MLA_ROPE) ** -0.5
    nb = t // Q_BLOCK

    def to_blocks(a):
        return jnp.moveaxis(a.reshape(b, nb, Q_BLOCK, *a.shape[2:]), 1, 0)

    def block(qs):
        qn, qr = qs
        s = jnp.einsum('bqhd,bkhd->bhqk', qn, k_nope) + jnp.einsum('bqhr,bkr->bhqk', qr, k_r)
        p = jax.nn.softmax(s.astype(jnp.float32) * scale, axis=-1).astype(v.dtype)
        return jnp.einsum('bhqk,bkhd->bqhd', p, v)

    o = lax.map(block, (to_blocks(q_nope), to_blocks(q_rope)))
    return jnp.moveaxis(o, 0, 1).reshape(b, t, MLA_WIDTH)


def short_conv(x_in, gate_b, gate_c, conv_w, conv_b):
    u = gate_c * x_in
    y = lax.conv_general_dilated(u, conv_w[:, None, :], window_strides=(1,),
                                 padding=((CONV_K // 2, CONV_K // 2),),
                                 dimension_numbers=('NWC', 'WIO', 'NWC'),
                                 feature_group_count=CONV_WIDTH) + conv_b
    return gate_b * y


def setup_inputs(seed: int = 0) -> dict:
    key = jax.random.key(seed)
    ks = jax.random.split(key, 24)
    f32 = jnp.float32

    def dense(k, shape, fan_in, mult=1.0):
        return jax.random.normal(k, shape, f32) * (mult * fan_in ** -0.5)

    def gain(k, shape):
        return 1.0 + 0.05 * jax.random.normal(k, shape, f32)

    def small(k, shape, s):
        return s * jax.random.normal(k, shape, f32)

    L = DEPTH
    return {
        'x': jax.random.normal(ks[0], (BATCH, SEQ, D_MODEL), f32),
        'c': jax.random.normal(ks[1], (BATCH, D_MODEL), f32),
        'positions': jnp.broadcast_to(jnp.arange(SEQ, dtype=jnp.int32), (BATCH, SEQ)),
        'norm1_g': gain(ks[2], (L, D_MODEL)),
        'norm2_g': gain(ks[3], (L, D_MODEL)),
        'w_ada': dense(ks[4], (L, D_MODEL, 6 * D_MODEL), D_MODEL, 0.5),
        'b_ada': small(ks[5], (L, 6 * D_MODEL), 0.02),
        'w_in': dense(ks[6], (L, D_MODEL, D_IN), D_MODEL),
        'na_q_g': gain(ks[7], (L, HEAD_DIM)),
        'na_k_g': gain(ks[8], (L, HEAD_DIM)),
        'na_rpb': small(ks[9], (L, NA_HEADS, 2 * NA_KR_MAX - 1, 2 * NA_KC - 1), 0.05),
        'mla_q_a_g': gain(ks[10], (L, MLA_Q_RANK)),
        'mla_kv_a_g': gain(ks[11], (L, MLA_KV_RANK)),
        'mla_w_uq': dense(ks[12], (L, MLA_Q_RANK, MLA_HEADS * (MLA_NOPE + MLA_ROPE)), MLA_Q_RANK),
        'mla_w_ukv': dense(ks[13], (L, MLA_KV_RANK, MLA_HEADS * (MLA_NOPE + MLA_V)), MLA_KV_RANK),
        'mla_qn_g': gain(ks[14], (L, MLA_NOPE)),
        'mla_kn_g': gain(ks[15], (L, MLA_NOPE)),
        'mla_qr_g': gain(ks[16], (L, MLA_ROPE)),
        'mla_kr_g': gain(ks[17], (L, MLA_ROPE)),
        'conv_w': dense(ks[18], (L, CONV_K, CONV_WIDTH), CONV_K),
        'conv_b': small(ks[19], (L, CONV_WIDTH), 0.02),
        'out_norm_g': gain(ks[20], (L, D_MIX)),
        'w_out': dense(ks[21], (L, D_MIX, D_MODEL), D_MIX),
        'w_gu': dense(ks[22], (L, D_MODEL, 2 * D_FF), D_MODEL),
        'w_down': dense(ks[23], (L, D_FF, D_MODEL), D_FF),
    }


def reference(x, c, positions, norm1_g, norm2_g, w_ada, b_ada, w_in, na_q_g, na_k_g, na_rpb,
              mla_q_a_g, mla_kv_a_g, mla_w_uq, mla_w_ukv, mla_qn_g, mla_kn_g, mla_qr_g, mla_kr_g,
              conv_w, conv_b, out_norm_g, w_out, w_gu, w_down):
    b, t, _ = x.shape
    cos, sin = rope_tables(positions)
    c_act = jax.nn.silu(c)
    i0 = 3 * NA_WIDTH
    i1 = i0 + MLA_Q_RANK
    i2 = i1 + MLA_KV_RANK
    i3 = i2 + MLA_ROPE
    for l in range(DEPTH):
        mod = c_act @ w_ada[l] + b_ada[l]
        sh1, sc1, g1, sh2, sc2, g2 = jnp.split(mod[:, None, :], 6, axis=-1)
        h = ada_norm(x, norm1_g[l], sh1, sc1)
        proj = h @ w_in[l]
        na_qkv, c_q, c_kv, k_rope, conv_in = jnp.split(proj, [i0, i1, i2, i3], axis=-1)
        q, k, v = jnp.split(na_qkv, 3, axis=-1)
        q = rms_norm(q.reshape(b, t, NA_HEADS, HEAD_DIM), na_q_g[l])
        k = rms_norm(k.reshape(b, t, NA_HEADS, HEAD_DIM), na_k_g[l])
        v = v.reshape(b, t, NA_HEADS, HEAD_DIM)
        y_na = neighbourhood_attention(q, k, v, na_rpb[l])
        y_mla = latent_attention(c_q, c_kv, k_rope, mla_q_a_g[l], mla_kv_a_g[l], mla_w_uq[l],
                                 mla_w_ukv[l], mla_qn_g[l], mla_kn_g[l], mla_qr_g[l], mla_kr_g[l],
                                 cos, sin)
        x_in, gate_b, gate_c = jnp.split(conv_in, 3, axis=-1)
        y_conv = short_conv(x_in, gate_b, gate_c, conv_w[l], conv_b[l])
        mixed = jnp.concatenate([y_na, y_mla, y_conv], axis=-1)
        mixed = rms_norm(mixed.reshape(b, t, OUT_GROUPS, HEAD_DIM),
                         out_norm_g[l].reshape(OUT_GROUPS, HEAD_DIM)).reshape(b, t, D_MIX)
        x = x + g1 * (mixed @ w_out[l])
        h2 = ada_norm(x, norm2_g[l], sh2, sc2)
        gt, up = jnp.split(h2 @ w_gu[l], 2, axis=-1)
        x = x + g2 * ((jax.nn.silu(gt) * up) @ w_down[l])
    return x
```

```python
import functools

import jax
import jax.numpy as jnp
import numpy as np
from jax import lax
from jax.experimental import pallas as pl
from jax.experimental.pallas import tpu as pltpu

F32 = jnp.float32
BF16 = jnp.bfloat16

GRID_W = 64
HEAD_DIM = 64
NA_HEADS = 4
NA_KR = 8
NA_KC = 16
MLA_HEADS = 8
MLA_NOPE = 64
MLA_ROPE = 32
MLA_V = 64
MLA_Q_RANK = 384
MLA_KV_RANK = 256
CONV_WIDTH = 256
ROPE_THETA = 10000.0
EPS = 1e-6

LANES = 128
NA_Q_ROWS = 8
NA_K_ROWS = 16
MASK_VALUE = -1e30
VMEM_LIMIT = 56 * 1024 * 1024


def _cparams(sem):
    return pltpu.CompilerParams(dimension_semantics=sem, vmem_limit_bytes=VMEM_LIMIT)


def _const_spec(shape):
    nd = len(shape)
    return pl.BlockSpec(shape, lambda *_: (0,) * nd)


def _split_bf16(x):
    hi = x.astype(BF16)
    lo = (x - hi.astype(F32)).astype(BF16)
    return hi, lo


def _ada_kernel(c_ref, w_ref, b_ref, o_ref):
    c = c_ref[...]
    a = c * jax.nn.sigmoid(c)
    a_hi, a_lo = _split_bf16(a)
    w_hi, w_lo = _split_bf16(w_ref[0])
    acc = jnp.dot(a_hi, w_hi, preferred_element_type=F32)
    acc += jnp.dot(a_lo, w_hi, preferred_element_type=F32)
    acc += jnp.dot(a_hi, w_lo, preferred_element_type=F32)
    o_ref[0] = acc + b_ref[0]


def _ada_modulation(c, w_ada, b_ada):
    L, D, N = w_ada.shape
    B = c.shape[0]
    rows = 8
    c_pad = jnp.zeros((rows, D), F32).at[:B].set(c)
    tn = 1536
    out = pl.pallas_call(
        _ada_kernel,
        grid=(L, N // tn),
        in_specs=[
            pl.BlockSpec((rows, D), lambda l, j: (0, 0)),
            pl.BlockSpec((1, D, tn), lambda l, j: (l, 0, j)),
            pl.BlockSpec((1, 1, tn), lambda l, j: (l, 0, j)),
        ],
        out_specs=pl.BlockSpec((1, rows, tn), lambda l, j: (l, 0, j)),
        out_shape=jax.ShapeDtypeStruct((L, rows, N), F32),
        compiler_params=_cparams(("parallel", "parallel")),
        name="ada_mod",
    )(c_pad, w_ada, b_ada.reshape(L, 1, N))
    return out[:, :B].reshape(L, B, 6, D)


def _rope_kernel(pos_ref, inv_ref, sgn_ref, cos_ref, sin_ref):
    ang = pos_ref[0].astype(F32) * inv_ref[...]
    cos_ref[0] = jnp.cos(ang)
    sin_ref[0] = jnp.sin(ang) * sgn_ref[...]


def _rope_tables(positions):
    B, T = positions.shape
    half = MLA_ROPE // 2
    inv = ROPE_THETA ** (-jnp.arange(0, MLA_ROPE, 2, dtype=F32) / MLA_ROPE)
    inv_lane = jnp.zeros((1, LANES), F32)
    inv_lane = inv_lane.at[0, MLA_NOPE:MLA_NOPE + half].set(inv)
    inv_lane = inv_lane.at[0, MLA_NOPE + half:MLA_NOPE + 2 * half].set(inv)
    sgn = np.zeros((1, LANES), np.float32)
    sgn[0, MLA_NOPE:MLA_NOPE + half] = -1.0
    sgn[0, MLA_NOPE + half:MLA_NOPE + 2 * half] = 1.0
    tm = min(T, 2048)
    spec = pl.BlockSpec((1, tm, LANES), lambda b, i: (b, i, 0))
    return pl.pallas_call(
        _rope_kernel,
        grid=(B, T // tm),
        in_specs=[
            pl.BlockSpec((1, tm, 1), lambda b, i: (b, i, 0)),
            _const_spec((1, LANES)),
            _const_spec((1, LANES)),
        ],
        out_specs=[spec, spec],
        out_shape=[jax.ShapeDtypeStruct((B, T, LANES), F32)] * 2,
        compiler_params=_cparams(("parallel", "parallel")),
        name="rope_tables",
    )(positions.reshape(B, T, 1), inv_lane, jnp.asarray(sgn))


def _slab_rms(xs, m2_ref, gain):
    x2 = xs * xs
    hi, lo = _split_bf16(x2)
    ms = jnp.dot(jnp.concatenate([hi, lo], axis=1), m2_ref[...], preferred_element_type=F32)
    return xs * lax.rsqrt(ms + EPS) * gain


def _slab_rope(xs, cos, sin, lane):
    half = MLA_ROPE // 2
    partner = jnp.where(lane < MLA_NOPE + half,
                        pltpu.roll(xs, LANES - half, 1),
                        pltpu.roll(xs, half, 1))
    return xs * cos + partner * sin


def _row_rms(x, gain):
    ms = jnp.mean(x * x, axis=-1, keepdims=True)
    return x * lax.rsqrt(ms + EPS) * gain


def _proj_kernel(x_ref, mod_ref, g1_ref, win_ref, wuq_ref, wuk_ref, wuv_ref, m2_ref,
                 gqna_ref, gkna_ref, gqa_ref, gkva_ref, gq_ref, gk_ref, gkr_ref, cos_ref, sin_ref,
                 qna_ref, kna_ref, vna_ref, qm_ref, km_ref, vm_ref, u_ref, gb_ref):
    x = x_ref[0]
    sh = mod_ref[0, 0:1, :]
    sc = mod_ref[0, 1:2, :]
    h = _row_rms(x, g1_ref[...]) * (1.0 + sc) + sh
    hb = h.astype(BF16)

    def proj(a, b):
        return jnp.dot(hb, win_ref[:, a:b], preferred_element_type=F32)

    nq = NA_HEADS * LANES
    pq = proj(0, nq)
    pk = proj(nq, 2 * nq)
    for hd in range(NA_HEADS):
        sl = slice(hd * LANES, (hd + 1) * LANES)
        qna_ref[0, :, sl] = _slab_rms(pq[:, sl], m2_ref, gqna_ref[...]).astype(BF16)
        kna_ref[0, :, sl] = _slab_rms(pk[:, sl], m2_ref, gkna_ref[...]).astype(BF16)
    o = 2 * nq
    vna_ref[0] = proj(o, o + 256).T.astype(BF16)
    o += 256

    cos = cos_ref[0]
    sin = sin_ref[0]
    lane = lax.broadcasted_iota(jnp.int32, (1, LANES), 1)

    cq = _row_rms(proj(o, o + MLA_Q_RANK), gqa_ref[...]).astype(BF16)
    o += MLA_Q_RANK
    q = jnp.dot(cq, wuq_ref[...], preferred_element_type=F32)
    for hd in range(MLA_HEADS):
        sl = slice(hd * LANES, (hd + 1) * LANES)
        qs = _slab_rms(q[:, sl], m2_ref, gq_ref[...])
        qm_ref[0, :, sl] = _slab_rope(qs, cos, sin, lane).astype(BF16)

    ckv = _row_rms(proj(o, o + MLA_KV_RANK), gkva_ref[...]).astype(BF16)
    o += MLA_KV_RANK
    kr = _slab_rms(proj(o, o + LANES), m2_ref, gkr_ref[...])
    kr = _slab_rope(kr, cos, sin, lane)
    o += LANES
    kn = jnp.dot(ckv, wuk_ref[...], preferred_element_type=F32)
    for hd in range(MLA_HEADS):
        sl = slice(hd * LANES, (hd + 1) * LANES)
        km_ref[0, :, sl] = (_slab_rms(kn[:, sl], m2_ref, gk_ref[...]) + kr).astype(BF16)
    vm_ref[0] = jnp.dot(ckv, wuv_ref[...], preferred_element_type=F32).T.astype(BF16)

    cw = CONV_WIDTH
    pc = proj(o, o + 3 * cw)
    u_ref[0] = (pc[:, 2 * cw:3 * cw] * pc[:, 0:cw]).T
    gb_ref[0] = pc[:, cw:2 * cw].T


def _proj_call(x, mod, cos_t, sin_t, p, tm):
    B, T, D = x.shape
    nq = NA_HEADS * LANES
    nm = MLA_HEADS * LANES
    tok = lambda w: pl.BlockSpec((1, tm, w), lambda b, i: (b, i, 0))
    chan = lambda c: pl.BlockSpec((1, c, tm), lambda b, i: (b, 0, i))
    consts = [p["g1"], p["w_in"], p["w_uq"], p["w_uk"], p["w_uv"], p["m2"], p["g_qna"], p["g_kna"],
              p["g_qa"], p["g_kva"], p["g_q"], p["g_k"], p["g_kr"]]
    in_specs = ([tok(D), pl.BlockSpec((1, 6, D), lambda b, i: (b, 0, 0))]
                + [_const_spec(a.shape) for a in consts] + [tok(LANES), tok(LANES)])
    out_shape = [
        jax.ShapeDtypeStruct((B, T, nq), BF16), jax.ShapeDtypeStruct((B, T, nq), BF16),
        jax.ShapeDtypeStruct((B, NA_HEADS * HEAD_DIM, T), BF16),
        jax.ShapeDtypeStruct((B, T, nm), BF16), jax.ShapeDtypeStruct((B, T, nm), BF16),
        jax.ShapeDtypeStruct((B, MLA_HEADS * MLA_V, T), BF16),
        jax.ShapeDtypeStruct((B, CONV_WIDTH, T), F32), jax.ShapeDtypeStruct((B, CONV_WIDTH, T), F32),
    ]
    out_specs = [tok(nq), tok(nq), chan(NA_HEADS * HEAD_DIM), tok(nm), tok(nm),
                 chan(MLA_HEADS * MLA_V), chan(CONV_WIDTH), chan(CONV_WIDTH)]
    return pl.pallas_call(
        _proj_kernel,
        grid=(B, T // tm),
        in_specs=in_specs,
        out_specs=out_specs,
        out_shape=out_shape,
        compiler_params=_cparams(("parallel", "parallel")),
        name="in_proj",
    )(x, mod, *consts, cos_t, sin_t)


def _na_kernel(q_ref, k_ref, v_ref, bias_ref, o_ref, *, rows):
    rb = pl.program_id(2)
    kstart = jnp.clip(rb * NA_Q_ROWS - (NA_K_ROWS - NA_Q_ROWS) // 2, 0, rows - NA_K_ROWS) * GRID_W
    kstart = pl.multiple_of(kstart, 256)
    nk = NA_K_ROWS * GRID_W
    k = k_ref[0, pl.ds(kstart, nk), :]
    s = lax.dot_general(k, q_ref[0], (((1,), (1,)), ((), ())), preferred_element_type=F32)
    s = s + bias_ref[0, 0]
    m = jnp.max(s, axis=0, keepdims=True)
    p = jnp.exp(s - m)
    l = jnp.sum(p, axis=0, keepdims=True)
    v = v_ref[0, :, pl.ds(kstart, nk)]
    o = jnp.dot(v, p.astype(BF16), preferred_element_type=F32)
    o_ref[0] = o / l


def _na_bias_tables(rpb, rows):
    cols = np.arange(GRID_W)
    col_start = np.clip(cols - NA_KC // 2, 0, GRID_W - NA_KC)
    col_ok = (cols[None, :] >= col_start[:, None]) & (cols[None, :] < col_start[:, None] + NA_KC)
    dc = np.clip(cols[None, :] - cols[:, None] + NA_KC - 1, 0, 2 * NA_KC - 2)
    half = (NA_K_ROWS - NA_Q_ROWS) // 2
    tables = []
    for r0, ks in ((0, 0), (NA_Q_ROWS, NA_Q_ROWS - half), (rows - NA_Q_ROWS, rows - NA_K_ROWS)):
        r = r0 + np.arange(NA_Q_ROWS)
        kr = ks + np.arange(NA_K_ROWS)
        row_start = np.clip(r - NA_KR // 2, 0, rows - NA_KR)
        row_ok = (kr[None, :] >= row_start[:, None]) & (kr[None, :] < row_start[:, None] + NA_KR)
        dr = np.clip(kr[None, :] - r[:, None] + NA_KR - 1, 0, 2 * NA_KR - 2)
        b = rpb[:, dr[:, None, :, None], dc[None, :, None, :]]
        ok = row_ok[:, None, :, None] & col_ok[None, :, None, :]
        b = jnp.where(jnp.asarray(ok)[None], b.astype(F32), MASK_VALUE)
        b = b.transpose(0, 3, 4, 1, 2).reshape(rpb.shape[0], NA_K_ROWS * GRID_W, NA_Q_ROWS * GRID_W)
        tables.append(b)
    return jnp.stack(tables)


def _na_call(qna, kna, vna_t, bias):
    B, T, _ = qna.shape
    rows = T // GRID_W
    nrb = rows // NA_Q_ROWS
    nq = NA_Q_ROWS * GRID_W
    nk = NA_K_ROWS * GRID_W

    def bias_map(h, b, rb):
        var = jnp.where(rb == 0, 0, jnp.where(rb == nrb - 1, 2, 1))
        return (var, h, 0, 0)

    return pl.pallas_call(
        functools.partial(_na_kernel, rows=rows),
        grid=(NA_HEADS, B, nrb),
        in_specs=[
            pl.BlockSpec((1, nq, LANES), lambda h, b, rb: (b, rb, h)),
            pl.BlockSpec((1, T, LANES), lambda h, b, rb: (b, 0, h)),
            pl.BlockSpec((1, HEAD_DIM, T), lambda h, b, rb: (b, h, 0)),
            pl.BlockSpec((1, 1, nk, nq), bias_map),
        ],
        out_specs=pl.BlockSpec((1, HEAD_DIM, nq), lambda h, b, rb: (b, h, rb)),
        out_shape=jax.ShapeDtypeStruct((B, NA_HEADS * HEAD_DIM, T), F32),
        compiler_params=_cparams(("parallel", "parallel", "arbitrary")),
        name="na_attn",
    )(qna, kna, vna_t, bias)


def _mla_kernel(q_ref, k_ref, v_ref, o_ref, *, tk):
    q = q_ref[0]
    tq = q.shape[0]
    nkv = k_ref.shape[1] // tk

    def body(j, carry):
        m, l, acc = carry
        off = pl.multiple_of(j * tk, tk)
        k = k_ref[0, pl.ds(off, tk), :]
        s = lax.dot_general(k, q, (((1,), (1,)), ((), ())), preferred_element_type=F32)
        m_new = jnp.maximum(m, jnp.max(s, axis=0, keepdims=True))
        alpha = jnp.exp(m - m_new)
        p = jnp.exp(s - m_new)
        l = alpha * l + jnp.sum(p, axis=0, keepdims=True)
        v = v_ref[0, :, pl.ds(off, tk)]
        acc = alpha * acc + jnp.dot(v, p.astype(BF16), preferred_element_type=F32)
        return m_new, l, acc

    init = (jnp.full((1, tq), -jnp.inf, F32), jnp.zeros((1, tq), F32), jnp.zeros((MLA_V, tq), F32))
    m, l, acc = lax.fori_loop(0, nkv, body, init)
    o_ref[0] = acc / l


def _mla_call(qm, km, vm_t, tq, tk):
    B, T, _ = qm.shape
    return pl.pallas_call(
        functools.partial(_mla_kernel, tk=tk),
        grid=(B, MLA_HEADS, T // tq),
        in_specs=[
            pl.BlockSpec((1, tq, LANES), lambda b, h, i: (b, i, h)),
            pl.BlockSpec((1, T, LANES), lambda b, h, i: (b, 0, h)),
            pl.BlockSpec((1, MLA_V, T), lambda b, h, i: (b, h, 0)),
        ],
        out_specs=pl.BlockSpec((1, MLA_V, tq), lambda b, h, i: (b, h, i)),
        out_shape=jax.ShapeDtypeStruct((B, MLA_HEADS * MLA_V, T), F32),
        compiler_params=_cparams(("parallel", "parallel", "arbitrary")),
        name="mla_attn",
    )(qm, km, vm_t)


def _group_rms_rows(x):
    c, tm = x.shape
    xg = x.reshape(c // HEAD_DIM, HEAD_DIM, tm)
    ms = jnp.mean(xg * xg, axis=1, keepdims=True)
    return (xg * lax.rsqrt(ms + EPS)).reshape(c, tm)


def _mix_kernel(x_ref, mod_ref, yna_ref, ym_ref, u_ref, up_ref, un_ref, gb_ref, cw_ref, cb_ref,
                og_ref, wout_ref, o_ref):
    i = pl.program_id(1)
    last = pl.num_programs(1) - 1
    u = u_ref[0]
    tm = u.shape[1]
    prev = jnp.where(i > 0, up_ref[0], 0.0)
    nxt = jnp.where(i < last, un_ref[0], 0.0)
    ext = jnp.concatenate([prev, u, nxt], axis=1)
    w = ext.shape[1]
    u_m1 = pltpu.roll(ext, 1, 1)[:, LANES:LANES + tm]
    u_p1 = pltpu.roll(ext, w - 1, 1)[:, LANES:LANES + tm]
    y = cw_ref[0] * u_m1 + cw_ref[1] * u + cw_ref[2] * u_p1 + cb_ref[...]
    yc = gb_ref[0] * y
    mixed = jnp.concatenate([_group_rms_rows(yna_ref[0]), _group_rms_rows(ym_ref[0]),
                             _group_rms_rows(yc)], axis=0)
    mixed = (mixed.T * og_ref[...]).astype(BF16)
    out = jnp.dot(mixed, wout_ref[...], preferred_element_type=F32)
    g1 = mod_ref[0, 2:3, :]
    o_ref[0] = x_ref[0] + g1 * out


def _mix_call(x, mod, yna_t, ym_t, u_t, gb_t, p, tm):
    B, T, D = x.shape
    nb = tm // LANES
    nlb = T // LANES
    chan = lambda c: pl.BlockSpec((1, c, tm), lambda b, i: (b, 0, i))
    consts = [p["conv_w"], p["conv_b"], p["out_g"], p["w_out"]]
    return pl.pallas_call(
        _mix_kernel,
        grid=(B, T // tm),
        in_specs=[
            pl.BlockSpec((1, tm, D), lambda b, i: (b, i, 0)),
            pl.BlockSpec((1, 6, D), lambda b, i: (b, 0, 0)),
            chan(NA_HEADS * HEAD_DIM), chan(MLA_HEADS * MLA_V), chan(CONV_WIDTH),
            pl.BlockSpec((1, CONV_WIDTH, LANES), lambda b, i: (b, 0, jnp.maximum(i * nb - 1, 0))),
            pl.BlockSpec((1, CONV_WIDTH, LANES), lambda b, i: (b, 0, jnp.minimum((i + 1) * nb, nlb - 1))),
            chan(CONV_WIDTH),
        ] + [_const_spec(a.shape) for a in consts],
        out_specs=pl.BlockSpec((1, tm, D), lambda b, i: (b, i, 0)),
        out_shape=jax.ShapeDtypeStruct((B, T, D), F32),
        compiler_params=_cparams(("parallel", "parallel")),
        name="mix_out",
    )(x, mod, yna_t, ym_t, u_t, u_t, u_t, gb_t, *consts)


def _ffn_kernel(x_ref, mod_ref, g2_ref, wg_ref, wu_ref, wd_ref, o_ref, act_ref, *, chunk):
    x = x_ref[0]
    sh = mod_ref[0, 3:4, :]
    sc = mod_ref[0, 4:5, :]
    hb = (_row_rms(x, g2_ref[...]) * (1.0 + sc) + sh).astype(BF16)
    dff = wg_ref.shape[1]
    for c in range(dff // chunk):
        sl = slice(c * chunk, (c + 1) * chunk)
        g = jnp.dot(hb, wg_ref[:, sl], preferred_element_type=F32)
        u = jnp.dot(hb, wu_ref[:, sl], preferred_element_type=F32)
        act_ref[:, sl] = (g * jax.nn.sigmoid(g) * u).astype(BF16)
    out = jnp.dot(act_ref[...], wd_ref[...], preferred_element_type=F32)
    o_ref[0] = x + mod_ref[0, 5:6, :] * out


def _ffn_call(x, mod, p, tm):
    B, T, D = x.shape
    dff = p["w_g"].shape[1]
    consts = [p["g2"], p["w_g"], p["w_u"], p["w_d"]]
    return pl.pallas_call(
        functools.partial(_ffn_kernel, chunk=256),
        grid=(B, T // tm),
        in_specs=[
            pl.BlockSpec((1, tm, D), lambda b, i: (b, i, 0)),
            pl.BlockSpec((1, 6, D), lambda b, i: (b, 0, 0)),
        ] + [_const_spec(a.shape) for a in consts],
        out_specs=pl.BlockSpec((1, tm, D), lambda b, i: (b, i, 0)),
        out_shape=jax.ShapeDtypeStruct((B, T, D), F32),
        scratch_shapes=[pltpu.VMEM((tm, dff), BF16)],
        compiler_params=_cparams(("parallel", "parallel")),
        name="ffn",
    )(x, mod, *consts)


def _pad_heads(w, heads, width):
    k = w.shape[0]
    w = w.reshape(k, heads, width)
    return jnp.pad(w, ((0, 0), (0, 0), (0, LANES - width))).reshape(k, heads * LANES)


def _lane_row(parts):
    row = jnp.zeros((1, LANES), F32)
    for off, v in parts:
        row = row.at[0, off:off + v.shape[0]].set(v.astype(F32))
    return row


def _norm_matrix():
    m = np.zeros((LANES, LANES), np.float32)
    m[:MLA_NOPE, :MLA_NOPE] = 1.0 / MLA_NOPE
    m[MLA_NOPE:MLA_NOPE + MLA_ROPE, MLA_NOPE:MLA_NOPE + MLA_ROPE] = 1.0 / MLA_ROPE
    return jnp.asarray(np.concatenate([m, m], axis=0), BF16)


def _layer_params(l, tm, norm1_g, norm2_g, w_in, na_q_g, na_k_g, mla_q_a_g, mla_kv_a_g, mla_w_uq,
                  mla_w_ukv, mla_qn_g, mla_kn_g, mla_qr_g, mla_kr_g, conv_w, conv_b, out_norm_g,
                  w_out, w_gu, w_down):
    naw = NA_HEADS * HEAD_DIM
    i0 = 3 * naw
    i1 = i0 + MLA_Q_RANK
    i2 = i1 + MLA_KV_RANK
    i3 = i2 + MLA_ROPE
    w = w_in[l]
    d = w.shape[0]
    kr_slab = jnp.pad(w[:, i2:i3], ((0, 0), (MLA_NOPE, LANES - MLA_NOPE - MLA_ROPE)))
    w_in_r = jnp.concatenate([
        _pad_heads(w[:, 0:naw], NA_HEADS, HEAD_DIM),
        _pad_heads(w[:, naw:2 * naw], NA_HEADS, HEAD_DIM),
        w[:, 2 * naw:3 * naw], w[:, i0:i1], w[:, i1:i2], kr_slab, w[:, i3:],
    ], axis=1).astype(BF16)
    ukv = mla_w_ukv[l].reshape(MLA_KV_RANK, MLA_HEADS, MLA_NOPE + MLA_V)
    w_uk = _pad_heads(ukv[:, :, :MLA_NOPE].reshape(MLA_KV_RANK, -1), MLA_HEADS, MLA_NOPE).astype(BF16)
    w_uv = ukv[:, :, MLA_NOPE:].reshape(MLA_KV_RANK, -1).astype(BF16)
    w_uq = _pad_heads(mla_w_uq[l], MLA_HEADS, MLA_NOPE + MLA_ROPE).astype(BF16)
    na_scale = HEAD_DIM ** -0.5
    mla_scale = (MLA_NOPE + MLA_ROPE) ** -0.5
    dff = w_down.shape[1]
    return {
        "g1": norm1_g[l].reshape(1, d), "g2": norm2_g[l].reshape(1, d),
        "w_in": w_in_r, "w_uq": w_uq, "w_uk": w_uk, "w_uv": w_uv, "m2": _norm_matrix(),
        "g_qna": _lane_row([(0, na_q_g[l] * na_scale)]),
        "g_kna": _lane_row([(0, na_k_g[l])]),
        "g_qa": mla_q_a_g[l].reshape(1, -1), "g_kva": mla_kv_a_g[l].reshape(1, -1),
        "g_q": _lane_row([(0, mla_qn_g[l] * mla_scale), (MLA_NOPE, mla_qr_g[l] * mla_scale)]),
        "g_k": _lane_row([(0, mla_kn_g[l])]),
        "g_kr": _lane_row([(MLA_NOPE, mla_kr_g[l])]),
        "conv_w": jnp.broadcast_to(conv_w[l][:, :, None], (3, CONV_WIDTH, tm)),
        "conv_b": jnp.broadcast_to(conv_b[l][:, None], (CONV_WIDTH, tm)),
        "out_g": out_norm_g[l].reshape(1, -1),
        "w_out": w_out[l].astype(BF16),
        "w_g": w_gu[l][:, :dff].astype(BF16),
        "w_u": w_gu[l][:, dff:].astype(BF16),
        "w_d": w_down[l].astype(BF16),
    }


def kernel(x, c, positions, norm1_g, norm2_g, w_ada, b_ada, w_in, na_q_g, na_k_g, na_rpb, mla_q_a_g,
           mla_kv_a_g, mla_w_uq, mla_w_ukv, mla_qn_g, mla_kn_g, mla_qr_g, mla_kr_g, conv_w, conv_b,
           out_norm_g, w_out, w_gu, w_down):
    B, T, D = x.shape
    depth = w_in.shape[0]
    rows = T // GRID_W
    tm = 512
    mod = _ada_modulation(c, w_ada, b_ada)
    cos_t, sin_t = _rope_tables(positions)
    for l in range(depth):
        p = _layer_params(l, tm, norm1_g, norm2_g, w_in, na_q_g, na_k_g, mla_q_a_g, mla_kv_a_g,
                          mla_w_uq, mla_w_ukv, mla_qn_g, mla_kn_g, mla_qr_g, mla_kr_g, conv_w,
                          conv_b, out_norm_g, w_out, w_gu, w_down)
        qna, kna, vna_t, qm, km, vm_t, u_t, gb_t = _proj_call(x, mod[l], cos_t, sin_t, p, tm)
        yna_t = _na_call(qna, kna, vna_t, _na_bias_tables(na_rpb[l], rows))
        ym_t = _mla_call(qm, km, vm_t, tq=512, tk=512)
        x = _mix_call(x, mod[l], yna_t, ym_t, u_t, gb_t, p, tm)
        x = _ffn_call(x, mod[l], p, tm)
    return x
```

```python
import functools

import jax
import jax.numpy as jnp
import numpy as np
from jax import lax
from jax.experimental import pallas as pl
from jax.experimental.pallas import tpu as pltpu

F32 = jnp.float32
BF16 = jnp.bfloat16

GRID_W = 64
HEAD_DIM = 64
NA_HEADS = 4
NA_KR = 8
NA_KC = 16
MLA_HEADS = 8
MLA_NOPE = 64
MLA_ROPE = 32
MLA_V = 64
MLA_Q_RANK = 384
MLA_KV_RANK = 256
CONV_WIDTH = 256
ROPE_THETA = 10000.0
EPS = 1e-6

LANES = 128
NA_Q_ROWS = 8
MLA_VROWS = MLA_V + 16
NA_K_ROWS = 16
MASK_VALUE = -1e30
VMEM_LIMIT = 56 * 1024 * 1024


def _cparams(sem):
    return pltpu.CompilerParams(dimension_semantics=sem, vmem_limit_bytes=VMEM_LIMIT)


def _const_spec(shape):
    nd = len(shape)
    return pl.BlockSpec(shape, lambda *_: (0,) * nd)


def _split_bf16(x):
    hi = x.astype(BF16)
    lo = (x - hi.astype(F32)).astype(BF16)
    return hi, lo


def _ada_kernel(c_ref, w_ref, b_ref, o_ref):
    c = c_ref[...]
    a = c * jax.nn.sigmoid(c)
    a_hi, a_lo = _split_bf16(a)
    w_hi, w_lo = _split_bf16(w_ref[0])
    acc = jnp.dot(a_hi, w_hi, preferred_element_type=F32)
    acc += jnp.dot(a_lo, w_hi, preferred_element_type=F32)
    acc += jnp.dot(a_hi, w_lo, preferred_element_type=F32)
    o_ref[0] = acc + b_ref[0]


def _ada_modulation(c, w_ada, b_ada):
    L, D, N = w_ada.shape
    B = c.shape[0]
    rows = 8
    c_pad = jnp.zeros((rows, D), F32).at[:B].set(c)
    tn = 1536
    out = pl.pallas_call(
        _ada_kernel,
        grid=(L, N // tn),
        in_specs=[
            pl.BlockSpec((rows, D), lambda l, j: (0, 0)),
            pl.BlockSpec((1, D, tn), lambda l, j: (l, 0, j)),
            pl.BlockSpec((1, 1, tn), lambda l, j: (l, 0, j)),
        ],
        out_specs=pl.BlockSpec((1, rows, tn), lambda l, j: (l, 0, j)),
        out_shape=jax.ShapeDtypeStruct((L, rows, N), F32),
        compiler_params=_cparams(("parallel", "parallel")),
        name="ada_mod",
    )(c_pad, w_ada, b_ada.reshape(L, 1, N))
    return out[:, :B].reshape(L, B, 6, D)


def _rope_kernel(pos_ref, inv_ref, sgn_ref, cos_ref, sin_ref):
    ang = pos_ref[0].astype(F32) * inv_ref[...]
    cos_ref[0] = jnp.cos(ang)
    sin_ref[0] = jnp.sin(ang) * sgn_ref[...]


def _rope_tables(positions):
    B, T = positions.shape
    half = MLA_ROPE // 2
    inv = ROPE_THETA ** (-jnp.arange(0, MLA_ROPE, 2, dtype=F32) / MLA_ROPE)
    inv_lane = jnp.zeros((1, LANES), F32)
    inv_lane = inv_lane.at[0, MLA_NOPE:MLA_NOPE + half].set(inv)
    inv_lane = inv_lane.at[0, MLA_NOPE + half:MLA_NOPE + 2 * half].set(inv)
    sgn = np.zeros((1, LANES), np.float32)
    sgn[0, MLA_NOPE:MLA_NOPE + half] = -1.0
    sgn[0, MLA_NOPE + half:MLA_NOPE + 2 * half] = 1.0
    tm = min(T, 2048)
    spec = pl.BlockSpec((1, tm, LANES), lambda b, i: (b, i, 0))
    return pl.pallas_call(
        _rope_kernel,
        grid=(B, T // tm),
        in_specs=[
            pl.BlockSpec((1, tm, 1), lambda b, i: (b, i, 0)),
            _const_spec((1, LANES)),
            _const_spec((1, LANES)),
        ],
        out_specs=[spec, spec],
        out_shape=[jax.ShapeDtypeStruct((B, T, LANES), F32)] * 2,
        compiler_params=_cparams(("parallel", "parallel")),
        name="rope_tables",
    )(positions.reshape(B, T, 1), inv_lane, jnp.asarray(sgn))


def _slab_rms(xs, m2_ref, gain):
    x2 = xs * xs
    hi, lo = _split_bf16(x2)
    ms = jnp.dot(jnp.concatenate([hi, lo], axis=1), m2_ref[...], preferred_element_type=F32)
    return xs * lax.rsqrt(ms + EPS) * gain


def _slab_rope(xs, cos, sin, lane):
    half = MLA_ROPE // 2
    partner = jnp.where(lane < MLA_NOPE + half,
                        pltpu.roll(xs, LANES - half, 1),
                        pltpu.roll(xs, half, 1))
    return xs * cos + partner * sin


def _row_rms(x, gain):
    ms = jnp.mean(x * x, axis=-1, keepdims=True)
    return x * lax.rsqrt(ms + EPS) * gain


def _proj_kernel(x_ref, mod_ref, g1_ref, win_ref, wuq_ref, wuk_ref, wuv_ref, m2_ref,
                 gqna_ref, gkna_ref, gqa_ref, gkva_ref, gq_ref, gk_ref, gkr_ref, cos_ref, sin_ref,
                 qna_ref, kna_ref, vna_ref, qm_ref, km_ref, vm_ref, u_ref, gb_ref):
    x = x_ref[0]
    sh = mod_ref[0, 0:1, :]
    sc = mod_ref[0, 1:2, :]
    h = _row_rms(x, g1_ref[...]) * (1.0 + sc) + sh
    hb = h.astype(BF16)

    def proj(a, b):
        return jnp.dot(hb, win_ref[:, a:b], preferred_element_type=F32)

    nq = NA_HEADS * LANES
    pq = proj(0, nq)
    pk = proj(nq, 2 * nq)
    for hd in range(NA_HEADS):
        sl = slice(hd * LANES, (hd + 1) * LANES)
        qna_ref[0, :, sl] = _slab_rms(pq[:, sl], m2_ref, gqna_ref[...]).astype(BF16)
        kna_ref[0, :, sl] = _slab_rms(pk[:, sl], m2_ref, gkna_ref[...]).astype(BF16)
    o = 2 * nq
    vna_ref[0] = proj(o, o + 256).T.astype(BF16)
    o += 256

    cos = cos_ref[0]
    sin = sin_ref[0]
    lane = lax.broadcasted_iota(jnp.int32, (1, LANES), 1)

    cq = _row_rms(proj(o, o + MLA_Q_RANK), gqa_ref[...]).astype(BF16)
    o += MLA_Q_RANK
    q = jnp.dot(cq, wuq_ref[...], preferred_element_type=F32)
    for hd in range(MLA_HEADS):
        sl = slice(hd * LANES, (hd + 1) * LANES)
        qs = _slab_rms(q[:, sl], m2_ref, gq_ref[...])
        qm_ref[0, :, sl] = _slab_rope(qs, cos, sin, lane).astype(BF16)

    ckv = _row_rms(proj(o, o + MLA_KV_RANK), gkva_ref[...]).astype(BF16)
    o += MLA_KV_RANK
    kr = _slab_rms(proj(o, o + LANES), m2_ref, gkr_ref[...])
    kr = _slab_rope(kr, cos, sin, lane)
    o += LANES
    kn = jnp.dot(ckv, wuk_ref[...], preferred_element_type=F32)
    for hd in range(MLA_HEADS):
        sl = slice(hd * LANES, (hd + 1) * LANES)
        km_ref[0, :, sl] = (_slab_rms(kn[:, sl], m2_ref, gk_ref[...]) + kr).astype(BF16)
    vt = jnp.dot(ckv, wuv_ref[...], preferred_element_type=F32).T.astype(BF16)
    ones = jnp.ones((MLA_VROWS - MLA_V, vt.shape[1]), BF16)
    for hd in range(MLA_HEADS):
        vm_ref[0, hd * MLA_VROWS:hd * MLA_VROWS + MLA_V, :] = vt[hd * MLA_V:(hd + 1) * MLA_V]
        vm_ref[0, hd * MLA_VROWS + MLA_V:(hd + 1) * MLA_VROWS, :] = ones

    cw = CONV_WIDTH
    pc = proj(o, o + 3 * cw)
    u_ref[0] = (pc[:, 2 * cw:3 * cw] * pc[:, 0:cw]).T
    gb_ref[0] = pc[:, cw:2 * cw].T


def _proj_call(x, mod, cos_t, sin_t, p, tm):
    B, T, D = x.shape
    nq = NA_HEADS * LANES
    nm = MLA_HEADS * LANES
    tok = lambda w: pl.BlockSpec((1, tm, w), lambda b, i: (b, i, 0))
    chan = lambda c: pl.BlockSpec((1, c, tm), lambda b, i: (b, 0, i))
    consts = [p["g1"], p["w_in"], p["w_uq"], p["w_uk"], p["w_uv"], p["m2"], p["g_qna"], p["g_kna"],
              p["g_qa"], p["g_kva"], p["g_q"], p["g_k"], p["g_kr"]]
    in_specs = ([tok(D), pl.BlockSpec((1, 6, D), lambda b, i: (b, 0, 0))]
                + [_const_spec(a.shape) for a in consts] + [tok(LANES), tok(LANES)])
    out_shape = [
        jax.ShapeDtypeStruct((B, T, nq), BF16), jax.ShapeDtypeStruct((B, T, nq), BF16),
        jax.ShapeDtypeStruct((B, NA_HEADS * HEAD_DIM, T), BF16),
        jax.ShapeDtypeStruct((B, T, nm), BF16), jax.ShapeDtypeStruct((B, T, nm), BF16),
        jax.ShapeDtypeStruct((B, MLA_HEADS * MLA_VROWS, T), BF16),
        jax.ShapeDtypeStruct((B, CONV_WIDTH, T), F32), jax.ShapeDtypeStruct((B, CONV_WIDTH, T), F32),
    ]
    out_specs = [tok(nq), tok(nq), chan(NA_HEADS * HEAD_DIM), tok(nm), tok(nm),
                 chan(MLA_HEADS * MLA_VROWS), chan(CONV_WIDTH), chan(CONV_WIDTH)]
    return pl.pallas_call(
        _proj_kernel,
        grid=(B, T // tm),
        in_specs=in_specs,
        out_specs=out_specs,
        out_shape=out_shape,
        compiler_params=_cparams(("parallel", "parallel")),
        name="in_proj",
    )(x, mod, *consts, cos_t, sin_t)


def _na_kernel(q_ref, k_ref, v_ref, bias_ref, o_ref, *, rows):
    rb = pl.program_id(2)
    kstart = jnp.clip(rb * NA_Q_ROWS - (NA_K_ROWS - NA_Q_ROWS) // 2, 0, rows - NA_K_ROWS) * GRID_W
    kstart = pl.multiple_of(kstart, 256)
    nk = NA_K_ROWS * GRID_W
    k = k_ref[0, pl.ds(kstart, nk), :]
    s = lax.dot_general(k, q_ref[0], (((1,), (1,)), ((), ())), preferred_element_type=F32)
    s = s + bias_ref[0, 0]
    m = jnp.max(s, axis=0, keepdims=True)
    p = jnp.exp(s - m)
    l = jnp.sum(p, axis=0, keepdims=True)
    v = v_ref[0, :, pl.ds(kstart, nk)]
    o = jnp.dot(v, p.astype(BF16), preferred_element_type=F32)
    o_ref[0] = o / l


def _na_bias_tables(rpb, rows):
    cols = np.arange(GRID_W)
    col_start = np.clip(cols - NA_KC // 2, 0, GRID_W - NA_KC)
    col_ok = (cols[None, :] >= col_start[:, None]) & (cols[None, :] < col_start[:, None] + NA_KC)
    dc = np.clip(cols[None, :] - cols[:, None] + NA_KC - 1, 0, 2 * NA_KC - 2)
    half = (NA_K_ROWS - NA_Q_ROWS) // 2
    sel_r, ok = [], []
    for r0, ks in ((0, 0), (NA_Q_ROWS, NA_Q_ROWS - half), (rows - NA_Q_ROWS, rows - NA_K_ROWS)):
        r = r0 + np.arange(NA_Q_ROWS)
        kr = ks + np.arange(NA_K_ROWS)
        row_start = np.clip(r - NA_KR // 2, 0, rows - NA_KR)
        row_ok = (kr[None, :] >= row_start[:, None]) & (kr[None, :] < row_start[:, None] + NA_KR)
        dr = np.clip(kr[None, :] - r[:, None] + NA_KR - 1, 0, 2 * NA_KR - 2)
        sel_r.append(np.eye(2 * NA_KR - 1, dtype=np.float32)[dr])
        ok.append(row_ok.T[:, None, :, None] & col_ok.T[None, :, None, :])
    sel_r = jnp.asarray(np.stack(sel_r))
    sel_c = jnp.asarray(np.eye(2 * NA_KC - 1, dtype=np.float32)[dc])
    t = jnp.einsum("vabi,hij->vhabj", sel_r, rpb.astype(F32), precision=lax.Precision.HIGHEST)
    b = jnp.einsum("vhabj,qwj->vhbwaq", t, sel_c, precision=lax.Precision.HIGHEST)
    b = jnp.where(jnp.asarray(np.stack(ok))[:, None], b, MASK_VALUE)
    return b.reshape(3, rpb.shape[0], NA_K_ROWS * GRID_W, NA_Q_ROWS * GRID_W)


def _na_call(qna, kna, vna_t, bias):
    B, T, _ = qna.shape
    rows = T // GRID_W
    nrb = rows // NA_Q_ROWS
    nq = NA_Q_ROWS * GRID_W
    nk = NA_K_ROWS * GRID_W

    def bias_map(h, b, rb):
        var = jnp.where(rb == 0, 0, jnp.where(rb == nrb - 1, 2, 1))
        return (var, h, 0, 0)

    return pl.pallas_call(
        functools.partial(_na_kernel, rows=rows),
        grid=(NA_HEADS, B, nrb),
        in_specs=[
            pl.BlockSpec((1, nq, LANES), lambda h, b, rb: (b, rb, h)),
            pl.BlockSpec((1, T, LANES), lambda h, b, rb: (b, 0, h)),
            pl.BlockSpec((1, HEAD_DIM, T), lambda h, b, rb: (b, h, 0)),
            pl.BlockSpec((1, 1, nk, nq), bias_map),
        ],
        out_specs=pl.BlockSpec((1, HEAD_DIM, nq), lambda h, b, rb: (b, h, rb)),
        out_shape=jax.ShapeDtypeStruct((B, NA_HEADS * HEAD_DIM, T), F32),
        compiler_params=_cparams(("parallel", "parallel", "arbitrary")),
        name="na_attn",
    )(qna, kna, vna_t, bias)


def _mla_kernel(q_ref, k_ref, v_ref, o_ref, s_ref, *, tk, sw):
    tq = q_ref.shape[1]
    nkv = k_ref.shape[1] // tk
    strips = [slice(c * sw, (c + 1) * sw) for c in range(tq // sw)]
    ns = len(strips)

    def step(cur_slot, cur_chunk, nxt_slot, nxt_chunk, m, acc):
        def score(c):
            if nxt_chunk is None:
                return
            k = k_ref[0, pl.ds(pl.multiple_of(nxt_chunk * tk, tk), tk), :]
            s_ref[nxt_slot, :, strips[c]] = lax.dot_general(
                k, q_ref[0, strips[c], :], (((1,), (1,)), ((), ())), preferred_element_type=F32)

        v = v_ref[0, :, pl.ds(pl.multiple_of(cur_chunk * tk, tk), tk)]
        ms, accs = [], []
        score(0)
        if ns > 1:
            score(1)
        for c, sl in enumerate(strips):
            s = s_ref[cur_slot, :, sl]
            m_old = m[:, sl]
            m_new = jnp.maximum(m_old, jnp.max(s, axis=0, keepdims=True))
            p = jnp.exp2(s - m_new).astype(BF16)
            pv = jnp.dot(v, p, preferred_element_type=F32)
            accs.append(jnp.exp2(m_old - m_new) * acc[:, sl] + pv)
            ms.append(m_new)
            if c + 2 < ns:
                score(c + 2)
        return jnp.concatenate(ms, axis=1), jnp.concatenate(accs, axis=1)

    def body(jj, carry):
        m, acc = step(0, 2 * jj, 1, 2 * jj + 1, *carry)
        return step(1, 2 * jj + 1, 0, 2 * jj + 2, m, acc)

    m = jnp.full((1, tq), -jnp.inf, F32)
    acc = jnp.zeros((MLA_VROWS, tq), F32)
    k0 = k_ref[0, 0:tk, :]
    for sl in strips:
        s_ref[0, :, sl] = lax.dot_general(k0, q_ref[0, sl, :], (((1,), (1,)), ((), ())),
                                          preferred_element_type=F32)
    m, acc = lax.fori_loop(0, nkv // 2 - 1, body, (m, acc))
    m, acc = step(0, nkv - 2, 1, nkv - 1, m, acc)
    m, acc = step(1, nkv - 1, None, None, m, acc)
    o_ref[0] = acc[:MLA_V] / acc[MLA_V:MLA_V + 1]


def _mla_call(qm, km, vm_t, tq, tk, sw):
    B, T, _ = qm.shape
    return pl.pallas_call(
        functools.partial(_mla_kernel, tk=tk, sw=sw),
        grid=(B, MLA_HEADS, T // tq),
        in_specs=[
            pl.BlockSpec((1, tq, LANES), lambda b, h, i: (b, i, h)),
            pl.BlockSpec((1, T, LANES), lambda b, h, i: (b, 0, h)),
            pl.BlockSpec((1, MLA_VROWS, T), lambda b, h, i: (b, h, 0)),
        ],
        out_specs=pl.BlockSpec((1, MLA_V, tq), lambda b, h, i: (b, h, i)),
        out_shape=jax.ShapeDtypeStruct((B, MLA_HEADS * MLA_V, T), F32),
        scratch_shapes=[pltpu.VMEM((2, tk, tq), F32)],
        compiler_params=_cparams(("parallel", "parallel", "arbitrary")),
        name="mla_attn",
    )(qm, km, vm_t)


def _group_rms_rows(x):
    c, tm = x.shape
    xg = x.reshape(c // HEAD_DIM, HEAD_DIM, tm)
    ms = jnp.mean(xg * xg, axis=1, keepdims=True)
    return (xg * lax.rsqrt(ms + EPS)).reshape(c, tm)


def _mix_kernel(x_ref, mod_ref, yna_ref, ym_ref, u_ref, up_ref, un_ref, gb_ref, cw_ref, cb_ref,
                og_ref, wout_ref, o_ref):
    i = pl.program_id(1)
    last = pl.num_programs(1) - 1
    u = u_ref[0]
    tm = u.shape[1]
    prev = jnp.where(i > 0, up_ref[0], 0.0)
    nxt = jnp.where(i < last, un_ref[0], 0.0)
    ext = jnp.concatenate([prev, u, nxt], axis=1)
    w = ext.shape[1]
    u_m1 = pltpu.roll(ext, 1, 1)[:, LANES:LANES + tm]
    u_p1 = pltpu.roll(ext, w - 1, 1)[:, LANES:LANES + tm]
    y = cw_ref[0] * u_m1 + cw_ref[1] * u + cw_ref[2] * u_p1 + cb_ref[...]
    yc = gb_ref[0] * y
    mixed = jnp.concatenate([_group_rms_rows(yna_ref[0]), _group_rms_rows(ym_ref[0]),
                             _group_rms_rows(yc)], axis=0)
    mixed = (mixed.T * og_ref[...]).astype(BF16)
    out = jnp.dot(mixed, wout_ref[...], preferred_element_type=F32)
    g1 = mod_ref[0, 2:3, :]
    o_ref[0] = x_ref[0] + g1 * out


def _mix_call(x, mod, yna_t, ym_t, u_t, gb_t, p, tm):
    B, T, D = x.shape
    nb = tm // LANES
    nlb = T // LANES
    chan = lambda c: pl.BlockSpec((1, c, tm), lambda b, i: (b, 0, i))
    consts = [p["conv_w"], p["conv_b"], p["out_g"], p["w_out"]]
    return pl.pallas_call(
        _mix_kernel,
        grid=(B, T // tm),
        in_specs=[
            pl.BlockSpec((1, tm, D), lambda b, i: (b, i, 0)),
            pl.BlockSpec((1, 6, D), lambda b, i: (b, 0, 0)),
            chan(NA_HEADS * HEAD_DIM), chan(MLA_HEADS * MLA_V), chan(CONV_WIDTH),
            pl.BlockSpec((1, CONV_WIDTH, LANES), lambda b, i: (b, 0, jnp.maximum(i * nb - 1, 0))),
            pl.BlockSpec((1, CONV_WIDTH, LANES), lambda b, i: (b, 0, jnp.minimum((i + 1) * nb, nlb - 1))),
            chan(CONV_WIDTH),
        ] + [_const_spec(a.shape) for a in consts],
        out_specs=pl.BlockSpec((1, tm, D), lambda b, i: (b, i, 0)),
        out_shape=jax.ShapeDtypeStruct((B, T, D), F32),
        compiler_params=_cparams(("parallel", "parallel")),
        name="mix_out",
    )(x, mod, yna_t, ym_t, u_t, u_t, u_t, gb_t, *consts)


def _ffn_kernel(x_ref, mod_ref, g2_ref, wg_ref, wu_ref, wd_ref, o_ref, act_ref, *, chunk):
    x = x_ref[0]
    sh = mod_ref[0, 3:4, :]
    sc = mod_ref[0, 4:5, :]
    hb = (_row_rms(x, g2_ref[...]) * (1.0 + sc) + sh).astype(BF16)
    dff = wg_ref.shape[1]
    for c in range(dff // chunk):
        sl = slice(c * chunk, (c + 1) * chunk)
        g = jnp.dot(hb, wg_ref[:, sl], preferred_element_type=F32)
        u = jnp.dot(hb, wu_ref[:, sl], preferred_element_type=F32)
        act_ref[:, sl] = (g * jax.nn.sigmoid(g) * u).astype(BF16)
    out = jnp.dot(act_ref[...], wd_ref[...], preferred_element_type=F32)
    o_ref[0] = x + mod_ref[0, 5:6, :] * out


def _ffn_call(x, mod, p, tm):
    B, T, D = x.shape
    dff = p["w_g"].shape[1]
    consts = [p["g2"], p["w_g"], p["w_u"], p["w_d"]]
    return pl.pallas_call(
        functools.partial(_ffn_kernel, chunk=256),
        grid=(B, T // tm),
        in_specs=[
            pl.BlockSpec((1, tm, D), lambda b, i: (b, i, 0)),
            pl.BlockSpec((1, 6, D), lambda b, i: (b, 0, 0)),
        ] + [_const_spec(a.shape) for a in consts],
        out_specs=pl.BlockSpec((1, tm, D), lambda b, i: (b, i, 0)),
        out_shape=jax.ShapeDtypeStruct((B, T, D), F32),
        scratch_shapes=[pltpu.VMEM((tm, dff), BF16)],
        compiler_params=_cparams(("parallel", "parallel")),
        name="ffn",
    )(x, mod, *consts)


def _pad_heads(w, heads, width):
    k = w.shape[0]
    w = w.reshape(k, heads, width)
    return jnp.pad(w, ((0, 0), (0, 0), (0, LANES - width))).reshape(k, heads * LANES)


def _lane_row(parts):
    row = jnp.zeros((1, LANES), F32)
    for off, v in parts:
        row = row.at[0, off:off + v.shape[0]].set(v.astype(F32))
    return row


def _norm_matrix():
    m = np.zeros((LANES, LANES), np.float32)
    m[:MLA_NOPE, :MLA_NOPE] = 1.0 / MLA_NOPE
    m[MLA_NOPE:MLA_NOPE + MLA_ROPE, MLA_NOPE:MLA_NOPE + MLA_ROPE] = 1.0 / MLA_ROPE
    return jnp.asarray(np.concatenate([m, m], axis=0), BF16)


def _layer_params(l, tm, norm1_g, norm2_g, w_in, na_q_g, na_k_g, mla_q_a_g, mla_kv_a_g, mla_w_uq,
                  mla_w_ukv, mla_qn_g, mla_kn_g, mla_qr_g, mla_kr_g, conv_w, conv_b, out_norm_g,
                  w_out, w_gu, w_down):
    naw = NA_HEADS * HEAD_DIM
    i0 = 3 * naw
    i1 = i0 + MLA_Q_RANK
    i2 = i1 + MLA_KV_RANK
    i3 = i2 + MLA_ROPE
    w = w_in[l]
    d = w.shape[0]
    kr_slab = jnp.pad(w[:, i2:i3], ((0, 0), (MLA_NOPE, LANES - MLA_NOPE - MLA_ROPE)))
    w_in_r = jnp.concatenate([
        _pad_heads(w[:, 0:naw], NA_HEADS, HEAD_DIM),
        _pad_heads(w[:, naw:2 * naw], NA_HEADS, HEAD_DIM),
        w[:, 2 * naw:3 * naw], w[:, i0:i1], w[:, i1:i2], kr_slab, w[:, i3:],
    ], axis=1).astype(BF16)
    ukv = mla_w_ukv[l].reshape(MLA_KV_RANK, MLA_HEADS, MLA_NOPE + MLA_V)
    w_uk = _pad_heads(ukv[:, :, :MLA_NOPE].reshape(MLA_KV_RANK, -1), MLA_HEADS, MLA_NOPE).astype(BF16)
    w_uv = ukv[:, :, MLA_NOPE:].reshape(MLA_KV_RANK, -1).astype(BF16)
    w_uq = _pad_heads(mla_w_uq[l], MLA_HEADS, MLA_NOPE + MLA_ROPE).astype(BF16)
    na_scale = HEAD_DIM ** -0.5
    mla_scale = (MLA_NOPE + MLA_ROPE) ** -0.5 * float(np.log2(np.e))
    dff = w_down.shape[1]
    return {
        "g1": norm1_g[l].reshape(1, d), "g2": norm2_g[l].reshape(1, d),
        "w_in": w_in_r, "w_uq": w_uq, "w_uk": w_uk, "w_uv": w_uv, "m2": _norm_matrix(),
        "g_qna": _lane_row([(0, na_q_g[l] * na_scale)]),
        "g_kna": _lane_row([(0, na_k_g[l])]),
        "g_qa": mla_q_a_g[l].reshape(1, -1), "g_kva": mla_kv_a_g[l].reshape(1, -1),
        "g_q": _lane_row([(0, mla_qn_g[l] * mla_scale), (MLA_NOPE, mla_qr_g[l] * mla_scale)]),
        "g_k": _lane_row([(0, mla_kn_g[l])]),
        "g_kr": _lane_row([(MLA_NOPE, mla_kr_g[l])]),
        "conv_w": jnp.broadcast_to(conv_w[l][:, :, None], (3, CONV_WIDTH, tm)),
        "conv_b": jnp.broadcast_to(conv_b[l][:, None], (CONV_WIDTH, tm)),
        "out_g": out_norm_g[l].reshape(1, -1),
        "w_out": w_out[l].astype(BF16),
        "w_g": w_gu[l][:, :dff].astype(BF16),
        "w_u": w_gu[l][:, dff:].astype(BF16),
        "w_d": w_down[l].astype(BF16),
    }


def kernel(x, c, positions, norm1_g, norm2_g, w_ada, b_ada, w_in, na_q_g, na_k_g, na_rpb, mla_q_a_g,
           mla_kv_a_g, mla_w_uq, mla_w_ukv, mla_qn_g, mla_kn_g, mla_qr_g, mla_kr_g, conv_w, conv_b,
           out_norm_g, w_out, w_gu, w_down):
    B, T, D = x.shape
    depth = w_in.shape[0]
    rows = T // GRID_W
    tm = 512
    mod = _ada_modulation(c, w_ada, b_ada)
    cos_t, sin_t = _rope_tables(positions)
    for l in range(depth):
        p = _layer_params(l, tm, norm1_g, norm2_g, w_in, na_q_g, na_k_g, mla_q_a_g, mla_kv_a_g,
                          mla_w_uq, mla_w_ukv, mla_qn_g, mla_kn_g, mla_qr_g, mla_kr_g, conv_w,
                          conv_b, out_norm_g, w_out, w_gu, w_down)
        qna, kna, vna_t, qm, km, vm_t, u_t, gb_t = _proj_call(x, mod[l], cos_t, sin_t, p, tm)
        yna_t = _na_call(qna, kna, vna_t, _na_bias_tables(na_rpb[l], rows))
        ym_t = _mla_call(qm, km, vm_t, tq=1024, tk=512, sw=256)
        x = _mix_call(x, mod[l], yna_t, ym_t, u_t, gb_t, p, tm)
        x = _ffn_call(x, mod[l], p, tm)
    return x
```

```python
import functools

import jax
import jax.numpy as jnp
import numpy as np
from jax import lax
from jax.experimental import pallas as pl
from jax.experimental.pallas import tpu as pltpu

F32 = jnp.float32
BF16 = jnp.bfloat16

GRID_W = 64
HEAD_DIM = 64
NA_HEADS = 4
NA_KR = 8
NA_KC = 16
MLA_HEADS = 8
MLA_NOPE = 64
MLA_ROPE = 32
MLA_V = 64
MLA_Q_RANK = 384
MLA_KV_RANK = 256
CONV_WIDTH = 256
ROPE_THETA = 10000.0
EPS = 1e-6

LANES = 128
NA_Q_ROWS = 8
MLA_VROWS = MLA_V + 16
NA_K_ROWS = 16
MASK_VALUE = -1e30
VMEM_LIMIT = 56 * 1024 * 1024


def _cparams(sem):
    return pltpu.CompilerParams(dimension_semantics=sem, vmem_limit_bytes=VMEM_LIMIT)


def _const_spec(shape):
    nd = len(shape)
    return pl.BlockSpec(shape, lambda *_: (0,) * nd)


def _split_bf16(x):
    hi = x.astype(BF16)
    lo = (x - hi.astype(F32)).astype(BF16)
    return hi, lo


def _ada_kernel(c_ref, w_ref, b_ref, o_ref):
    c = c_ref[...]
    a = c * jax.nn.sigmoid(c)
    a_hi, a_lo = _split_bf16(a)
    w_hi, w_lo = _split_bf16(w_ref[0])
    acc = jnp.dot(a_hi, w_hi, preferred_element_type=F32)
    acc += jnp.dot(a_lo, w_hi, preferred_element_type=F32)
    acc += jnp.dot(a_hi, w_lo, preferred_element_type=F32)
    o_ref[0] = acc + b_ref[0]


def _ada_modulation(c, w_ada, b_ada):
    L, D, N = w_ada.shape
    B = c.shape[0]
    rows = 8
    c_pad = jnp.zeros((rows, D), F32).at[:B].set(c)
    tn = 1536
    out = pl.pallas_call(
        _ada_kernel,
        grid=(L, N // tn),
        in_specs=[
            pl.BlockSpec((rows, D), lambda l, j: (0, 0)),
            pl.BlockSpec((1, D, tn), lambda l, j: (l, 0, j)),
            pl.BlockSpec((1, 1, tn), lambda l, j: (l, 0, j)),
        ],
        out_specs=pl.BlockSpec((1, rows, tn), lambda l, j: (l, 0, j)),
        out_shape=jax.ShapeDtypeStruct((L, rows, N), F32),
        compiler_params=_cparams(("parallel", "parallel")),
        name="ada_mod",
    )(c_pad, w_ada, b_ada.reshape(L, 1, N))
    return out[:, :B].reshape(L, B, 6, D)


def _rope_kernel(pos_ref, inv_ref, sgn_ref, cos_ref, sin_ref):
    ang = pos_ref[0].astype(F32) * inv_ref[...]
    cos_ref[0] = jnp.cos(ang)
    sin_ref[0] = jnp.sin(ang) * sgn_ref[...]


def _rope_tables(positions):
    B, T = positions.shape
    half = MLA_ROPE // 2
    inv = ROPE_THETA ** (-jnp.arange(0, MLA_ROPE, 2, dtype=F32) / MLA_ROPE)
    inv_lane = jnp.zeros((1, LANES), F32)
    inv_lane = inv_lane.at[0, MLA_NOPE:MLA_NOPE + half].set(inv)
    inv_lane = inv_lane.at[0, MLA_NOPE + half:MLA_NOPE + 2 * half].set(inv)
    sgn = np.zeros((1, LANES), np.float32)
    sgn[0, MLA_NOPE:MLA_NOPE + half] = -1.0
    sgn[0, MLA_NOPE + half:MLA_NOPE + 2 * half] = 1.0
    tm = min(T, 2048)
    spec = pl.BlockSpec((1, tm, LANES), lambda b, i: (b, i, 0))
    return pl.pallas_call(
        _rope_kernel,
        grid=(B, T // tm),
        in_specs=[
            pl.BlockSpec((1, tm, 1), lambda b, i: (b, i, 0)),
            _const_spec((1, LANES)),
            _const_spec((1, LANES)),
        ],
        out_specs=[spec, spec],
        out_shape=[jax.ShapeDtypeStruct((B, T, LANES), F32)] * 2,
        compiler_params=_cparams(("parallel", "parallel")),
        name="rope_tables",
    )(positions.reshape(B, T, 1), inv_lane, jnp.asarray(sgn))


def _slab_rms(xs, m2_ref, gain):
    x2 = xs * xs
    hi, lo = _split_bf16(x2)
    ms = jnp.dot(jnp.concatenate([hi, lo], axis=1), m2_ref[...], preferred_element_type=F32)
    return xs * lax.rsqrt(ms + EPS) * gain


def _slab_rope(xs, cos, sin, lane):
    half = MLA_ROPE // 2
    partner = jnp.where(lane < MLA_NOPE + half,
                        pltpu.roll(xs, LANES - half, 1),
                        pltpu.roll(xs, half, 1))
    return xs * cos + partner * sin


def _row_rms(x, gain):
    ms = jnp.mean(x * x, axis=-1, keepdims=True)
    return x * lax.rsqrt(ms + EPS) * gain


def _proj_kernel(x_ref, mod_ref, g1_ref, win_ref, wuq_ref, wuk_ref, wuv_ref, m2_ref,
                 gqna_ref, gkna_ref, gqa_ref, gkva_ref, gq_ref, gk_ref, gkr_ref, cos_ref, sin_ref,
                 qna_ref, kna_ref, vna_ref, qm_ref, km_ref, vm_ref, u_ref, gb_ref):
    x = x_ref[0]
    sh = mod_ref[0, 0:1, :]
    sc = mod_ref[0, 1:2, :]
    h = _row_rms(x, g1_ref[...]) * (1.0 + sc) + sh
    hb = h.astype(BF16)

    def proj(a, b):
        return jnp.dot(hb, win_ref[:, a:b], preferred_element_type=F32)

    nq = NA_HEADS * LANES
    pq = proj(0, nq)
    pk = proj(nq, 2 * nq)
    for hd in range(NA_HEADS):
        sl = slice(hd * LANES, (hd + 1) * LANES)
        qna_ref[0, :, sl] = _slab_rms(pq[:, sl], m2_ref, gqna_ref[...]).astype(BF16)
        kna_ref[0, :, sl] = _slab_rms(pk[:, sl], m2_ref, gkna_ref[...]).astype(BF16)
    o = 2 * nq
    vna_ref[0] = proj(o, o + 256).T.astype(BF16)
    o += 256

    cos = cos_ref[0]
    sin = sin_ref[0]
    lane = lax.broadcasted_iota(jnp.int32, (1, LANES), 1)

    cq = _row_rms(proj(o, o + MLA_Q_RANK), gqa_ref[...]).astype(BF16)
    o += MLA_Q_RANK
    q = jnp.dot(cq, wuq_ref[...], preferred_element_type=F32)
    for hd in range(MLA_HEADS):
        sl = slice(hd * LANES, (hd + 1) * LANES)
        qs = _slab_rms(q[:, sl], m2_ref, gq_ref[...])
        qm_ref[0, :, sl] = _slab_rope(qs, cos, sin, lane).astype(BF16)

    ckv = _row_rms(proj(o, o + MLA_KV_RANK), gkva_ref[...]).astype(BF16)
    o += MLA_KV_RANK
    kr = _slab_rms(proj(o, o + LANES), m2_ref, gkr_ref[...])
    kr = _slab_rope(kr, cos, sin, lane)
    o += LANES
    kn = jnp.dot(ckv, wuk_ref[...], preferred_element_type=F32)
    for hd in range(MLA_HEADS):
        sl = slice(hd * LANES, (hd + 1) * LANES)
        km_ref[0, :, sl] = (_slab_rms(kn[:, sl], m2_ref, gk_ref[...]) + kr).astype(BF16)
    vt = jnp.dot(ckv, wuv_ref[...], preferred_element_type=F32).T.astype(BF16)
    ones = jnp.ones((MLA_VROWS - MLA_V, vt.shape[1]), BF16)
    for hd in range(MLA_HEADS):
        vm_ref[0, hd * MLA_VROWS:hd * MLA_VROWS + MLA_V, :] = vt[hd * MLA_V:(hd + 1) * MLA_V]
        vm_ref[0, hd * MLA_VROWS + MLA_V:(hd + 1) * MLA_VROWS, :] = ones

    cw = CONV_WIDTH
    pc = proj(o, o + 3 * cw)
    u_ref[0] = (pc[:, 2 * cw:3 * cw] * pc[:, 0:cw]).T
    gb_ref[0] = pc[:, cw:2 * cw].T


def _proj_call(x, mod, cos_t, sin_t, p, tm):
    B, T, D = x.shape
    nq = NA_HEADS * LANES
    nm = MLA_HEADS * LANES
    tok = lambda w: pl.BlockSpec((1, tm, w), lambda b, i: (b, i, 0))
    chan = lambda c: pl.BlockSpec((1, c, tm), lambda b, i: (b, 0, i))
    consts = [p["g1"], p["w_in"], p["w_uq"], p["w_uk"], p["w_uv"], p["m2"], p["g_qna"], p["g_kna"],
              p["g_qa"], p["g_kva"], p["g_q"], p["g_k"], p["g_kr"]]
    in_specs = ([tok(D), pl.BlockSpec((1, 6, D), lambda b, i: (b, 0, 0))]
                + [_const_spec(a.shape) for a in consts] + [tok(LANES), tok(LANES)])
    out_shape = [
        jax.ShapeDtypeStruct((B, T, nq), BF16), jax.ShapeDtypeStruct((B, T, nq), BF16),
        jax.ShapeDtypeStruct((B, NA_HEADS * HEAD_DIM, T), BF16),
        jax.ShapeDtypeStruct((B, T, nm), BF16), jax.ShapeDtypeStruct((B, T, nm), BF16),
        jax.ShapeDtypeStruct((B, MLA_HEADS * MLA_VROWS, T), BF16),
        jax.ShapeDtypeStruct((B, CONV_WIDTH, T), F32), jax.ShapeDtypeStruct((B, CONV_WIDTH, T), F32),
    ]
    out_specs = [tok(nq), tok(nq), chan(NA_HEADS * HEAD_DIM), tok(nm), tok(nm),
                 chan(MLA_HEADS * MLA_VROWS), chan(CONV_WIDTH), chan(CONV_WIDTH)]
    return pl.pallas_call(
        _proj_kernel,
        grid=(B, T // tm),
        in_specs=in_specs,
        out_specs=out_specs,
        out_shape=out_shape,
        compiler_params=_cparams(("parallel", "parallel")),
        name="in_proj",
    )(x, mod, *consts, cos_t, sin_t)


def _na_kernel(q_ref, k_ref, v_ref, bias_ref, o_ref, *, rows):
    rb = pl.program_id(2)
    kstart = jnp.clip(rb * NA_Q_ROWS - (NA_K_ROWS - NA_Q_ROWS) // 2, 0, rows - NA_K_ROWS) * GRID_W
    kstart = pl.multiple_of(kstart, 256)
    nk = NA_K_ROWS * GRID_W
    k = k_ref[0, pl.ds(kstart, nk), :]
    s = lax.dot_general(k, q_ref[0], (((1,), (1,)), ((), ())), preferred_element_type=F32)
    s = s + bias_ref[0, 0]
    m = jnp.max(s, axis=0, keepdims=True)
    p = jnp.exp(s - m)
    l = jnp.sum(p, axis=0, keepdims=True)
    v = v_ref[0, :, pl.ds(kstart, nk)]
    o = jnp.dot(v, p.astype(BF16), preferred_element_type=F32)
    o_ref[0] = o / l


def _na_bias_tables(rpb, rows):
    cols = np.arange(GRID_W)
    col_start = np.clip(cols - NA_KC // 2, 0, GRID_W - NA_KC)
    col_ok = (cols[None, :] >= col_start[:, None]) & (cols[None, :] < col_start[:, None] + NA_KC)
    dc = np.clip(cols[None, :] - cols[:, None] + NA_KC - 1, 0, 2 * NA_KC - 2)
    half = (NA_K_ROWS - NA_Q_ROWS) // 2
    sel_r, ok = [], []
    for r0, ks in ((0, 0), (NA_Q_ROWS, NA_Q_ROWS - half), (rows - NA_Q_ROWS, rows - NA_K_ROWS)):
        r = r0 + np.arange(NA_Q_ROWS)
        kr = ks + np.arange(NA_K_ROWS)
        row_start = np.clip(r - NA_KR // 2, 0, rows - NA_KR)
        row_ok = (kr[None, :] >= row_start[:, None]) & (kr[None, :] < row_start[:, None] + NA_KR)
        dr = np.clip(kr[None, :] - r[:, None] + NA_KR - 1, 0, 2 * NA_KR - 2)
        sel_r.append(np.eye(2 * NA_KR - 1, dtype=np.float32)[dr])
        ok.append(row_ok.T[:, None, :, None] & col_ok.T[None, :, None, :])
    sel_r = jnp.asarray(np.stack(sel_r))
    sel_c = jnp.asarray(np.eye(2 * NA_KC - 1, dtype=np.float32)[dc])
    t = jnp.einsum("vabi,hij->vhabj", sel_r, rpb.astype(F32), precision=lax.Precision.HIGHEST)
    b = jnp.einsum("vhabj,qwj->vhbwaq", t, sel_c, precision=lax.Precision.HIGHEST)
    b = jnp.where(jnp.asarray(np.stack(ok))[:, None], b, MASK_VALUE)
    return b.reshape(3, rpb.shape[0], NA_K_ROWS * GRID_W, NA_Q_ROWS * GRID_W)


def _na_call(qna, kna, vna_t, bias):
    B, T, _ = qna.shape
    rows = T // GRID_W
    nrb = rows // NA_Q_ROWS
    nq = NA_Q_ROWS * GRID_W
    nk = NA_K_ROWS * GRID_W

    def bias_map(h, b, rb):
        var = jnp.where(rb == 0, 0, jnp.where(rb == nrb - 1, 2, 1))
        return (var, h, 0, 0)

    return pl.pallas_call(
        functools.partial(_na_kernel, rows=rows),
        grid=(NA_HEADS, B, nrb),
        in_specs=[
            pl.BlockSpec((1, nq, LANES), lambda h, b, rb: (b, rb, h)),
            pl.BlockSpec((1, T, LANES), lambda h, b, rb: (b, 0, h)),
            pl.BlockSpec((1, HEAD_DIM, T), lambda h, b, rb: (b, h, 0)),
            pl.BlockSpec((1, 1, nk, nq), bias_map),
        ],
        out_specs=pl.BlockSpec((1, HEAD_DIM, nq), lambda h, b, rb: (b, h, rb)),
        out_shape=jax.ShapeDtypeStruct((B, NA_HEADS * HEAD_DIM, T), F32),
        compiler_params=_cparams(("parallel", "parallel", "arbitrary")),
        name="na_attn",
    )(qna, kna, vna_t, bias)


def _mla_kernel(q_ref, k_ref, v_ref, o_ref, s_ref, mx_ref, *, tq, tk, sw, unroll):
    T = k_ref.shape[1]
    nq = T // tq
    nkv = T // tk
    strips = [slice(c * sw, (c + 1) * sw) for c in range(tq // sw)]
    ns = len(strips)
    nt = (((1,), (1,)), ((), ()))

    def score(slot, qi, j, c):
        k = k_ref[0, pl.ds(pl.multiple_of(j * tk, tk), tk), :]
        q = q_ref[0, pl.ds(pl.multiple_of(qi * tq + c * sw, sw), sw), :]
        s = lax.dot_general(k, q, nt, preferred_element_type=F32)
        s_ref[slot, :, strips[c]] = s
        mx_ref[slot, :, strips[c]] = jnp.max(s, axis=0, keepdims=True)

    def step(slot, j, nxt_qi, nxt_j, m, acc):
        v = v_ref[0, :, pl.ds(pl.multiple_of(j * tk, tk), tk)]
        ms, accs = [], []
        for c in range(min(2, ns)):
            score(1 - slot, nxt_qi, nxt_j, c)
        for c, sl in enumerate(strips):
            m_old = m[:, sl]
            m_new = jnp.maximum(m_old, mx_ref[slot, :, sl])
            p = jnp.exp2(s_ref[slot, :, sl] - m_new).astype(BF16)
            pv = jnp.dot(v, p, preferred_element_type=F32)
            accs.append(jnp.exp2(m_old - m_new) * acc[:, sl] + pv)
            ms.append(m_new)
            if c + 2 < ns:
                score(1 - slot, nxt_qi, nxt_j, c + 2)
        return jnp.concatenate(ms, axis=1), jnp.concatenate(accs, axis=1)

    def body(bi, carry):
        m, acc = carry
        t0 = bi * unroll
        qi = t0 // nkv
        j0 = t0 % nkv
        fresh = j0 == 0
        m = jnp.where(fresh, -jnp.inf, m)
        acc = jnp.where(fresh, 0.0, acc)
        for u in range(unroll):
            if u + 1 < unroll:
                nxt_qi, nxt_j = qi, j0 + u + 1
            else:
                wrap = j0 + unroll == nkv
                nxt_qi = jnp.minimum(qi + wrap.astype(jnp.int32), nq - 1)
                nxt_j = jnp.where(wrap, 0, j0 + unroll)
            m, acc = step(u % 2, j0 + u, nxt_qi, nxt_j, m, acc)

        @pl.when(j0 + unroll == nkv)
        def _():
            o_ref[0, :, pl.ds(pl.multiple_of(qi * tq, tq), tq)] = acc[:MLA_V] / acc[MLA_V:MLA_V + 1]

        return m, acc

    for c in range(ns):
        score(0, 0, 0, c)
    init = (jnp.full((1, tq), -jnp.inf, F32), jnp.zeros((MLA_VROWS, tq), F32))
    lax.fori_loop(0, nq * nkv // unroll, body, init)


def _mla_call(qm, km, vm_t, tq, tk, sw, unroll):
    B, T, _ = qm.shape
    assert unroll % 2 == 0 and (T // tk) % unroll == 0 and T % tq == 0 and tq % sw == 0
    return pl.pallas_call(
        functools.partial(_mla_kernel, tq=tq, tk=tk, sw=sw, unroll=unroll),
        grid=(B, MLA_HEADS),
        in_specs=[
            pl.BlockSpec((1, T, LANES), lambda b, h: (b, 0, h)),
            pl.BlockSpec((1, T, LANES), lambda b, h: (b, 0, h)),
            pl.BlockSpec((1, MLA_VROWS, T), lambda b, h: (b, h, 0)),
        ],
        out_specs=pl.BlockSpec((1, MLA_V, T), lambda b, h: (b, h, 0)),
        out_shape=jax.ShapeDtypeStruct((B, MLA_HEADS * MLA_V, T), F32),
        scratch_shapes=[pltpu.VMEM((2, tk, tq), F32), pltpu.VMEM((2, 1, tq), F32)],
        compiler_params=_cparams(("parallel", "parallel")),
        name="mla_attn",
    )(qm, km, vm_t)


def _group_rms_rows(x):
    c, tm = x.shape
    xg = x.reshape(c // HEAD_DIM, HEAD_DIM, tm)
    ms = jnp.mean(xg * xg, axis=1, keepdims=True)
    return (xg * lax.rsqrt(ms + EPS)).reshape(c, tm)


def _mix_kernel(x_ref, mod_ref, yna_ref, ym_ref, u_ref, up_ref, un_ref, gb_ref, cw_ref, cb_ref,
                og_ref, wout_ref, o_ref):
    i = pl.program_id(1)
    last = pl.num_programs(1) - 1
    u = u_ref[0]
    tm = u.shape[1]
    prev = jnp.where(i > 0, up_ref[0], 0.0)
    nxt = jnp.where(i < last, un_ref[0], 0.0)
    ext = jnp.concatenate([prev, u, nxt], axis=1)
    w = ext.shape[1]
    u_m1 = pltpu.roll(ext, 1, 1)[:, LANES:LANES + tm]
    u_p1 = pltpu.roll(ext, w - 1, 1)[:, LANES:LANES + tm]
    y = cw_ref[0] * u_m1 + cw_ref[1] * u + cw_ref[2] * u_p1 + cb_ref[...]
    yc = gb_ref[0] * y
    mixed = jnp.concatenate([_group_rms_rows(yna_ref[0]), _group_rms_rows(ym_ref[0]),
                             _group_rms_rows(yc)], axis=0)
    mixed = (mixed.T * og_ref[...]).astype(BF16)
    out = jnp.dot(mixed, wout_ref[...], preferred_element_type=F32)
    g1 = mod_ref[0, 2:3, :]
    o_ref[0] = x_ref[0] + g1 * out


def _mix_call(x, mod, yna_t, ym_t, u_t, gb_t, p, tm):
    B, T, D = x.shape
    nb = tm // LANES
    nlb = T // LANES
    chan = lambda c: pl.BlockSpec((1, c, tm), lambda b, i: (b, 0, i))
    consts = [p["conv_w"], p["conv_b"], p["out_g"], p["w_out"]]
    return pl.pallas_call(
        _mix_kernel,
        grid=(B, T // tm),
        in_specs=[
            pl.BlockSpec((1, tm, D), lambda b, i: (b, i, 0)),
            pl.BlockSpec((1, 6, D), lambda b, i: (b, 0, 0)),
            chan(NA_HEADS * HEAD_DIM), chan(MLA_HEADS * MLA_V), chan(CONV_WIDTH),
            pl.BlockSpec((1, CONV_WIDTH, LANES), lambda b, i: (b, 0, jnp.maximum(i * nb - 1, 0))),
            pl.BlockSpec((1, CONV_WIDTH, LANES), lambda b, i: (b, 0, jnp.minimum((i + 1) * nb, nlb - 1))),
            chan(CONV_WIDTH),
        ] + [_const_spec(a.shape) for a in consts],
        out_specs=pl.BlockSpec((1, tm, D), lambda b, i: (b, i, 0)),
        out_shape=jax.ShapeDtypeStruct((B, T, D), F32),
        compiler_params=_cparams(("parallel", "parallel")),
        name="mix_out",
    )(x, mod, yna_t, ym_t, u_t, u_t, u_t, gb_t, *consts)


def _ffn_kernel(x_ref, mod_ref, g2_ref, wg_ref, wu_ref, wd_ref, o_ref, act_ref, *, chunk):
    x = x_ref[0]
    sh = mod_ref[0, 3:4, :]
    sc = mod_ref[0, 4:5, :]
    hb = (_row_rms(x, g2_ref[...]) * (1.0 + sc) + sh).astype(BF16)
    dff = wg_ref.shape[1]
    for c in range(dff // chunk):
        sl = slice(c * chunk, (c + 1) * chunk)
        g = jnp.dot(hb, wg_ref[:, sl], preferred_element_type=F32)
        u = jnp.dot(hb, wu_ref[:, sl], preferred_element_type=F32)
        act_ref[:, sl] = (g * jax.nn.sigmoid(g) * u).astype(BF16)
    out = jnp.dot(act_ref[...], wd_ref[...], preferred_element_type=F32)
    o_ref[0] = x + mod_ref[0, 5:6, :] * out


def _ffn_call(x, mod, p, tm):
    B, T, D = x.shape
    dff = p["w_g"].shape[1]
    consts = [p["g2"], p["w_g"], p["w_u"], p["w_d"]]
    return pl.pallas_call(
        functools.partial(_ffn_kernel, chunk=256),
        grid=(B, T // tm),
        in_specs=[
            pl.BlockSpec((1, tm, D), lambda b, i: (b, i, 0)),
            pl.BlockSpec((1, 6, D), lambda b, i: (b, 0, 0)),
        ] + [_const_spec(a.shape) for a in consts],
        out_specs=pl.BlockSpec((1, tm, D), lambda b, i: (b, i, 0)),
        out_shape=jax.ShapeDtypeStruct((B, T, D), F32),
        scratch_shapes=[pltpu.VMEM((tm, dff), BF16)],
        compiler_params=_cparams(("parallel", "parallel")),
        name="ffn",
    )(x, mod, *consts)


def _pad_heads(w, heads, width):
    k = w.shape[0]
    w = w.reshape(k, heads, width)
    return jnp.pad(w, ((0, 0), (0, 0), (0, LANES - width))).reshape(k, heads * LANES)


def _lane_row(parts):
    row = jnp.zeros((1, LANES), F32)
    for off, v in parts:
        row = row.at[0, off:off + v.shape[0]].set(v.astype(F32))
    return row


def _norm_matrix():
    m = np.zeros((LANES, LANES), np.float32)
    m[:MLA_NOPE, :MLA_NOPE] = 1.0 / MLA_NOPE
    m[MLA_NOPE:MLA_NOPE + MLA_ROPE, MLA_NOPE:MLA_NOPE + MLA_ROPE] = 1.0 / MLA_ROPE
    return jnp.asarray(np.concatenate([m, m], axis=0), BF16)


def _layer_params(l, tm, norm1_g, norm2_g, w_in, na_q_g, na_k_g, mla_q_a_g, mla_kv_a_g, mla_w_uq,
                  mla_w_ukv, mla_qn_g, mla_kn_g, mla_qr_g, mla_kr_g, conv_w, conv_b, out_norm_g,
                  w_out, w_gu, w_down):
    naw = NA_HEADS * HEAD_DIM
    i0 = 3 * naw
    i1 = i0 + MLA_Q_RANK
    i2 = i1 + MLA_KV_RANK
    i3 = i2 + MLA_ROPE
    w = w_in[l]
    d = w.shape[0]
    kr_slab = jnp.pad(w[:, i2:i3], ((0, 0), (MLA_NOPE, LANES - MLA_NOPE - MLA_ROPE)))
    w_in_r = jnp.concatenate([
        _pad_heads(w[:, 0:naw], NA_HEADS, HEAD_DIM),
        _pad_heads(w[:, naw:2 * naw], NA_HEADS, HEAD_DIM),
        w[:, 2 * naw:3 * naw], w[:, i0:i1], w[:, i1:i2], kr_slab, w[:, i3:],
    ], axis=1).astype(BF16)
    ukv = mla_w_ukv[l].reshape(MLA_KV_RANK, MLA_HEADS, MLA_NOPE + MLA_V)
    w_uk = _pad_heads(ukv[:, :, :MLA_NOPE].reshape(MLA_KV_RANK, -1), MLA_HEADS, MLA_NOPE).astype(BF16)
    w_uv = ukv[:, :, MLA_NOPE:].reshape(MLA_KV_RANK, -1).astype(BF16)
    w_uq = _pad_heads(mla_w_uq[l], MLA_HEADS, MLA_NOPE + MLA_ROPE).astype(BF16)
    na_scale = HEAD_DIM ** -0.5
    mla_scale = (MLA_NOPE + MLA_ROPE) ** -0.5 * float(np.log2(np.e))
    dff = w_down.shape[1]
    return {
        "g1": norm1_g[l].reshape(1, d), "g2": norm2_g[l].reshape(1, d),
        "w_in": w_in_r, "w_uq": w_uq, "w_uk": w_uk, "w_uv": w_uv, "m2": _norm_matrix(),
        "g_qna": _lane_row([(0, na_q_g[l] * na_scale)]),
        "g_kna": _lane_row([(0, na_k_g[l])]),
        "g_qa": mla_q_a_g[l].reshape(1, -1), "g_kva": mla_kv_a_g[l].reshape(1, -1),
        "g_q": _lane_row([(0, mla_qn_g[l] * mla_scale), (MLA_NOPE, mla_qr_g[l] * mla_scale)]),
        "g_k": _lane_row([(0, mla_kn_g[l])]),
        "g_kr": _lane_row([(MLA_NOPE, mla_kr_g[l])]),
        "conv_w": jnp.broadcast_to(conv_w[l][:, :, None], (3, CONV_WIDTH, tm)),
        "conv_b": jnp.broadcast_to(conv_b[l][:, None], (CONV_WIDTH, tm)),
        "out_g": out_norm_g[l].reshape(1, -1),
        "w_out": w_out[l].astype(BF16),
        "w_g": w_gu[l][:, :dff].astype(BF16),
        "w_u": w_gu[l][:, dff:].astype(BF16),
        "w_d": w_down[l].astype(BF16),
    }


def kernel(x, c, positions, norm1_g, norm2_g, w_ada, b_ada, w_in, na_q_g, na_k_g, na_rpb, mla_q_a_g,
           mla_kv_a_g, mla_w_uq, mla_w_ukv, mla_qn_g, mla_kn_g, mla_qr_g, mla_kr_g, conv_w, conv_b,
           out_norm_g, w_out, w_gu, w_down):
    B, T, D = x.shape
    depth = w_in.shape[0]
    rows = T // GRID_W
    tm = 512
    mod = _ada_modulation(c, w_ada, b_ada)
    cos_t, sin_t = _rope_tables(positions)
    for l in range(depth):
        p = _layer_params(l, tm, norm1_g, norm2_g, w_in, na_q_g, na_k_g, mla_q_a_g, mla_kv_a_g,
                          mla_w_uq, mla_w_ukv, mla_qn_g, mla_kn_g, mla_qr_g, mla_kr_g, conv_w,
                          conv_b, out_norm_g, w_out, w_gu, w_down)
        qna, kna, vna_t, qm, km, vm_t, u_t, gb_t = _proj_call(x, mod[l], cos_t, sin_t, p, tm)
        yna_t = _na_call(qna, kna, vna_t, _na_bias_tables(na_rpb[l], rows))
        ym_t = _mla_call(qm, km, vm_t, tq=1024, tk=512, sw=256, unroll=4)
        x = _mix_call(x, mod[l], yna_t, ym_t, u_t, gb_t, p, tm)
        x = _ffn_call(x, mod[l], p, tm)
    return x
```

```python
import functools

import jax
import jax.numpy as jnp
import numpy as np
from jax import lax
from jax.experimental import pallas as pl
from jax.experimental.pallas import tpu as pltpu

F32 = jnp.float32
BF16 = jnp.bfloat16

GRID_W = 64
HEAD_DIM = 64
NA_HEADS = 4
NA_KR = 8
NA_KC = 16
MLA_HEADS = 8
MLA_NOPE = 64
MLA_ROPE = 32
MLA_V = 64
MLA_Q_RANK = 384
MLA_KV_RANK = 256
CONV_WIDTH = 256
ROPE_THETA = 10000.0
EPS = 1e-6

LANES = 128
NA_Q_ROWS = 8
ONES_ROWS = 16
MLA_VROWS = MLA_V + ONES_ROWS
NA_VROWS = HEAD_DIM + ONES_ROWS
NA_K_ROWS = 16
MLA_TK = 512
MASK_VALUE = -1e30
VMEM_LIMIT = 56 * 1024 * 1024


def _cparams(sem):
    return pltpu.CompilerParams(dimension_semantics=sem, vmem_limit_bytes=VMEM_LIMIT)


def _const_spec(shape):
    nd = len(shape)
    return pl.BlockSpec(shape, lambda *_: (0,) * nd, pipeline_mode=pl.Buffered(1))


def _split_bf16(x):
    hi = x.astype(BF16)
    lo = (x - hi.astype(F32)).astype(BF16)
    return hi, lo


def _ada_kernel(c_ref, w_ref, b_ref, o_ref):
    c = c_ref[...]
    a = c * jax.nn.sigmoid(c)
    a_hi, a_lo = _split_bf16(a)
    w_hi, w_lo = _split_bf16(w_ref[0])
    acc = jnp.dot(a_hi, w_hi, preferred_element_type=F32)
    acc += jnp.dot(a_lo, w_hi, preferred_element_type=F32)
    acc += jnp.dot(a_hi, w_lo, preferred_element_type=F32)
    o_ref[0] = acc + b_ref[0]


def _ada_modulation(c, w_ada, b_ada):
    L, D, N = w_ada.shape
    B = c.shape[0]
    rows = 8
    c_pad = jnp.zeros((rows, D), F32).at[:B].set(c)
    tn = 1536
    out = pl.pallas_call(
        _ada_kernel,
        grid=(L, N // tn),
        in_specs=[
            pl.BlockSpec((rows, D), lambda l, j: (0, 0)),
            pl.BlockSpec((1, D, tn), lambda l, j: (l, 0, j)),
            pl.BlockSpec((1, 1, tn), lambda l, j: (l, 0, j)),
        ],
        out_specs=pl.BlockSpec((1, rows, tn), lambda l, j: (l, 0, j)),
        out_shape=jax.ShapeDtypeStruct((L, rows, N), F32),
        compiler_params=_cparams(("parallel", "parallel")),
        name="ada_mod",
    )(c_pad, w_ada, b_ada.reshape(L, 1, N))
    return out[:, :B].reshape(L, B, 6, D)


def _rope_kernel(pos_ref, inv_ref, sgn_ref, cos_ref, sin_ref):
    ang = pos_ref[0].astype(F32) * inv_ref[...]
    cos_ref[0] = jnp.cos(ang)
    sin_ref[0] = jnp.sin(ang) * sgn_ref[...]


def _rope_tables(positions):
    B, T = positions.shape
    half = MLA_ROPE // 2
    inv = ROPE_THETA ** (-jnp.arange(0, MLA_ROPE, 2, dtype=F32) / MLA_ROPE)
    inv_lane = jnp.zeros((1, LANES), F32)
    inv_lane = inv_lane.at[0, MLA_NOPE:MLA_NOPE + half].set(inv)
    inv_lane = inv_lane.at[0, MLA_NOPE + half:MLA_NOPE + 2 * half].set(inv)
    sgn = np.zeros((1, LANES), np.float32)
    sgn[0, MLA_NOPE:MLA_NOPE + half] = -1.0
    sgn[0, MLA_NOPE + half:MLA_NOPE + 2 * half] = 1.0
    tm = min(T, 2048)
    spec = pl.BlockSpec((1, tm, LANES), lambda b, i: (b, i, 0))
    return pl.pallas_call(
        _rope_kernel,
        grid=(B, T // tm),
        in_specs=[
            pl.BlockSpec((1, tm, 1), lambda b, i: (b, i, 0)),
            _const_spec((1, LANES)),
            _const_spec((1, LANES)),
        ],
        out_specs=[spec, spec],
        out_shape=[jax.ShapeDtypeStruct((B, T, LANES), F32)] * 2,
        compiler_params=_cparams(("parallel", "parallel")),
        name="rope_tables",
    )(positions.reshape(B, T, 1), inv_lane, jnp.asarray(sgn))


def _pair_rms(x, m_ref, gain):
    ms = jnp.dot((x * x).astype(BF16), m_ref[...], preferred_element_type=F32)
    return x * lax.rsqrt(ms + EPS) * gain


def _slab_rope(xs, cos, sin, first_half):
    half = MLA_ROPE // 2
    w = xs.shape[1]
    partner = jnp.where(first_half,
                        pltpu.roll(xs, w - half, 1),
                        pltpu.roll(xs, half, 1))
    return xs * cos + partner * sin


def _store_values(v_ref, vt, heads, width):
    ones = jnp.ones((ONES_ROWS, vt.shape[1]), BF16)
    rows = width + ONES_ROWS
    for hd in range(heads):
        v_ref[0, hd * rows:hd * rows + width, :] = vt[hd * width:(hd + 1) * width]
        v_ref[0, hd * rows + width:(hd + 1) * rows, :] = ones


def _row_rms(x, gain):
    ms = jnp.mean(x * x, axis=-1, keepdims=True)
    return x * lax.rsqrt(ms + EPS) * gain


def _proj_kernel(x_ref, mod_ref, g1_ref, win_ref, wuq_ref, wuk_ref, wuv_ref, m2_ref, mna_ref,
                 gqna_ref, gkna_ref, gqa_ref, gkva_ref, gq_ref, gk_ref, gkr_ref, cos_ref, sin_ref,
                 qna_ref, kna_ref, vna_ref, qm_ref, km_ref, vm_ref, u_ref, gb_ref):
    x = x_ref[0]
    sh = mod_ref[0, 0:1, :]
    sc = mod_ref[0, 1:2, :]
    h = _row_rms(x, g1_ref[...]) * (1.0 + sc) + sh
    hb = h.astype(BF16)

    def proj(a, b):
        return jnp.dot(hb, win_ref[:, a:b], preferred_element_type=F32)

    nq = NA_HEADS * HEAD_DIM
    pair = 2 * LANES
    qna_ref[0] = _pair_rms(proj(0, nq), mna_ref, gqna_ref[...]).astype(BF16)
    kna_ref[0] = _pair_rms(proj(nq, 2 * nq), mna_ref, gkna_ref[...]).astype(BF16)
    o = 2 * nq
    _store_values(vna_ref, proj(o, o + nq).T.astype(BF16), NA_HEADS, HEAD_DIM)
    o += nq

    cos = cos_ref[0]
    sin = sin_ref[0]
    cos2 = jnp.concatenate([cos, cos], axis=1)
    sin2 = jnp.concatenate([sin, sin], axis=1)
    lane = lax.broadcasted_iota(jnp.int32, (1, pair), 1) % LANES
    first_half = lane < MLA_NOPE + MLA_ROPE // 2

    cq = _row_rms(proj(o, o + MLA_Q_RANK), gqa_ref[...]).astype(BF16)
    o += MLA_Q_RANK
    q = jnp.dot(cq, wuq_ref[...], preferred_element_type=F32)
    for s0 in range(0, MLA_HEADS * LANES, pair):
        qs = _pair_rms(q[:, s0:s0 + pair], m2_ref, gq_ref[...])
        qm_ref[0, :, s0:s0 + pair] = _slab_rope(qs, cos2, sin2, first_half).astype(BF16)

    ckv = _row_rms(proj(o, o + MLA_KV_RANK), gkva_ref[...]).astype(BF16)
    o += MLA_KV_RANK
    kr = proj(o, o + LANES)
    kr = kr * lax.rsqrt(jnp.sum(kr * kr, axis=-1, keepdims=True) * (1.0 / MLA_ROPE) + EPS) * gkr_ref[...]
    kr = _slab_rope(kr, cos, sin, first_half[:, :LANES])
    kr2 = jnp.concatenate([kr, kr], axis=1)
    o += LANES
    kn = jnp.dot(ckv, wuk_ref[...], preferred_element_type=F32)
    for s0 in range(0, MLA_HEADS * LANES, pair):
        km_ref[0, :, s0:s0 + pair] = (_pair_rms(kn[:, s0:s0 + pair], m2_ref, gk_ref[...]) + kr2).astype(BF16)
    vt = jnp.dot(ckv, wuv_ref[...], preferred_element_type=F32).T.astype(BF16)
    _store_values(vm_ref, vt, MLA_HEADS, MLA_V)

    cw = CONV_WIDTH
    pc = proj(o, o + 3 * cw)
    u_ref[0] = (pc[:, 2 * cw:3 * cw] * pc[:, 0:cw]).T
    gb_ref[0] = pc[:, cw:2 * cw].T


def _proj_call(x, mod, cos_t, sin_t, p, tm):
    B, T, D = x.shape
    nq = NA_HEADS * HEAD_DIM
    nm = MLA_HEADS * LANES
    tok = lambda w: pl.BlockSpec((1, tm, w), lambda b, i: (b, i, 0))
    chan = lambda c: pl.BlockSpec((1, c, tm), lambda b, i: (b, 0, i))
    consts = [p["g1"], p["w_in"], p["w_uq"], p["w_uk"], p["w_uv"], p["m2"], p["m_na"], p["g_qna"],
              p["g_kna"], p["g_qa"], p["g_kva"], p["g_q"], p["g_k"], p["g_kr"]]
    in_specs = ([tok(D), pl.BlockSpec((1, 6, D), lambda b, i: (b, 0, 0))]
                + [_const_spec(a.shape) for a in consts] + [tok(LANES), tok(LANES)])
    out_shape = [
        jax.ShapeDtypeStruct((B, T, nq), BF16), jax.ShapeDtypeStruct((B, T, nq), BF16),
        jax.ShapeDtypeStruct((B, NA_HEADS * NA_VROWS, T), BF16),
        jax.ShapeDtypeStruct((B, T, nm), BF16), jax.ShapeDtypeStruct((B, T, nm), BF16),
        jax.ShapeDtypeStruct((B, MLA_HEADS * MLA_VROWS, T), BF16),
        jax.ShapeDtypeStruct((B, CONV_WIDTH, T), F32), jax.ShapeDtypeStruct((B, CONV_WIDTH, T), F32),
    ]
    out_specs = [tok(nq), tok(nq), chan(NA_HEADS * NA_VROWS), tok(nm), tok(nm),
                 chan(MLA_HEADS * MLA_VROWS), chan(CONV_WIDTH), chan(CONV_WIDTH)]
    return pl.pallas_call(
        _proj_kernel,
        grid=(B, T // tm),
        in_specs=in_specs,
        out_specs=out_specs,
        out_shape=out_shape,
        compiler_params=_cparams(("parallel", "parallel")),
        name="in_proj",
    )(x, mod, *consts, cos_t, sin_t)


def _na_kernel(q_ref, k_ref, v_ref, bias_ref, o_ref, s_ref, mx_ref, *, rows, sw):
    rb = pl.program_id(1)
    kstart = jnp.clip(rb * NA_Q_ROWS - (NA_K_ROWS - NA_Q_ROWS) // 2, 0, rows - NA_K_ROWS) * GRID_W
    kstart = pl.multiple_of(kstart, 256)
    nk = NA_K_ROWS * GRID_W
    kh = nk // 2
    nq = NA_Q_ROWS * GRID_W
    units = [(h, c) for h in range(NA_HEADS) for c in range(nq // sw)]
    lane = lax.broadcasted_iota(jnp.int32, (1, LANES), 1)
    nt = (((1,), (1,)), ((), ()))

    def score(i):
        h, c = units[i]
        slab = slice((h // 2) * LANES, (h // 2 + 1) * LANES)
        q = q_ref[0, c * sw:(c + 1) * sw, slab]
        q = jnp.where(lane >= HEAD_DIM if h % 2 else lane < HEAD_DIM, q, jnp.zeros_like(q))
        mx = None
        for r0 in range(0, nk, kh):
            k = k_ref[0, pl.ds(pl.multiple_of(kstart + r0, 256), kh), slab]
            s = (lax.dot_general(k, q, nt, preferred_element_type=F32)
                 + bias_ref[0, 0, h, r0:r0 + kh, c * sw:(c + 1) * sw])
            s_ref[i % 3, r0:r0 + kh, :] = s
            part = jnp.max(s, axis=0, keepdims=True)
            mx = part if mx is None else jnp.maximum(mx, part)
        mx_ref[i % 3] = mx

    score(0)
    score(1)
    for i, (h, c) in enumerate(units):
        p = jnp.exp2(s_ref[i % 3] - mx_ref[i % 3]).astype(BF16)
        v = v_ref[0, h * NA_VROWS:(h + 1) * NA_VROWS, pl.ds(kstart, nk)]
        acc = jnp.dot(v, p, preferred_element_type=F32)
        o_ref[0, h * HEAD_DIM:(h + 1) * HEAD_DIM, c * sw:(c + 1) * sw] = (
            acc[:HEAD_DIM] / acc[HEAD_DIM:HEAD_DIM + 1])
        if i + 2 < len(units):
            score(i + 2)


def _na_bias_kernel(e_ref, o_ref, *, rows):
    half = (NA_K_ROWS - NA_Q_ROWS) // 2
    lane = lax.broadcasted_iota(jnp.int32, (GRID_W, LANES), 1)
    masked = jnp.full((GRID_W, LANES), MASK_VALUE, F32)
    for v, (r0, ks) in enumerate(((0, 0), (NA_Q_ROWS, NA_Q_ROWS - half), (rows - NA_Q_ROWS, rows - NA_K_ROWS))):
        for kr in range(NA_K_ROWS):
            for pair in range(NA_Q_ROWS // 2):
                blocks = []
                for qr in (2 * pair, 2 * pair + 1):
                    r, k = r0 + qr, ks + kr
                    row_start = min(max(r - NA_KR // 2, 0), rows - NA_KR)
                    ok = row_start <= k < row_start + NA_KR
                    blocks.append(e_ref[0, 0, k - r + NA_KR - 1] if ok else masked)
                o_ref[0, v, 0, kr * GRID_W:(kr + 1) * GRID_W, pair * LANES:(pair + 1) * LANES] = (
                    jnp.where(lane < GRID_W, blocks[0], blocks[1]))


def _na_bias_tables(rpb, rows):
    L, H = rpb.shape[:2]
    cols = np.arange(GRID_W)
    col_start = np.clip(cols - NA_KC // 2, 0, GRID_W - NA_KC)
    col_ok = (cols[None, :] >= col_start[:, None]) & (cols[None, :] < col_start[:, None] + NA_KC)
    dc = np.clip(cols[None, :] - cols[:, None] + NA_KC - 1, 0, 2 * NA_KC - 2)
    sel = np.eye(2 * NA_KC - 1, dtype=np.float32)[dc.T]
    sel = np.concatenate([sel, sel], axis=1)
    ok = np.concatenate([col_ok.T, col_ok.T], axis=1)
    e = jnp.einsum("lhij,wqj->lhiwq", rpb.astype(F32) * float(np.log2(np.e)), jnp.asarray(sel),
                   precision=lax.Precision.HIGHEST)
    e = jnp.where(jnp.asarray(ok), e, MASK_VALUE)
    nk, nq = NA_K_ROWS * GRID_W, NA_Q_ROWS * GRID_W
    return pl.pallas_call(
        functools.partial(_na_bias_kernel, rows=rows),
        grid=(L, H),
        in_specs=[pl.BlockSpec((1, 1, 2 * NA_KR - 1, GRID_W, LANES), lambda l, h: (l, h, 0, 0, 0))],
        out_specs=pl.BlockSpec((1, 3, 1, nk, nq), lambda l, h: (l, 0, h, 0, 0)),
        out_shape=jax.ShapeDtypeStruct((L, 3, H, nk, nq), F32),
        compiler_params=_cparams(("parallel", "parallel")),
        name="na_bias",
    )(e)


def _na_call(qna, kna, vna_t, bias, layer, sw):
    B, T, _ = qna.shape
    rows = T // GRID_W
    nrb = rows // NA_Q_ROWS
    nq = NA_Q_ROWS * GRID_W
    nk = NA_K_ROWS * GRID_W

    def bias_map(b, rb):
        return (layer, jnp.where(rb == 0, 0, jnp.where(rb == nrb - 1, 2, 1)), 0, 0, 0)

    return pl.pallas_call(
        functools.partial(_na_kernel, rows=rows, sw=sw),
        grid=(B, nrb),
        in_specs=[
            pl.BlockSpec((1, nq, NA_HEADS * HEAD_DIM), lambda b, rb: (b, rb, 0)),
            pl.BlockSpec((1, T, NA_HEADS * HEAD_DIM), lambda b, rb: (b, 0, 0)),
            pl.BlockSpec((1, NA_HEADS * NA_VROWS, T), lambda b, rb: (b, 0, 0)),
            pl.BlockSpec((1, 1, NA_HEADS, nk, nq), bias_map),
        ],
        out_specs=pl.BlockSpec((1, NA_HEADS * HEAD_DIM, nq), lambda b, rb: (b, 0, rb)),
        out_shape=jax.ShapeDtypeStruct((B, NA_HEADS * HEAD_DIM, T), F32),
        scratch_shapes=[pltpu.VMEM((3, nk, sw), F32), pltpu.VMEM((3, 1, sw), F32)],
        compiler_params=_cparams(("parallel", "arbitrary")),
        name="na_attn",
    )(qna, kna, vna_t, bias)


def _mla_kernel(q_ref, k_ref, v_ref, o_ref, s_ref, mx_ref, *, tq, tk, sw, unroll):
    T = k_ref.shape[1]
    nq = T // tq
    nkv = T // tk
    strips = [slice(c * sw, (c + 1) * sw) for c in range(tq // sw)]
    ns = len(strips)
    nt = (((1,), (1,)), ((), ()))

    def score(slot, qi, j, c):
        k = k_ref[0, pl.ds(pl.multiple_of(j * tk, tk), tk), :]
        q = q_ref[0, pl.ds(pl.multiple_of(qi * tq + c * sw, sw), sw), :]
        s = lax.dot_general(k, q, nt, preferred_element_type=F32)
        s_ref[slot, :, strips[c]] = s
        mx_ref[slot, :, strips[c]] = jnp.max(s, axis=0, keepdims=True)

    def step(slot, j, nxt_qi, nxt_j, m, acc):
        v = v_ref[0, :, pl.ds(pl.multiple_of(j * tk, tk), tk)]
        ms, accs = [], []
        for c in range(min(2, ns)):
            score(1 - slot, nxt_qi, nxt_j, c)
        for c, sl in enumerate(strips):
            m_old = m[:, sl]
            m_new = jnp.maximum(m_old, mx_ref[slot, :, sl])
            p = jnp.exp2(s_ref[slot, :, sl] - m_new).astype(BF16)
            pv = jnp.dot(v, p, preferred_element_type=F32)
            accs.append(jnp.exp2(m_old - m_new) * acc[:, sl] + pv)
            ms.append(m_new)
            if c + 2 < ns:
                score(1 - slot, nxt_qi, nxt_j, c + 2)
        return jnp.concatenate(ms, axis=1), jnp.concatenate(accs, axis=1)

    def body(bi, carry):
        m, acc = carry
        t0 = bi * unroll
        qi = t0 // nkv
        j0 = t0 % nkv
        fresh = j0 == 0
        m = jnp.where(fresh, -jnp.inf, m)
        acc = jnp.where(fresh, 0.0, acc)
        for u in range(unroll):
            if u + 1 < unroll:
                nxt_qi, nxt_j = qi, j0 + u + 1
            else:
                wrap = j0 + unroll == nkv
                nxt_qi = jnp.minimum(qi + wrap.astype(jnp.int32), nq - 1)
                nxt_j = jnp.where(wrap, 0, j0 + unroll)
            m, acc = step(u % 2, j0 + u, nxt_qi, nxt_j, m, acc)

        @pl.when(j0 + unroll == nkv)
        def _():
            o_ref[0, :, pl.ds(pl.multiple_of(qi * tq, tq), tq)] = acc[:MLA_V] / acc[MLA_V:MLA_V + 1]

        return m, acc

    for c in range(ns):
        score(0, 0, 0, c)
    init = (jnp.full((1, tq), -jnp.inf, F32), jnp.zeros((MLA_VROWS, tq), F32))
    lax.fori_loop(0, nq * nkv // unroll, body, init)


def _mla_call(qm, km, vm_t, tq, tk, sw, unroll):
    B, T, _ = qm.shape
    assert unroll % 2 == 0 and (T // tk) % unroll == 0 and T % tq == 0 and tq % sw == 0
    return pl.pallas_call(
        functools.partial(_mla_kernel, tq=tq, tk=tk, sw=sw, unroll=unroll),
        grid=(B, MLA_HEADS),
        in_specs=[
            pl.BlockSpec((1, T, LANES), lambda b, h: (b, 0, h)),
            pl.BlockSpec((1, T, LANES), lambda b, h: (b, 0, h)),
            pl.BlockSpec((1, MLA_VROWS, T), lambda b, h: (b, h, 0)),
        ],
        out_specs=pl.BlockSpec((1, MLA_V, T), lambda b, h: (b, h, 0)),
        out_shape=jax.ShapeDtypeStruct((B, MLA_HEADS * MLA_V, T), F32),
        scratch_shapes=[pltpu.VMEM((2, tk, tq), F32), pltpu.VMEM((2, 1, tq), F32)],
        compiler_params=_cparams(("parallel", "parallel")),
        name="mla_attn",
    )(qm, km, vm_t)


def _group_rms_rows(x):
    c, tm = x.shape
    xg = x.reshape(c // HEAD_DIM, HEAD_DIM, tm)
    ms = jnp.mean(xg * xg, axis=1, keepdims=True)
    return (xg * lax.rsqrt(ms + EPS)).reshape(c, tm)


def _mix_ffn_kernel(x_ref, mod_ref, yna_ref, ym_ref, u_ref, up_ref, un_ref, gb_ref, cw_ref, cb_ref,
                    og_ref, wout_ref, g2_ref, wg_ref, wu_ref, wd_ref, o_ref, act_ref, *, chunk):
    i = pl.program_id(1)
    last = pl.num_programs(1) - 1
    u = u_ref[0]
    tm = u.shape[1]
    prev = jnp.where(i > 0, up_ref[0], 0.0)
    nxt = jnp.where(i < last, un_ref[0], 0.0)
    ext = jnp.concatenate([prev, u, nxt], axis=1)
    w = ext.shape[1]
    u_m1 = pltpu.roll(ext, 1, 1)[:, LANES:LANES + tm]
    u_p1 = pltpu.roll(ext, w - 1, 1)[:, LANES:LANES + tm]
    y = cw_ref[0] * u_m1 + cw_ref[1] * u + cw_ref[2] * u_p1 + cb_ref[...]
    yc = gb_ref[0] * y
    mixed = jnp.concatenate([_group_rms_rows(yna_ref[0]), _group_rms_rows(ym_ref[0]),
                             _group_rms_rows(yc)], axis=0)
    mixed = (mixed.T * og_ref[...]).astype(BF16)
    x = x_ref[0] + mod_ref[0, 2:3, :] * jnp.dot(mixed, wout_ref[...], preferred_element_type=F32)
    sh = mod_ref[0, 3:4, :]
    sc = mod_ref[0, 4:5, :]
    hb = (_row_rms(x, g2_ref[...]) * (1.0 + sc) + sh).astype(BF16)
    dff = wg_ref.shape[1]
    for c in range(dff // chunk):
        sl = slice(c * chunk, (c + 1) * chunk)
        g = jnp.dot(hb, wg_ref[:, sl], preferred_element_type=F32)
        up = jnp.dot(hb, wu_ref[:, sl], preferred_element_type=F32)
        act_ref[:, sl] = (g * jax.nn.sigmoid(g) * up).astype(BF16)
    out = jnp.dot(act_ref[...], wd_ref[...], preferred_element_type=F32)
    o_ref[0] = x + mod_ref[0, 5:6, :] * out


def _mix_ffn_call(x, mod, yna_t, ym_t, u_t, gb_t, p, tm):
    B, T, D = x.shape
    nb = tm // LANES
    nlb = T // LANES
    dff = p["w_g"].shape[1]
    chan = lambda c: pl.BlockSpec((1, c, tm), lambda b, i: (b, 0, i))
    consts = [p["conv_w"], p["conv_b"], p["out_g"], p["w_out"], p["g2"], p["w_g"], p["w_u"], p["w_d"]]
    return pl.pallas_call(
        functools.partial(_mix_ffn_kernel, chunk=256),
        grid=(B, T // tm),
        in_specs=[
            pl.BlockSpec((1, tm, D), lambda b, i: (b, i, 0)),
            pl.BlockSpec((1, 6, D), lambda b, i: (b, 0, 0)),
            chan(NA_HEADS * HEAD_DIM), chan(MLA_HEADS * MLA_V), chan(CONV_WIDTH),
            pl.BlockSpec((1, CONV_WIDTH, LANES), lambda b, i: (b, 0, jnp.maximum(i * nb - 1, 0))),
            pl.BlockSpec((1, CONV_WIDTH, LANES), lambda b, i: (b, 0, jnp.minimum((i + 1) * nb, nlb - 1))),
            chan(CONV_WIDTH),
        ] + [_const_spec(a.shape) for a in consts],
        out_specs=pl.BlockSpec((1, tm, D), lambda b, i: (b, i, 0)),
        out_shape=jax.ShapeDtypeStruct((B, T, D), F32),
        scratch_shapes=[pltpu.VMEM((tm, dff), BF16)],
        compiler_params=_cparams(("parallel", "parallel")),
        name="mix_ffn",
    )(x, mod, yna_t, ym_t, u_t, u_t, u_t, gb_t, *consts)


def _pad_heads(w, heads, width):
    k = w.shape[0]
    w = w.reshape(k, heads, width)
    return jnp.pad(w, ((0, 0), (0, 0), (0, LANES - width))).reshape(k, heads * LANES)


def _lane_row(parts, repeat=1):
    row = jnp.zeros((1, LANES), F32)
    for off, v in parts:
        row = row.at[0, off:off + v.shape[0]].set(v.astype(F32))
    return jnp.tile(row, (1, repeat))


def _norm_matrices():
    mla = np.zeros((LANES, LANES), np.float32)
    mla[:MLA_NOPE, :MLA_NOPE] = 1.0 / MLA_NOPE
    mla[MLA_NOPE:MLA_NOPE + MLA_ROPE, MLA_NOPE:MLA_NOPE + MLA_ROPE] = 1.0 / MLA_ROPE
    na = np.zeros((LANES, LANES), np.float32)
    na[:HEAD_DIM, :HEAD_DIM] = 1.0 / HEAD_DIM
    na[HEAD_DIM:, HEAD_DIM:] = 1.0 / HEAD_DIM
    z = np.zeros((LANES, LANES), np.float32)
    pair = lambda m: jnp.asarray(np.block([[m, z], [z, m]]), BF16)
    return pair(mla), pair(na)


def _layer_params(l, tm, norm1_g, norm2_g, w_in, na_q_g, na_k_g, mla_q_a_g, mla_kv_a_g, mla_w_uq,
                  mla_w_ukv, mla_qn_g, mla_kn_g, mla_qr_g, mla_kr_g, conv_w, conv_b, out_norm_g,
                  w_out, w_gu, w_down):
    naw = NA_HEADS * HEAD_DIM
    i0 = 3 * naw
    i1 = i0 + MLA_Q_RANK
    i2 = i1 + MLA_KV_RANK
    i3 = i2 + MLA_ROPE
    w = w_in[l]
    d = w.shape[0]
    kr_slab = jnp.pad(w[:, i2:i3], ((0, 0), (MLA_NOPE, LANES - MLA_NOPE - MLA_ROPE)))
    w_in_r = jnp.concatenate([w[:, :i2], kr_slab, w[:, i3:]], axis=1).astype(BF16)
    ukv = mla_w_ukv[l].reshape(MLA_KV_RANK, MLA_HEADS, MLA_NOPE + MLA_V)
    w_uk = _pad_heads(ukv[:, :, :MLA_NOPE].reshape(MLA_KV_RANK, -1), MLA_HEADS, MLA_NOPE).astype(BF16)
    w_uv = ukv[:, :, MLA_NOPE:].reshape(MLA_KV_RANK, -1).astype(BF16)
    w_uq = _pad_heads(mla_w_uq[l], MLA_HEADS, MLA_NOPE + MLA_ROPE).astype(BF16)
    na_scale = HEAD_DIM ** -0.5 * float(np.log2(np.e))
    mla_scale = (MLA_NOPE + MLA_ROPE) ** -0.5 * float(np.log2(np.e))
    dff = w_down.shape[1]
    m_mla, m_na = _norm_matrices()
    return {
        "g1": norm1_g[l].reshape(1, d), "g2": norm2_g[l].reshape(1, d),
        "w_in": w_in_r, "w_uq": w_uq, "w_uk": w_uk, "w_uv": w_uv, "m2": m_mla, "m_na": m_na,
        "g_qna": _lane_row([(0, na_q_g[l] * na_scale), (HEAD_DIM, na_q_g[l] * na_scale)], 2),
        "g_kna": _lane_row([(0, na_k_g[l]), (HEAD_DIM, na_k_g[l])], 2),
        "g_qa": mla_q_a_g[l].reshape(1, -1), "g_kva": mla_kv_a_g[l].reshape(1, -1),
        "g_q": _lane_row([(0, mla_qn_g[l] * mla_scale), (MLA_NOPE, mla_qr_g[l] * mla_scale)], 2),
        "g_k": _lane_row([(0, mla_kn_g[l])], 2),
        "g_kr": _lane_row([(MLA_NOPE, mla_kr_g[l])]),
        "conv_w": jnp.broadcast_to(conv_w[l][:, :, None], (3, CONV_WIDTH, tm)),
        "conv_b": jnp.broadcast_to(conv_b[l][:, None], (CONV_WIDTH, tm)),
        "out_g": out_norm_g[l].reshape(1, -1),
        "w_out": w_out[l].astype(BF16),
        "w_g": w_gu[l][:, :dff].astype(BF16),
        "w_u": w_gu[l][:, dff:].astype(BF16),
        "w_d": w_down[l].astype(BF16),
    }


def kernel(x, c, positions, norm1_g, norm2_g, w_ada, b_ada, w_in, na_q_g, na_k_g, na_rpb, mla_q_a_g,
           mla_kv_a_g, mla_w_uq, mla_w_ukv, mla_qn_g, mla_kn_g, mla_qr_g, mla_kr_g, conv_w, conv_b,
           out_norm_g, w_out, w_gu, w_down):
    B, T, D = x.shape
    depth = w_in.shape[0]
    rows = T // GRID_W
    tm = 512
    mod = _ada_modulation(c, w_ada, b_ada)
    cos_t, sin_t = _rope_tables(positions)
    na_bias = _na_bias_tables(na_rpb, rows)
    for l in range(depth):
        p = _layer_params(l, tm, norm1_g, norm2_g, w_in, na_q_g, na_k_g, mla_q_a_g, mla_kv_a_g,
                          mla_w_uq, mla_w_ukv, mla_qn_g, mla_kn_g, mla_qr_g, mla_kr_g, conv_w,
                          conv_b, out_norm_g, w_out, w_gu, w_down)
        qna, kna, vna_t, qm, km, vm_t, u_t, gb_t = _proj_call(x, mod[l], cos_t, sin_t, p, tm)
        yna_t = _na_call(qna, kna, vna_t, na_bias, l, sw=256)
        ym_t = _mla_call(qm, km, vm_t, tq=1024, tk=MLA_TK, sw=256, unroll=min(8, T // MLA_TK))
        x = _mix_ffn_call(x, mod[l], yna_t, ym_t, u_t, gb_t, p, tm)
    return x
```

```python
import functools

import jax
import jax.numpy as jnp
import numpy as np
from jax import lax
from jax.experimental import pallas as pl
from jax.experimental.pallas import tpu as pltpu

F32 = jnp.float32
BF16 = jnp.bfloat16

GRID_W = 64
HEAD_DIM = 64
NA_HEADS = 4
NA_KR = 8
NA_KC = 16
MLA_HEADS = 8
MLA_NOPE = 64
MLA_ROPE = 32
MLA_V = 64
MLA_Q_RANK = 384
MLA_KV_RANK = 256
CONV_WIDTH = 256
ROPE_THETA = 10000.0
EPS = 1e-6

LANES = 128
NA_Q_ROWS = 8
ONES_ROWS = 16
MLA_VROWS = MLA_V + ONES_ROWS
NA_VROWS = HEAD_DIM + ONES_ROWS
NA_K_ROWS = 16
MLA_TK = 512
MASK_VALUE = -1e30
VMEM_LIMIT = 56 * 1024 * 1024


def _cparams(sem):
    return pltpu.CompilerParams(dimension_semantics=sem, vmem_limit_bytes=VMEM_LIMIT)


def _const_spec(shape):
    nd = len(shape)
    return pl.BlockSpec(shape, lambda *_: (0,) * nd, pipeline_mode=pl.Buffered(1))


def _split_bf16(x):
    hi = x.astype(BF16)
    lo = (x - hi.astype(F32)).astype(BF16)
    return hi, lo


def _ada_kernel(c_ref, w_ref, b_ref, o_ref):
    c = c_ref[...]
    a = c * jax.nn.sigmoid(c)
    a_hi, a_lo = _split_bf16(a)
    w_hi, w_lo = _split_bf16(w_ref[0])
    acc = jnp.dot(a_hi, w_hi, preferred_element_type=F32)
    acc += jnp.dot(a_lo, w_hi, preferred_element_type=F32)
    acc += jnp.dot(a_hi, w_lo, preferred_element_type=F32)
    o_ref[0] = acc + b_ref[0]


def _ada_modulation(c, w_ada, b_ada):
    L, D, N = w_ada.shape
    B = c.shape[0]
    rows = 8
    c_pad = jnp.zeros((rows, D), F32).at[:B].set(c)
    tn = 1536
    out = pl.pallas_call(
        _ada_kernel,
        grid=(L, N // tn),
        in_specs=[
            pl.BlockSpec((rows, D), lambda l, j: (0, 0)),
            pl.BlockSpec((1, D, tn), lambda l, j: (l, 0, j)),
            pl.BlockSpec((1, 1, tn), lambda l, j: (l, 0, j)),
        ],
        out_specs=pl.BlockSpec((1, rows, tn), lambda l, j: (l, 0, j)),
        out_shape=jax.ShapeDtypeStruct((L, rows, N), F32),
        compiler_params=_cparams(("parallel", "parallel")),
        name="ada_mod",
    )(c_pad, w_ada, b_ada.reshape(L, 1, N))
    return out[:, :B].reshape(L, B, 6, D)


def _rope_kernel(pos_ref, inv_ref, sgn_ref, cos_ref, sin_ref):
    ang = pos_ref[0].astype(F32) * inv_ref[...]
    cos_ref[0] = jnp.cos(ang)
    sin_ref[0] = jnp.sin(ang) * sgn_ref[...]


def _rope_tables(positions):
    B, T = positions.shape
    half = MLA_ROPE // 2
    inv = ROPE_THETA ** (-jnp.arange(0, MLA_ROPE, 2, dtype=F32) / MLA_ROPE)
    inv_lane = jnp.zeros((1, LANES), F32)
    inv_lane = inv_lane.at[0, MLA_NOPE:MLA_NOPE + half].set(inv)
    inv_lane = inv_lane.at[0, MLA_NOPE + half:MLA_NOPE + 2 * half].set(inv)
    sgn = np.zeros((1, LANES), np.float32)
    sgn[0, MLA_NOPE:MLA_NOPE + half] = -1.0
    sgn[0, MLA_NOPE + half:MLA_NOPE + 2 * half] = 1.0
    tm = min(T, 2048)
    spec = pl.BlockSpec((1, tm, LANES), lambda b, i: (b, i, 0))
    return pl.pallas_call(
        _rope_kernel,
        grid=(B, T // tm),
        in_specs=[
            pl.BlockSpec((1, tm, 1), lambda b, i: (b, i, 0)),
            _const_spec((1, LANES)),
            _const_spec((1, LANES)),
        ],
        out_specs=[spec, spec],
        out_shape=[jax.ShapeDtypeStruct((B, T, LANES), F32)] * 2,
        compiler_params=_cparams(("parallel", "parallel")),
        name="rope_tables",
    )(positions.reshape(B, T, 1), inv_lane, jnp.asarray(sgn))


def _pair_rms(x, m_ref, gain):
    ms = jnp.dot((x * x).astype(BF16), m_ref[...], preferred_element_type=F32)
    return x * lax.rsqrt(ms + EPS) * gain


def _slab_rope(xs, cos, sin, first_half):
    half = MLA_ROPE // 2
    w = xs.shape[1]
    partner = jnp.where(first_half,
                        pltpu.roll(xs, w - half, 1),
                        pltpu.roll(xs, half, 1))
    return xs * cos + partner * sin


def _store_values(v_ref, vt, heads, width):
    ones = jnp.ones((ONES_ROWS, vt.shape[1]), BF16)
    rows = width + ONES_ROWS
    for hd in range(heads):
        v_ref[0, hd * rows:hd * rows + width, :] = vt[hd * width:(hd + 1) * width]
        v_ref[0, hd * rows + width:(hd + 1) * rows, :] = ones


def _row_rms(x, gain):
    ms = jnp.mean(x * x, axis=-1, keepdims=True)
    return x * lax.rsqrt(ms + EPS) * gain


def _proj_kernel(x_ref, mod_ref, g1_ref, win_ref, wuq_ref, wuk_ref, wuv_ref, m2_ref, mna_ref,
                 gqna_ref, gkna_ref, gqa_ref, gkva_ref, gq_ref, gk_ref, gkr_ref, cos_ref, sin_ref,
                 qna_ref, kna_ref, vna_ref, qm_ref, km_ref, vm_ref, u_ref, gb_ref):
    x = x_ref[0]
    sh = mod_ref[0, 0:1, :]
    sc = mod_ref[0, 1:2, :]
    h = _row_rms(x, g1_ref[...]) * (1.0 + sc) + sh
    hb = h.astype(BF16)

    def proj(a, b):
        return jnp.dot(hb, win_ref[:, a:b], preferred_element_type=F32)

    nq = NA_HEADS * HEAD_DIM
    pair = 2 * LANES
    qna_ref[0] = _pair_rms(proj(0, nq), mna_ref, gqna_ref[...]).T.astype(BF16)
    kna_ref[0] = _pair_rms(proj(nq, 2 * nq), mna_ref, gkna_ref[...]).astype(BF16)
    o = 2 * nq
    _store_values(vna_ref, proj(o, o + nq).T.astype(BF16), NA_HEADS, HEAD_DIM)
    o += nq

    cos = cos_ref[0]
    sin = sin_ref[0]
    cos2 = jnp.concatenate([cos, cos], axis=1)
    sin2 = jnp.concatenate([sin, sin], axis=1)
    lane = lax.broadcasted_iota(jnp.int32, (1, pair), 1) % LANES
    first_half = lane < MLA_NOPE + MLA_ROPE // 2

    cq = _row_rms(proj(o, o + MLA_Q_RANK), gqa_ref[...]).astype(BF16)
    o += MLA_Q_RANK
    q = jnp.dot(cq, wuq_ref[...], preferred_element_type=F32)
    for s0 in range(0, MLA_HEADS * LANES, pair):
        qs = _pair_rms(q[:, s0:s0 + pair], m2_ref, gq_ref[...])
        qm_ref[0, s0:s0 + pair, :] = _slab_rope(qs, cos2, sin2, first_half).T.astype(BF16)

    ckv = _row_rms(proj(o, o + MLA_KV_RANK), gkva_ref[...]).astype(BF16)
    o += MLA_KV_RANK
    kr = proj(o, o + LANES)
    kr = kr * lax.rsqrt(jnp.sum(kr * kr, axis=-1, keepdims=True) * (1.0 / MLA_ROPE) + EPS) * gkr_ref[...]
    kr = _slab_rope(kr, cos, sin, first_half[:, :LANES])
    kr2 = jnp.concatenate([kr, kr], axis=1)
    o += LANES
    kn = jnp.dot(ckv, wuk_ref[...], preferred_element_type=F32)
    for s0 in range(0, MLA_HEADS * LANES, pair):
        km_ref[0, :, s0:s0 + pair] = (_pair_rms(kn[:, s0:s0 + pair], m2_ref, gk_ref[...]) + kr2).astype(BF16)
    vt = jnp.dot(ckv, wuv_ref[...], preferred_element_type=F32).T.astype(BF16)
    _store_values(vm_ref, vt, MLA_HEADS, MLA_V)

    cw = CONV_WIDTH
    pc = proj(o, o + 3 * cw)
    u_ref[0] = (pc[:, 2 * cw:3 * cw] * pc[:, 0:cw]).T
    gb_ref[0] = pc[:, cw:2 * cw].T


def _proj_call(x, mod, cos_t, sin_t, p, tm):
    B, T, D = x.shape
    nq = NA_HEADS * HEAD_DIM
    nm = MLA_HEADS * LANES
    tok = lambda w: pl.BlockSpec((1, tm, w), lambda b, i: (b, i, 0))
    chan = lambda c: pl.BlockSpec((1, c, tm), lambda b, i: (b, 0, i))
    consts = [p["g1"], p["w_in"], p["w_uq"], p["w_uk"], p["w_uv"], p["m2"], p["m_na"], p["g_qna"],
              p["g_kna"], p["g_qa"], p["g_kva"], p["g_q"], p["g_k"], p["g_kr"]]
    in_specs = ([tok(D), pl.BlockSpec((1, 6, D), lambda b, i: (b, 0, 0))]
                + [_const_spec(a.shape) for a in consts] + [tok(LANES), tok(LANES)])
    out_shape = [
        jax.ShapeDtypeStruct((B, nq, T), BF16), jax.ShapeDtypeStruct((B, T, nq), BF16),
        jax.ShapeDtypeStruct((B, NA_HEADS * NA_VROWS, T), BF16),
        jax.ShapeDtypeStruct((B, nm, T), BF16), jax.ShapeDtypeStruct((B, T, nm), BF16),
        jax.ShapeDtypeStruct((B, MLA_HEADS * MLA_VROWS, T), BF16),
        jax.ShapeDtypeStruct((B, CONV_WIDTH, T), F32), jax.ShapeDtypeStruct((B, CONV_WIDTH, T), F32),
    ]
    out_specs = [chan(nq), tok(nq), chan(NA_HEADS * NA_VROWS), chan(nm), tok(nm),
                 chan(MLA_HEADS * MLA_VROWS), chan(CONV_WIDTH), chan(CONV_WIDTH)]
    return pl.pallas_call(
        _proj_kernel,
        grid=(B, T // tm),
        in_specs=in_specs,
        out_specs=out_specs,
        out_shape=out_shape,
        compiler_params=_cparams(("parallel", "parallel")),
        name="in_proj",
    )(x, mod, *consts, cos_t, sin_t)


def _na_kernel(q_ref, k_ref, v_ref, bias_ref, o_ref, s_ref, mx_ref, *, rows, sw):
    rb = pl.program_id(1)
    kstart = jnp.clip(rb * NA_Q_ROWS - (NA_K_ROWS - NA_Q_ROWS) // 2, 0, rows - NA_K_ROWS) * GRID_W
    kstart = pl.multiple_of(kstart, 256)
    nk = NA_K_ROWS * GRID_W
    kh = nk // 2
    nq = NA_Q_ROWS * GRID_W
    units = [(h, c) for h in range(NA_HEADS) for c in range(nq // sw)]
    chan = lax.broadcasted_iota(jnp.int32, (LANES, 1), 0)

    def score(i):
        h, c = units[i]
        slab = slice((h // 2) * LANES, (h // 2 + 1) * LANES)
        q = q_ref[0, slab, c * sw:(c + 1) * sw]
        q = jnp.where(chan >= HEAD_DIM if h % 2 else chan < HEAD_DIM, q, jnp.zeros_like(q))
        mx = None
        for r0 in range(0, nk, kh):
            k = k_ref[0, pl.ds(pl.multiple_of(kstart + r0, 256), kh), slab]
            s = (jnp.dot(k, q, preferred_element_type=F32)
                 + bias_ref[0, 0, h, r0:r0 + kh, c * sw:(c + 1) * sw])
            s_ref[i % 3, r0:r0 + kh, :] = s
            part = jnp.max(s, axis=0, keepdims=True)
            mx = part if mx is None else jnp.maximum(mx, part)
        mx_ref[i % 3] = mx

    score(0)
    score(1)
    for i, (h, c) in enumerate(units):
        p = jnp.exp2(s_ref[i % 3] - mx_ref[i % 3]).astype(BF16)
        v = v_ref[0, h * NA_VROWS:(h + 1) * NA_VROWS, pl.ds(kstart, nk)]
        acc = jnp.dot(v, p, preferred_element_type=F32)
        o_ref[0, h * HEAD_DIM:(h + 1) * HEAD_DIM, c * sw:(c + 1) * sw] = (
            acc[:HEAD_DIM] / acc[HEAD_DIM:HEAD_DIM + 1])
        if i + 2 < len(units):
            score(i + 2)


def _na_bias_kernel(e_ref, o_ref, *, rows):
    half = (NA_K_ROWS - NA_Q_ROWS) // 2
    lane = lax.broadcasted_iota(jnp.int32, (GRID_W, LANES), 1)
    masked = jnp.full((GRID_W, LANES), MASK_VALUE, F32)
    for v, (r0, ks) in enumerate(((0, 0), (NA_Q_ROWS, NA_Q_ROWS - half), (rows - NA_Q_ROWS, rows - NA_K_ROWS))):
        for kr in range(NA_K_ROWS):
            for pair in range(NA_Q_ROWS // 2):
                blocks = []
                for qr in (2 * pair, 2 * pair + 1):
                    r, k = r0 + qr, ks + kr
                    row_start = min(max(r - NA_KR // 2, 0), rows - NA_KR)
                    ok = row_start <= k < row_start + NA_KR
                    blocks.append(e_ref[0, 0, k - r + NA_KR - 1] if ok else masked)
                o_ref[0, v, 0, kr * GRID_W:(kr + 1) * GRID_W, pair * LANES:(pair + 1) * LANES] = (
                    jnp.where(lane < GRID_W, blocks[0], blocks[1]))


def _na_bias_tables(rpb, rows):
    L, H = rpb.shape[:2]
    cols = np.arange(GRID_W)
    col_start = np.clip(cols - NA_KC // 2, 0, GRID_W - NA_KC)
    col_ok = (cols[None, :] >= col_start[:, None]) & (cols[None, :] < col_start[:, None] + NA_KC)
    dc = np.clip(cols[None, :] - cols[:, None] + NA_KC - 1, 0, 2 * NA_KC - 2)
    sel = np.eye(2 * NA_KC - 1, dtype=np.float32)[dc.T]
    sel = np.concatenate([sel, sel], axis=1)
    ok = np.concatenate([col_ok.T, col_ok.T], axis=1)
    e = jnp.einsum("lhij,wqj->lhiwq", rpb.astype(F32) * float(np.log2(np.e)), jnp.asarray(sel),
                   precision=lax.Precision.HIGHEST)
    e = jnp.where(jnp.asarray(ok), e, MASK_VALUE)
    nk, nq = NA_K_ROWS * GRID_W, NA_Q_ROWS * GRID_W
    return pl.pallas_call(
        functools.partial(_na_bias_kernel, rows=rows),
        grid=(L, H),
        in_specs=[pl.BlockSpec((1, 1, 2 * NA_KR - 1, GRID_W, LANES), lambda l, h: (l, h, 0, 0, 0))],
        out_specs=pl.BlockSpec((1, 3, 1, nk, nq), lambda l, h: (l, 0, h, 0, 0)),
        out_shape=jax.ShapeDtypeStruct((L, 3, H, nk, nq), F32),
        compiler_params=_cparams(("parallel", "parallel")),
        name="na_bias",
    )(e)


def _na_call(qna, kna, vna_t, bias, layer, sw):
    B, T, _ = kna.shape
    rows = T // GRID_W
    nrb = rows // NA_Q_ROWS
    nq = NA_Q_ROWS * GRID_W
    nk = NA_K_ROWS * GRID_W

    def bias_map(b, rb):
        return (layer, jnp.where(rb == 0, 0, jnp.where(rb == nrb - 1, 2, 1)), 0, 0, 0)

    return pl.pallas_call(
        functools.partial(_na_kernel, rows=rows, sw=sw),
        grid=(B, nrb),
        in_specs=[
            pl.BlockSpec((1, NA_HEADS * HEAD_DIM, nq), lambda b, rb: (b, 0, rb)),
            pl.BlockSpec((1, T, NA_HEADS * HEAD_DIM), lambda b, rb: (b, 0, 0)),
            pl.BlockSpec((1, NA_HEADS * NA_VROWS, T), lambda b, rb: (b, 0, 0)),
            pl.BlockSpec((1, 1, NA_HEADS, nk, nq), bias_map),
        ],
        out_specs=pl.BlockSpec((1, NA_HEADS * HEAD_DIM, nq), lambda b, rb: (b, 0, rb)),
        out_shape=jax.ShapeDtypeStruct((B, NA_HEADS * HEAD_DIM, T), F32),
        scratch_shapes=[pltpu.VMEM((3, nk, sw), F32), pltpu.VMEM((3, 1, sw), F32)],
        compiler_params=_cparams(("parallel", "arbitrary")),
        name="na_attn",
    )(qna, kna, vna_t, bias)


def _mla_kernel(q_ref, k_ref, v_ref, o_ref, s_ref, mx_ref, *, tq, tk, sw, unroll):
    T = k_ref.shape[1]
    nq = T // tq
    nkv = T // tk
    strips = [slice(c * sw, (c + 1) * sw) for c in range(tq // sw)]
    ns = len(strips)
    hk = tk // 2

    def score(slot, qi, j, c):
        k = k_ref[0, pl.ds(pl.multiple_of(j * tk, tk), tk), :]
        q = q_ref[0, :, pl.ds(pl.multiple_of(qi * tq + c * sw, sw), sw)]
        s = jnp.dot(k, q, preferred_element_type=F32)
        s_ref[slot, :, strips[c]] = s
        mx_ref[slot, :, strips[c]] = jnp.max(s, axis=0, keepdims=True)

    def step(slot, j, nxt_qi, nxt_j, m, acc):
        v = v_ref[0, :, pl.ds(pl.multiple_of(j * tk, tk), tk)]
        ms, accs = [], []
        for c in range(min(2, ns)):
            score(1 - slot, nxt_qi, nxt_j, c)
        for c, sl in enumerate(strips):
            m_old = m[:, sl]
            m_new = jnp.maximum(m_old, mx_ref[slot, :, sl])
            p = jnp.exp2(s_ref[slot, :, sl] - m_new).astype(BF16)
            pv = jnp.dot(v[:, :hk], p[:hk], preferred_element_type=F32)
            if c + 2 < ns:
                score(1 - slot, nxt_qi, nxt_j, c + 2)
            pv = pv + jnp.dot(v[:, hk:], p[hk:], preferred_element_type=F32)
            accs.append(jnp.exp2(m_old - m_new) * acc[:, sl] + pv)
            ms.append(m_new)
        return jnp.concatenate(ms, axis=1), jnp.concatenate(accs, axis=1)

    def body(bi, carry):
        m, acc = carry
        t0 = bi * unroll
        qi = t0 // nkv
        j0 = t0 % nkv
        fresh = j0 == 0
        m = jnp.where(fresh, -jnp.inf, m)
        acc = jnp.where(fresh, 0.0, acc)
        for u in range(unroll):
            if u + 1 < unroll:
                nxt_qi, nxt_j = qi, j0 + u + 1
            else:
                nxt_qi = jnp.minimum(qi + (j0 + unroll) // nkv, nq - 1)
                nxt_j = (j0 + unroll) % nkv
            m, acc = step(u % 2, j0 + u, nxt_qi, nxt_j, m, acc)

        @pl.when(j0 + unroll == nkv)
        def _():
            o_ref[0, :, pl.ds(pl.multiple_of(qi * tq, tq), tq)] = acc[:MLA_V] / acc[MLA_V:MLA_V + 1]

        return m, acc

    for c in range(ns):
        score(0, 0, 0, c)
    init = (jnp.full((1, tq), -jnp.inf, F32), jnp.zeros((MLA_VROWS, tq), F32))
    lax.fori_loop(0, nq * nkv // unroll, body, init)


def _mla_call(qm, km, vm_t, tq, tk, sw, unroll):
    B, T, _ = km.shape
    assert unroll % 2 == 0 and (T // tk) % unroll == 0 and T % tq == 0 and tq % sw == 0
    return pl.pallas_call(
        functools.partial(_mla_kernel, tq=tq, tk=tk, sw=sw, unroll=unroll),
        grid=(B, MLA_HEADS),
        in_specs=[
            pl.BlockSpec((1, LANES, T), lambda b, h: (b, h, 0)),
            pl.BlockSpec((1, T, LANES), lambda b, h: (b, 0, h)),
            pl.BlockSpec((1, MLA_VROWS, T), lambda b, h: (b, h, 0)),
        ],
        out_specs=pl.BlockSpec((1, MLA_V, T), lambda b, h: (b, h, 0)),
        out_shape=jax.ShapeDtypeStruct((B, MLA_HEADS * MLA_V, T), F32),
        scratch_shapes=[pltpu.VMEM((2, tk, tq), F32), pltpu.VMEM((2, 1, tq), F32)],
        compiler_params=_cparams(("parallel", "parallel")),
        name="mla_attn",
    )(qm, km, vm_t)


def _group_rms_rows(x):
    c, tm = x.shape
    xg = x.reshape(c // HEAD_DIM, HEAD_DIM, tm)
    ms = jnp.mean(xg * xg, axis=1, keepdims=True)
    return (xg * lax.rsqrt(ms + EPS)).reshape(c, tm)


def _mix_ffn_kernel(x_ref, mod_ref, yna_ref, ym_ref, u_ref, up_ref, un_ref, gb_ref, cw_ref, cb_ref,
                    og_ref, wout_ref, g2_ref, wg_ref, wu_ref, wd_ref, o_ref, act_ref, *, chunk):
    i = pl.program_id(1)
    last = pl.num_programs(1) - 1
    u = u_ref[0]
    tm = u.shape[1]
    prev = jnp.where(i > 0, up_ref[0], 0.0)
    nxt = jnp.where(i < last, un_ref[0], 0.0)
    ext = jnp.concatenate([prev, u, nxt], axis=1)
    w = ext.shape[1]
    u_m1 = pltpu.roll(ext, 1, 1)[:, LANES:LANES + tm]
    u_p1 = pltpu.roll(ext, w - 1, 1)[:, LANES:LANES + tm]
    y = cw_ref[0] * u_m1 + cw_ref[1] * u + cw_ref[2] * u_p1 + cb_ref[...]
    yc = gb_ref[0] * y
    mixed = jnp.concatenate([_group_rms_rows(yna_ref[0]), _group_rms_rows(ym_ref[0]),
                             _group_rms_rows(yc)], axis=0)
    mixed = (mixed.T * og_ref[...]).astype(BF16)
    x = x_ref[0] + mod_ref[0, 2:3, :] * jnp.dot(mixed, wout_ref[...], preferred_element_type=F32)
    sh = mod_ref[0, 3:4, :]
    sc = mod_ref[0, 4:5, :]
    hb = (_row_rms(x, g2_ref[...]) * (1.0 + sc) + sh).astype(BF16)
    dff = wg_ref.shape[1]
    for c in range(dff // chunk):
        sl = slice(c * chunk, (c + 1) * chunk)
        g = jnp.dot(hb, wg_ref[:, sl], preferred_element_type=F32)
        up = jnp.dot(hb, wu_ref[:, sl], preferred_element_type=F32)
        act_ref[:, sl] = (g * jax.nn.sigmoid(g) * up).astype(BF16)
    out = jnp.dot(act_ref[...], wd_ref[...], preferred_element_type=F32)
    o_ref[0] = x + mod_ref[0, 5:6, :] * out


def _mix_ffn_call(x, mod, yna_t, ym_t, u_t, gb_t, p, tm):
    B, T, D = x.shape
    nb = tm // LANES
    nlb = T // LANES
    dff = p["w_g"].shape[1]
    chan = lambda c: pl.BlockSpec((1, c, tm), lambda b, i: (b, 0, i))
    consts = [p["conv_w"], p["conv_b"], p["out_g"], p["w_out"], p["g2"], p["w_g"], p["w_u"], p["w_d"]]
    return pl.pallas_call(
        functools.partial(_mix_ffn_kernel, chunk=256),
        grid=(B, T // tm),
        in_specs=[
            pl.BlockSpec((1, tm, D), lambda b, i: (b, i, 0)),
            pl.BlockSpec((1, 6, D), lambda b, i: (b, 0, 0)),
            chan(NA_HEADS * HEAD_DIM), chan(MLA_HEADS * MLA_V), chan(CONV_WIDTH),
            pl.BlockSpec((1, CONV_WIDTH, LANES), lambda b, i: (b, 0, jnp.maximum(i * nb - 1, 0))),
            pl.BlockSpec((1, CONV_WIDTH, LANES), lambda b, i: (b, 0, jnp.minimum((i + 1) * nb, nlb - 1))),
            chan(CONV_WIDTH),
        ] + [_const_spec(a.shape) for a in consts],
        out_specs=pl.BlockSpec((1, tm, D), lambda b, i: (b, i, 0)),
        out_shape=jax.ShapeDtypeStruct((B, T, D), F32),
        scratch_shapes=[pltpu.VMEM((tm, dff), BF16)],
        compiler_params=_cparams(("parallel", "parallel")),
        name="mix_ffn",
    )(x, mod, yna_t, ym_t, u_t, u_t, u_t, gb_t, *consts)


def _pad_heads(w, heads, width):
    k = w.shape[0]
    w = w.reshape(k, heads, width)
    return jnp.pad(w, ((0, 0), (0, 0), (0, LANES - width))).reshape(k, heads * LANES)


def _lane_row(parts, repeat=1):
    row = jnp.zeros((1, LANES), F32)
    for off, v in parts:
        row = row.at[0, off:off + v.shape[0]].set(v.astype(F32))
    return jnp.tile(row, (1, repeat))


def _norm_matrices():
    mla = np.zeros((LANES, LANES), np.float32)
    mla[:MLA_NOPE, :MLA_NOPE] = 1.0 / MLA_NOPE
    mla[MLA_NOPE:MLA_NOPE + MLA_ROPE, MLA_NOPE:MLA_NOPE + MLA_ROPE] = 1.0 / MLA_ROPE
    na = np.zeros((LANES, LANES), np.float32)
    na[:HEAD_DIM, :HEAD_DIM] = 1.0 / HEAD_DIM
    na[HEAD_DIM:, HEAD_DIM:] = 1.0 / HEAD_DIM
    z = np.zeros((LANES, LANES), np.float32)
    pair = lambda m: jnp.asarray(np.block([[m, z], [z, m]]), BF16)
    return pair(mla), pair(na)


def _layer_params(l, tm, norm1_g, norm2_g, w_in, na_q_g, na_k_g, mla_q_a_g, mla_kv_a_g, mla_w_uq,
                  mla_w_ukv, mla_qn_g, mla_kn_g, mla_qr_g, mla_kr_g, conv_w, conv_b, out_norm_g,
                  w_out, w_gu, w_down):
    naw = NA_HEADS * HEAD_DIM
    i0 = 3 * naw
    i1 = i0 + MLA_Q_RANK
    i2 = i1 + MLA_KV_RANK
    i3 = i2 + MLA_ROPE
    w = w_in[l]
    d = w.shape[0]
    kr_slab = jnp.pad(w[:, i2:i3], ((0, 0), (MLA_NOPE, LANES - MLA_NOPE - MLA_ROPE)))
    w_in_r = jnp.concatenate([w[:, :i2], kr_slab, w[:, i3:]], axis=1).astype(BF16)
    ukv = mla_w_ukv[l].reshape(MLA_KV_RANK, MLA_HEADS, MLA_NOPE + MLA_V)
    w_uk = _pad_heads(ukv[:, :, :MLA_NOPE].reshape(MLA_KV_RANK, -1), MLA_HEADS, MLA_NOPE).astype(BF16)
    w_uv = ukv[:, :, MLA_NOPE:].reshape(MLA_KV_RANK, -1).astype(BF16)
    w_uq = _pad_heads(mla_w_uq[l], MLA_HEADS, MLA_NOPE + MLA_ROPE).astype(BF16)
    na_scale = HEAD_DIM ** -0.5 * float(np.log2(np.e))
    mla_scale = (MLA_NOPE + MLA_ROPE) ** -0.5 * float(np.log2(np.e))
    dff = w_down.shape[1]
    m_mla, m_na = _norm_matrices()
    return {
        "g1": norm1_g[l].reshape(1, d), "g2": norm2_g[l].reshape(1, d),
        "w_in": w_in_r, "w_uq": w_uq, "w_uk": w_uk, "w_uv": w_uv, "m2": m_mla, "m_na": m_na,
        "g_qna": _lane_row([(0, na_q_g[l] * na_scale), (HEAD_DIM, na_q_g[l] * na_scale)], 2),
        "g_kna": _lane_row([(0, na_k_g[l]), (HEAD_DIM, na_k_g[l])], 2),
        "g_qa": mla_q_a_g[l].reshape(1, -1), "g_kva": mla_kv_a_g[l].reshape(1, -1),
        "g_q": _lane_row([(0, mla_qn_g[l] * mla_scale), (MLA_NOPE, mla_qr_g[l] * mla_scale)], 2),
        "g_k": _lane_row([(0, mla_kn_g[l])], 2),
        "g_kr": _lane_row([(MLA_NOPE, mla_kr_g[l])]),
        "conv_w": jnp.broadcast_to(conv_w[l][:, :, None], (3, CONV_WIDTH, tm)),
        "conv_b": jnp.broadcast_to(conv_b[l][:, None], (CONV_WIDTH, tm)),
        "out_g": out_norm_g[l].reshape(1, -1),
        "w_out": w_out[l].astype(BF16),
        "w_g": w_gu[l][:, :dff].astype(BF16),
        "w_u": w_gu[l][:, dff:].astype(BF16),
        "w_d": w_down[l].astype(BF16),
    }


def kernel(x, c, positions, norm1_g, norm2_g, w_ada, b_ada, w_in, na_q_g, na_k_g, na_rpb, mla_q_a_g,
           mla_kv_a_g, mla_w_uq, mla_w_ukv, mla_qn_g, mla_kn_g, mla_qr_g, mla_kr_g, conv_w, conv_b,
           out_norm_g, w_out, w_gu, w_down):
    B, T, D = x.shape
    depth = w_in.shape[0]
    rows = T // GRID_W
    tm = 512
    tm_proj = 1024
    mod = _ada_modulation(c, w_ada, b_ada)
    cos_t, sin_t = _rope_tables(positions)
    na_bias = _na_bias_tables(na_rpb, rows)
    for l in range(depth):
        p = _layer_params(l, tm, norm1_g, norm2_g, w_in, na_q_g, na_k_g, mla_q_a_g, mla_kv_a_g,
                          mla_w_uq, mla_w_ukv, mla_qn_g, mla_kn_g, mla_qr_g, mla_kr_g, conv_w,
                          conv_b, out_norm_g, w_out, w_gu, w_down)
        qna, kna, vna_t, qm, km, vm_t, u_t, gb_t = _proj_call(x, mod[l], cos_t, sin_t, p, min(tm_proj, T))
        yna_t = _na_call(qna, kna, vna_t, na_bias, l, sw=256)
        ym_t = _mla_call(qm, km, vm_t, tq=1024, tk=MLA_TK, sw=256, unroll=min(8, T // MLA_TK))
        x = _mix_ffn_call(x, mod[l], yna_t, ym_t, u_t, gb_t, p, tm)
    return x
```

```python
import functools

import jax
import jax.numpy as jnp
import numpy as np
from jax import lax
from jax.experimental import pallas as pl
from jax.experimental.pallas import tpu as pltpu

F32 = jnp.float32
BF16 = jnp.bfloat16

GRID_W = 64
HEAD_DIM = 64
NA_HEADS = 4
NA_KR = 8
NA_KC = 16
MLA_HEADS = 8
MLA_NOPE = 64
MLA_ROPE = 32
MLA_V = 64
MLA_Q_RANK = 384
MLA_KV_RANK = 256
CONV_WIDTH = 256
ROPE_THETA = 10000.0
EPS = 1e-6

LANES = 128
NA_Q_ROWS = 8
ONES_ROWS = 16
MLA_VROWS = MLA_V + ONES_ROWS
NA_VROWS = HEAD_DIM + ONES_ROWS
NA_K_ROWS = 16
MLA_TK = 512
MASK_VALUE = -1e30
VMEM_LIMIT = 56 * 1024 * 1024


def _cparams(sem):
    return pltpu.CompilerParams(dimension_semantics=sem, vmem_limit_bytes=VMEM_LIMIT)


def _const_spec(shape):
    nd = len(shape)
    return pl.BlockSpec(shape, lambda *_: (0,) * nd, pipeline_mode=pl.Buffered(1))


def _split_bf16(x):
    hi = x.astype(BF16)
    lo = (x - hi.astype(F32)).astype(BF16)
    return hi, lo


def _ada_kernel(c_ref, w_ref, b_ref, o_ref):
    c = c_ref[...]
    a = c * jax.nn.sigmoid(c)
    a_hi, a_lo = _split_bf16(a)
    w_hi, w_lo = _split_bf16(w_ref[0])
    acc = jnp.dot(a_hi, w_hi, preferred_element_type=F32)
    acc += jnp.dot(a_lo, w_hi, preferred_element_type=F32)
    acc += jnp.dot(a_hi, w_lo, preferred_element_type=F32)
    o_ref[0] = acc + b_ref[0]


def _ada_modulation(c, w_ada, b_ada):
    L, D, N = w_ada.shape
    B = c.shape[0]
    rows = 8
    c_pad = jnp.zeros((rows, D), F32).at[:B].set(c)
    tn = 1536
    out = pl.pallas_call(
        _ada_kernel,
        grid=(L, N // tn),
        in_specs=[
            pl.BlockSpec((rows, D), lambda l, j: (0, 0)),
            pl.BlockSpec((1, D, tn), lambda l, j: (l, 0, j)),
            pl.BlockSpec((1, 1, tn), lambda l, j: (l, 0, j)),
        ],
        out_specs=pl.BlockSpec((1, rows, tn), lambda l, j: (l, 0, j)),
        out_shape=jax.ShapeDtypeStruct((L, rows, N), F32),
        compiler_params=_cparams(("parallel", "parallel")),
        name="ada_mod",
    )(c_pad, w_ada, b_ada.reshape(L, 1, N))
    return out[:, :B].reshape(L, B, 6, D)


def _rope_kernel(pos_ref, inv_ref, sgn_ref, cos_ref, sin_ref):
    ang = pos_ref[0].astype(F32) * inv_ref[...]
    cos_ref[0] = jnp.cos(ang)
    sin_ref[0] = jnp.sin(ang) * sgn_ref[...]


def _rope_tables(positions):
    B, T = positions.shape
    half = MLA_ROPE // 2
    inv = ROPE_THETA ** (-jnp.arange(0, MLA_ROPE, 2, dtype=F32) / MLA_ROPE)
    inv_lane = jnp.zeros((1, LANES), F32)
    inv_lane = inv_lane.at[0, MLA_NOPE:MLA_NOPE + half].set(inv)
    inv_lane = inv_lane.at[0, MLA_NOPE + half:MLA_NOPE + 2 * half].set(inv)
    sgn = np.zeros((1, LANES), np.float32)
    sgn[0, MLA_NOPE:MLA_NOPE + half] = -1.0
    sgn[0, MLA_NOPE + half:MLA_NOPE + 2 * half] = 1.0
    tm = min(T, 2048)
    spec = pl.BlockSpec((1, tm, LANES), lambda b, i: (b, i, 0))
    return pl.pallas_call(
        _rope_kernel,
        grid=(B, T // tm),
        in_specs=[
            pl.BlockSpec((1, tm, 1), lambda b, i: (b, i, 0)),
            _const_spec((1, LANES)),
            _const_spec((1, LANES)),
        ],
        out_specs=[spec, spec],
        out_shape=[jax.ShapeDtypeStruct((B, T, LANES), F32)] * 2,
        compiler_params=_cparams(("parallel", "parallel")),
        name="rope_tables",
    )(positions.reshape(B, T, 1), inv_lane, jnp.asarray(sgn))


def _pair_ms(x, m_ref):
    return jnp.dot((x * x).astype(BF16), m_ref[...], preferred_element_type=F32)


def _slab_rope(xs, cos, sin, first_half):
    half = MLA_ROPE // 2
    w = xs.shape[1]
    partner = jnp.where(first_half,
                        pltpu.roll(xs, w - half, 1),
                        pltpu.roll(xs, half, 1))
    return xs * cos + partner * sin


def _store_values(v_ref, vt, heads, width):
    ones = jnp.ones((ONES_ROWS, vt.shape[1]), BF16)
    rows = width + ONES_ROWS
    for hd in range(heads):
        v_ref[0, hd * rows:hd * rows + width, :] = vt[hd * width:(hd + 1) * width]
        v_ref[0, hd * rows + width:(hd + 1) * rows, :] = ones


def _row_rms(x, gain):
    ms = jnp.mean(x * x, axis=-1, keepdims=True)
    return x * lax.rsqrt(ms + EPS) * gain


def _proj_kernel(x_ref, mod_ref, g1_ref, win_ref, wuq_ref, wuk_ref, wuv_ref, m2_ref, mna_ref,
                 gqna_ref, gkna_ref, gqa_ref, gkva_ref, gq_ref, gk_ref, gkr_ref, cos_ref, sin_ref,
                 qna_ref, kna_ref, vna_ref, qm_ref, km_ref, vm_ref, u_ref, gb_ref):
    x = x_ref[0]
    sh = mod_ref[0, 0:1, :]
    sc = mod_ref[0, 1:2, :]
    h = _row_rms(x, g1_ref[...]) * (1.0 + sc) + sh
    hb = h.astype(BF16)

    nq = NA_HEADS * HEAD_DIM
    pair = 2 * LANES
    cw = CONV_WIDTH
    pairs = range(0, MLA_HEADS * LANES, pair)

    pall = jnp.dot(hb, win_ref[...], preferred_element_type=F32)
    o = 0
    pq = pall[:, o:o + nq]; o += nq
    pk = pall[:, o:o + nq]; o += nq
    pv = pall[:, o:o + nq]; o += nq
    cq = pall[:, o:o + MLA_Q_RANK]; o += MLA_Q_RANK
    ckv = pall[:, o:o + MLA_KV_RANK]; o += MLA_KV_RANK
    kr = pall[:, o:o + LANES]; o += LANES
    pc = pall[:, o:o + 3 * cw]

    cq = _row_rms(cq, gqa_ref[...]).astype(BF16)
    ckv = _row_rms(ckv, gkva_ref[...]).astype(BF16)
    q = jnp.dot(cq, wuq_ref[...], preferred_element_type=F32)
    kn = jnp.dot(ckv, wuk_ref[...], preferred_element_type=F32)
    vt = jnp.dot(ckv, wuv_ref[...], preferred_element_type=F32)

    ms_qna = _pair_ms(pq, mna_ref)
    ms_kna = _pair_ms(pk, mna_ref)
    ms_q = [_pair_ms(q[:, s0:s0 + pair], m2_ref) for s0 in pairs]
    ms_k = [_pair_ms(kn[:, s0:s0 + pair], m2_ref) for s0 in pairs]

    qna_ref[0] = (pq * lax.rsqrt(ms_qna + EPS) * gqna_ref[...]).T.astype(BF16)
    kna_ref[0] = (pk * lax.rsqrt(ms_kna + EPS) * gkna_ref[...]).astype(BF16)
    _store_values(vna_ref, pv.T.astype(BF16), NA_HEADS, HEAD_DIM)

    cos = cos_ref[0]
    sin = sin_ref[0]
    cos2 = jnp.concatenate([cos, cos], axis=1)
    sin2 = jnp.concatenate([sin, sin], axis=1)
    lane = lax.broadcasted_iota(jnp.int32, (1, pair), 1) % LANES
    first_half = lane < MLA_NOPE + MLA_ROPE // 2

    for s0, ms in zip(pairs, ms_q):
        qs = q[:, s0:s0 + pair] * lax.rsqrt(ms + EPS) * gq_ref[...]
        qm_ref[0, s0:s0 + pair, :] = _slab_rope(qs, cos2, sin2, first_half).T.astype(BF16)

    kr = kr * lax.rsqrt(jnp.sum(kr * kr, axis=-1, keepdims=True) * (1.0 / MLA_ROPE) + EPS) * gkr_ref[...]
    kr = _slab_rope(kr, cos, sin, first_half[:, :LANES])
    kr2 = jnp.concatenate([kr, kr], axis=1)
    for s0, ms in zip(pairs, ms_k):
        kn_s = kn[:, s0:s0 + pair] * lax.rsqrt(ms + EPS) * gk_ref[...]
        km_ref[0, :, s0:s0 + pair] = (kn_s + kr2).astype(BF16)
    _store_values(vm_ref, vt.T.astype(BF16), MLA_HEADS, MLA_V)

    u_ref[0] = (pc[:, 2 * cw:3 * cw] * pc[:, 0:cw]).T
    gb_ref[0] = pc[:, cw:2 * cw].T


def _proj_call(x, mod, cos_t, sin_t, p, tm):
    B, T, D = x.shape
    nq = NA_HEADS * HEAD_DIM
    nm = MLA_HEADS * LANES
    tok = lambda w: pl.BlockSpec((1, tm, w), lambda b, i: (b, i, 0))
    chan = lambda c: pl.BlockSpec((1, c, tm), lambda b, i: (b, 0, i))
    consts = [p["g1"], p["w_in"], p["w_uq"], p["w_uk"], p["w_uv"], p["m2"], p["m_na"], p["g_qna"],
              p["g_kna"], p["g_qa"], p["g_kva"], p["g_q"], p["g_k"], p["g_kr"]]
    in_specs = ([tok(D), pl.BlockSpec((1, 6, D), lambda b, i: (b, 0, 0))]
                + [_const_spec(a.shape) for a in consts] + [tok(LANES), tok(LANES)])
    out_shape = [
        jax.ShapeDtypeStruct((B, nq, T), BF16), jax.ShapeDtypeStruct((B, T, nq), BF16),
        jax.ShapeDtypeStruct((B, NA_HEADS * NA_VROWS, T), BF16),
        jax.ShapeDtypeStruct((B, nm, T), BF16), jax.ShapeDtypeStruct((B, T, nm), BF16),
        jax.ShapeDtypeStruct((B, MLA_HEADS * MLA_VROWS, T), BF16),
        jax.ShapeDtypeStruct((B, CONV_WIDTH, T), F32), jax.ShapeDtypeStruct((B, CONV_WIDTH, T), F32),
    ]
    out_specs = [chan(nq), tok(nq), chan(NA_HEADS * NA_VROWS), chan(nm), tok(nm),
                 chan(MLA_HEADS * MLA_VROWS), chan(CONV_WIDTH), chan(CONV_WIDTH)]
    return pl.pallas_call(
        _proj_kernel,
        grid=(B, T // tm),
        in_specs=in_specs,
        out_specs=out_specs,
        out_shape=out_shape,
        compiler_params=_cparams(("parallel", "parallel")),
        name="in_proj",
    )(x, mod, *consts, cos_t, sin_t)


def _na_kernel(q_ref, k_ref, v_ref, bias_ref, o_ref, s_ref, mx_ref, *, rows, sw):
    rb = pl.program_id(1)
    kstart = jnp.clip(rb * NA_Q_ROWS - (NA_K_ROWS - NA_Q_ROWS) // 2, 0, rows - NA_K_ROWS) * GRID_W
    kstart = pl.multiple_of(kstart, 256)
    nk = NA_K_ROWS * GRID_W
    kh = nk // 2
    nq = NA_Q_ROWS * GRID_W
    units = [(h, c) for h in range(NA_HEADS) for c in range(nq // sw)]
    chan = lax.broadcasted_iota(jnp.int32, (LANES, 1), 0)

    def score(i):
        h, c = units[i]
        slab = slice((h // 2) * LANES, (h // 2 + 1) * LANES)
        q = q_ref[0, slab, c * sw:(c + 1) * sw]
        q = jnp.where(chan >= HEAD_DIM if h % 2 else chan < HEAD_DIM, q, jnp.zeros_like(q))
        mx = None
        for r0 in range(0, nk, kh):
            k = k_ref[0, pl.ds(pl.multiple_of(kstart + r0, 256), kh), slab]
            s = (jnp.dot(k, q, preferred_element_type=F32)
                 + bias_ref[0, 0, h, r0:r0 + kh, c * sw:(c + 1) * sw])
            s_ref[i % 3, r0:r0 + kh, :] = s
            part = jnp.max(s, axis=0, keepdims=True)
            mx = part if mx is None else jnp.maximum(mx, part)
        mx_ref[i % 3] = mx

    score(0)
    score(1)
    for i, (h, c) in enumerate(units):
        p = jnp.exp2(s_ref[i % 3] - mx_ref[i % 3]).astype(BF16)
        v = v_ref[0, h * NA_VROWS:(h + 1) * NA_VROWS, pl.ds(kstart, nk)]
        acc = jnp.dot(v, p, preferred_element_type=F32)
        o_ref[0, h * HEAD_DIM:(h + 1) * HEAD_DIM, c * sw:(c + 1) * sw] = (
            acc[:HEAD_DIM] / acc[HEAD_DIM:HEAD_DIM + 1])
        if i + 2 < len(units):
            score(i + 2)


def _na_bias_kernel(e_ref, o_ref, *, rows):
    half = (NA_K_ROWS - NA_Q_ROWS) // 2
    lane = lax.broadcasted_iota(jnp.int32, (GRID_W, LANES), 1)
    masked = jnp.full((GRID_W, LANES), MASK_VALUE, F32)
    for v, (r0, ks) in enumerate(((0, 0), (NA_Q_ROWS, NA_Q_ROWS - half), (rows - NA_Q_ROWS, rows - NA_K_ROWS))):
        for kr in range(NA_K_ROWS):
            for pair in range(NA_Q_ROWS // 2):
                blocks = []
                for qr in (2 * pair, 2 * pair + 1):
                    r, k = r0 + qr, ks + kr
                    row_start = min(max(r - NA_KR // 2, 0), rows - NA_KR)
                    ok = row_start <= k < row_start + NA_KR
                    blocks.append(e_ref[0, 0, k - r + NA_KR - 1] if ok else masked)
                o_ref[0, v, 0, kr * GRID_W:(kr + 1) * GRID_W, pair * LANES:(pair + 1) * LANES] = (
                    jnp.where(lane < GRID_W, blocks[0], blocks[1]))


def _na_bias_tables(rpb, rows):
    L, H = rpb.shape[:2]
    cols = np.arange(GRID_W)
    col_start = np.clip(cols - NA_KC // 2, 0, GRID_W - NA_KC)
    col_ok = (cols[None, :] >= col_start[:, None]) & (cols[None, :] < col_start[:, None] + NA_KC)
    dc = np.clip(cols[None, :] - cols[:, None] + NA_KC - 1, 0, 2 * NA_KC - 2)
    sel = np.eye(2 * NA_KC - 1, dtype=np.float32)[dc.T]
    sel = np.concatenate([sel, sel], axis=1)
    ok = np.concatenate([col_ok.T, col_ok.T], axis=1)
    e = jnp.einsum("lhij,wqj->lhiwq", rpb.astype(F32) * float(np.log2(np.e)), jnp.asarray(sel),
                   precision=lax.Precision.HIGHEST)
    e = jnp.where(jnp.asarray(ok), e, MASK_VALUE)
    nk, nq = NA_K_ROWS * GRID_W, NA_Q_ROWS * GRID_W
    return pl.pallas_call(
        functools.partial(_na_bias_kernel, rows=rows),
        grid=(L, H),
        in_specs=[pl.BlockSpec((1, 1, 2 * NA_KR - 1, GRID_W, LANES), lambda l, h: (l, h, 0, 0, 0))],
        out_specs=pl.BlockSpec((1, 3, 1, nk, nq), lambda l, h: (l, 0, h, 0, 0)),
        out_shape=jax.ShapeDtypeStruct((L, 3, H, nk, nq), F32),
        compiler_params=_cparams(("parallel", "parallel")),
        name="na_bias",
    )(e)


def _na_call(qna, kna, vna_t, bias, layer, sw):
    B, T, _ = kna.shape
    rows = T // GRID_W
    nrb = rows // NA_Q_ROWS
    nq = NA_Q_ROWS * GRID_W
    nk = NA_K_ROWS * GRID_W

    def bias_map(b, rb):
        return (layer, jnp.where(rb == 0, 0, jnp.where(rb == nrb - 1, 2, 1)), 0, 0, 0)

    return pl.pallas_call(
        functools.partial(_na_kernel, rows=rows, sw=sw),
        grid=(B, nrb),
        in_specs=[
            pl.BlockSpec((1, NA_HEADS * HEAD_DIM, nq), lambda b, rb: (b, 0, rb)),
            pl.BlockSpec((1, T, NA_HEADS * HEAD_DIM), lambda b, rb: (b, 0, 0)),
            pl.BlockSpec((1, NA_HEADS * NA_VROWS, T), lambda b, rb: (b, 0, 0)),
            pl.BlockSpec((1, 1, NA_HEADS, nk, nq), bias_map),
        ],
        out_specs=pl.BlockSpec((1, NA_HEADS * HEAD_DIM, nq), lambda b, rb: (b, 0, rb)),
        out_shape=jax.ShapeDtypeStruct((B, NA_HEADS * HEAD_DIM, T), F32),
        scratch_shapes=[pltpu.VMEM((3, nk, sw), F32), pltpu.VMEM((3, 1, sw), F32)],
        compiler_params=_cparams(("parallel", "arbitrary")),
        name="na_attn",
    )(qna, kna, vna_t, bias)


def _mla_kernel(q_ref, k_ref, v_ref, o_ref, s_ref, mx_ref, *, tq, tk, sw, unroll):
    T = k_ref.shape[1]
    nq = T // tq
    nkv = T // tk
    strips = [slice(c * sw, (c + 1) * sw) for c in range(tq // sw)]
    ns = len(strips)
    hk = tk // 2

    def score(slot, qi, j, c):
        k = k_ref[0, pl.ds(pl.multiple_of(j * tk, tk), tk), :]
        q = q_ref[0, :, pl.ds(pl.multiple_of(qi * tq + c * sw, sw), sw)]
        s = jnp.dot(k, q, preferred_element_type=F32)
        s_ref[slot, :, strips[c]] = s
        mx_ref[slot, :, strips[c]] = jnp.max(s, axis=0, keepdims=True)

    def step(slot, j, nxt_qi, nxt_j, m, acc):
        v = v_ref[0, :, pl.ds(pl.multiple_of(j * tk, tk), tk)]
        ms, accs = [], []
        for c in range(min(2, ns)):
            score(1 - slot, nxt_qi, nxt_j, c)
        for c, sl in enumerate(strips):
            m_old = m[:, sl]
            m_new = jnp.maximum(m_old, mx_ref[slot, :, sl])
            p = jnp.exp2(s_ref[slot, :, sl] - m_new).astype(BF16)
            pv = jnp.dot(v[:, :hk], p[:hk], preferred_element_type=F32)
            if c + 2 < ns:
                score(1 - slot, nxt_qi, nxt_j, c + 2)
            pv = pv + jnp.dot(v[:, hk:], p[hk:], preferred_element_type=F32)
            accs.append(jnp.exp2(m_old - m_new) * acc[:, sl] + pv)
            ms.append(m_new)
        return jnp.concatenate(ms, axis=1), jnp.concatenate(accs, axis=1)

    def body(bi, carry):
        m, acc = carry
        t0 = bi * unroll
        qi = t0 // nkv
        j0 = t0 % nkv
        fresh = j0 == 0
        m = jnp.where(fresh, -jnp.inf, m)
        acc = jnp.where(fresh, 0.0, acc)
        for u in range(unroll):
            if u + 1 < unroll:
                nxt_qi, nxt_j = qi, j0 + u + 1
            else:
                nxt_qi = jnp.minimum(qi + (j0 + unroll) // nkv, nq - 1)
                nxt_j = (j0 + unroll) % nkv
            m, acc = step(u % 2, j0 + u, nxt_qi, nxt_j, m, acc)

        @pl.when(j0 + unroll == nkv)
        def _():
            o_ref[0, :, pl.ds(pl.multiple_of(qi * tq, tq), tq)] = acc[:MLA_V] / acc[MLA_V:MLA_V + 1]

        return m, acc

    for c in range(ns):
        score(0, 0, 0, c)
    init = (jnp.full((1, tq), -jnp.inf, F32), jnp.zeros((MLA_VROWS, tq), F32))
    lax.fori_loop(0, nq * nkv // unroll, body, init)


def _mla_call(qm, km, vm_t, tq, tk, sw, unroll):
    B, T, _ = km.shape
    assert unroll % 2 == 0 and (T // tk) % unroll == 0 and T % tq == 0 and tq % sw == 0
    return pl.pallas_call(
        functools.partial(_mla_kernel, tq=tq, tk=tk, sw=sw, unroll=unroll),
        grid=(B, MLA_HEADS),
        in_specs=[
            pl.BlockSpec((1, LANES, T), lambda b, h: (b, h, 0)),
            pl.BlockSpec((1, T, LANES), lambda b, h: (b, 0, h)),
            pl.BlockSpec((1, MLA_VROWS, T), lambda b, h: (b, h, 0)),
        ],
        out_specs=pl.BlockSpec((1, MLA_V, T), lambda b, h: (b, h, 0)),
        out_shape=jax.ShapeDtypeStruct((B, MLA_HEADS * MLA_V, T), F32),
        scratch_shapes=[pltpu.VMEM((2, tk, tq), F32), pltpu.VMEM((2, 1, tq), F32)],
        compiler_params=_cparams(("parallel", "parallel")),
        name="mla_attn",
    )(qm, km, vm_t)


def _group_rms_rows(x):
    c, tm = x.shape
    xg = x.reshape(c // HEAD_DIM, HEAD_DIM, tm)
    ms = jnp.mean(xg * xg, axis=1, keepdims=True)
    return (xg * lax.rsqrt(ms + EPS)).reshape(c, tm)


def _mix_ffn_kernel(x_ref, mod_ref, yna_ref, ym_ref, u_ref, up_ref, un_ref, gb_ref, cw_ref, cb_ref,
                    og_ref, wout_ref, g2_ref, wg_ref, wu_ref, wd_ref, o_ref, act_ref, *, chunk):
    i = pl.program_id(1)
    last = pl.num_programs(1) - 1
    u = u_ref[0]
    tm = u.shape[1]
    prev = jnp.where(i > 0, up_ref[0], 0.0)
    nxt = jnp.where(i < last, un_ref[0], 0.0)
    ext = jnp.concatenate([prev, u, nxt], axis=1)
    w = ext.shape[1]
    u_m1 = pltpu.roll(ext, 1, 1)[:, LANES:LANES + tm]
    u_p1 = pltpu.roll(ext, w - 1, 1)[:, LANES:LANES + tm]
    y = cw_ref[0] * u_m1 + cw_ref[1] * u + cw_ref[2] * u_p1 + cb_ref[...]
    yc = gb_ref[0] * y
    mixed = jnp.concatenate([_group_rms_rows(yna_ref[0]), _group_rms_rows(ym_ref[0]),
                             _group_rms_rows(yc)], axis=0)
    mixed = (mixed.T * og_ref[...]).astype(BF16)
    x = x_ref[0] + mod_ref[0, 2:3, :] * jnp.dot(mixed, wout_ref[...], preferred_element_type=F32)
    sh = mod_ref[0, 3:4, :]
    sc = mod_ref[0, 4:5, :]
    hb = (_row_rms(x, g2_ref[...]) * (1.0 + sc) + sh).astype(BF16)
    dff = wg_ref.shape[1]
    for c in range(dff // chunk):
        sl = slice(c * chunk, (c + 1) * chunk)
        g = jnp.dot(hb, wg_ref[:, sl], preferred_element_type=F32)
        up = jnp.dot(hb, wu_ref[:, sl], preferred_element_type=F32)
        act_ref[:, sl] = (g * jax.nn.sigmoid(g) * up).astype(BF16)
    out = jnp.dot(act_ref[...], wd_ref[...], preferred_element_type=F32)
    o_ref[0] = x + mod_ref[0, 5:6, :] * out


def _mix_ffn_call(x, mod, yna_t, ym_t, u_t, gb_t, p, tm):
    B, T, D = x.shape
    nb = tm // LANES
    nlb = T // LANES
    dff = p["w_g"].shape[1]
    chan = lambda c: pl.BlockSpec((1, c, tm), lambda b, i: (b, 0, i))
    consts = [p["conv_w"], p["conv_b"], p["out_g"], p["w_out"], p["g2"], p["w_g"], p["w_u"], p["w_d"]]
    return pl.pallas_call(
        functools.partial(_mix_ffn_kernel, chunk=256),
        grid=(B, T // tm),
        in_specs=[
            pl.BlockSpec((1, tm, D), lambda b, i: (b, i, 0)),
            pl.BlockSpec((1, 6, D), lambda b, i: (b, 0, 0)),
            chan(NA_HEADS * HEAD_DIM), chan(MLA_HEADS * MLA_V), chan(CONV_WIDTH),
            pl.BlockSpec((1, CONV_WIDTH, LANES), lambda b, i: (b, 0, jnp.maximum(i * nb - 1, 0))),
            pl.BlockSpec((1, CONV_WIDTH, LANES), lambda b, i: (b, 0, jnp.minimum((i + 1) * nb, nlb - 1))),
            chan(CONV_WIDTH),
        ] + [_const_spec(a.shape) for a in consts],
        out_specs=pl.BlockSpec((1, tm, D), lambda b, i: (b, i, 0)),
        out_shape=jax.ShapeDtypeStruct((B, T, D), F32),
        scratch_shapes=[pltpu.VMEM((tm, dff), BF16)],
        compiler_params=_cparams(("parallel", "parallel")),
        name="mix_ffn",
    )(x, mod, yna_t, ym_t, u_t, u_t, u_t, gb_t, *consts)


def _pad_heads(w, heads, width):
    k = w.shape[0]
    w = w.reshape(k, heads, width)
    return jnp.pad(w, ((0, 0), (0, 0), (0, LANES - width))).reshape(k, heads * LANES)


def _lane_row(parts, repeat=1):
    row = jnp.zeros((1, LANES), F32)
    for off, v in parts:
        row = row.at[0, off:off + v.shape[0]].set(v.astype(F32))
    return jnp.tile(row, (1, repeat))


def _norm_matrices():
    mla = np.zeros((LANES, LANES), np.float32)
    mla[:MLA_NOPE, :MLA_NOPE] = 1.0 / MLA_NOPE
    mla[MLA_NOPE:MLA_NOPE + MLA_ROPE, MLA_NOPE:MLA_NOPE + MLA_ROPE] = 1.0 / MLA_ROPE
    na = np.zeros((LANES, LANES), np.float32)
    na[:HEAD_DIM, :HEAD_DIM] = 1.0 / HEAD_DIM
    na[HEAD_DIM:, HEAD_DIM:] = 1.0 / HEAD_DIM
    z = np.zeros((LANES, LANES), np.float32)
    pair = lambda m: jnp.asarray(np.block([[m, z], [z, m]]), BF16)
    return pair(mla), pair(na)


def _layer_params(l, tm, norm1_g, norm2_g, w_in, na_q_g, na_k_g, mla_q_a_g, mla_kv_a_g, mla_w_uq,
                  mla_w_ukv, mla_qn_g, mla_kn_g, mla_qr_g, mla_kr_g, conv_w, conv_b, out_norm_g,
                  w_out, w_gu, w_down):
    naw = NA_HEADS * HEAD_DIM
    i0 = 3 * naw
    i1 = i0 + MLA_Q_RANK
    i2 = i1 + MLA_KV_RANK
    i3 = i2 + MLA_ROPE
    w = w_in[l]
    d = w.shape[0]
    kr_slab = jnp.pad(w[:, i2:i3], ((0, 0), (MLA_NOPE, LANES - MLA_NOPE - MLA_ROPE)))
    w_in_r = jnp.concatenate([w[:, :i2], kr_slab, w[:, i3:]], axis=1).astype(BF16)
    ukv = mla_w_ukv[l].reshape(MLA_KV_RANK, MLA_HEADS, MLA_NOPE + MLA_V)
    w_uk = _pad_heads(ukv[:, :, :MLA_NOPE].reshape(MLA_KV_RANK, -1), MLA_HEADS, MLA_NOPE).astype(BF16)
    w_uv = ukv[:, :, MLA_NOPE:].reshape(MLA_KV_RANK, -1).astype(BF16)
    w_uq = _pad_heads(mla_w_uq[l], MLA_HEADS, MLA_NOPE + MLA_ROPE).astype(BF16)
    na_scale = HEAD_DIM ** -0.5 * float(np.log2(np.e))
    mla_scale = (MLA_NOPE + MLA_ROPE) ** -0.5 * float(np.log2(np.e))
    dff = w_down.shape[1]
    m_mla, m_na = _norm_matrices()
    return {
        "g1": norm1_g[l].reshape(1, d), "g2": norm2_g[l].reshape(1, d),
        "w_in": w_in_r, "w_uq": w_uq, "w_uk": w_uk, "w_uv": w_uv, "m2": m_mla, "m_na": m_na,
        "g_qna": _lane_row([(0, na_q_g[l] * na_scale), (HEAD_DIM, na_q_g[l] * na_scale)], 2),
        "g_kna": _lane_row([(0, na_k_g[l]), (HEAD_DIM, na_k_g[l])], 2),
        "g_qa": mla_q_a_g[l].reshape(1, -1), "g_kva": mla_kv_a_g[l].reshape(1, -1),
        "g_q": _lane_row([(0, mla_qn_g[l] * mla_scale), (MLA_NOPE, mla_qr_g[l] * mla_scale)], 2),
        "g_k": _lane_row([(0, mla_kn_g[l])], 2),
        "g_kr": _lane_row([(MLA_NOPE, mla_kr_g[l])]),
        "conv_w": jnp.broadcast_to(conv_w[l][:, :, None], (3, CONV_WIDTH, tm)),
        "conv_b": jnp.broadcast_to(conv_b[l][:, None], (CONV_WIDTH, tm)),
        "out_g": out_norm_g[l].reshape(1, -1),
        "w_out": w_out[l].astype(BF16),
        "w_g": w_gu[l][:, :dff].astype(BF16),
        "w_u": w_gu[l][:, dff:].astype(BF16),
        "w_d": w_down[l].astype(BF16),
    }


def kernel(x, c, positions, norm1_g, norm2_g, w_ada, b_ada, w_in, na_q_g, na_k_g, na_rpb, mla_q_a_g,
           mla_kv_a_g, mla_w_uq, mla_w_ukv, mla_qn_g, mla_kn_g, mla_qr_g, mla_kr_g, conv_w, conv_b,
           out_norm_g, w_out, w_gu, w_down):
    B, T, D = x.shape
    depth = w_in.shape[0]
    rows = T // GRID_W
    tm = 512
    tm_proj = 1024
    mod = _ada_modulation(c, w_ada, b_ada)
    cos_t, sin_t = _rope_tables(positions)
    na_bias = _na_bias_tables(na_rpb, rows)
    for l in range(depth):
        p = _layer_params(l, tm, norm1_g, norm2_g, w_in, na_q_g, na_k_g, mla_q_a_g, mla_kv_a_g,
                          mla_w_uq, mla_w_ukv, mla_qn_g, mla_kn_g, mla_qr_g, mla_kr_g, conv_w,
                          conv_b, out_norm_g, w_out, w_gu, w_down)
        qna, kna, vna_t, qm, km, vm_t, u_t, gb_t = _proj_call(x, mod[l], cos_t, sin_t, p, min(tm_proj, T))
        yna_t = _na_call(qna, kna, vna_t, na_bias, l, sw=256)
        ym_t = _mla_call(qm, km, vm_t, tq=1024, tk=MLA_TK, sw=256, unroll=min(16, T // MLA_TK))
        x = _mix_ffn_call(x, mod[l], yna_t, ym_t, u_t, gb_t, p, tm)
    return x
```

```python
import functools
from typing import NamedTuple

import jax
import jax.numpy as jnp
import numpy as np
from jax import lax
from jax.experimental import pallas as pl
from jax.experimental.pallas import tpu as pltpu

F32 = jnp.float32
BF16 = jnp.bfloat16

GRID_W = 64
HEAD_DIM = 64
NA_HEADS = 4
NA_KR = 8
NA_KC = 16
MLA_HEADS = 8
MLA_NOPE = 64
MLA_ROPE = 32
MLA_V = 64
MLA_Q_RANK = 384
MLA_KV_RANK = 256
CONV_WIDTH = 256
ROPE_THETA = 10000.0
EPS = 1e-6

LANES = 128
NA_Q_ROWS = 8
ONES_ROWS = 16
MLA_VROWS = MLA_V + ONES_ROWS
NA_VROWS = HEAD_DIM + ONES_ROWS
NA_K_ROWS = 16
MASK_VALUE = -1e30
VMEM_LIMIT = 56 * 1024 * 1024


class _Tiles(NamedTuple):
    proj: int
    proj_sub: int
    mix: int
    mla_q: int
    mla_k: int
    strip: int
    mla_unroll: int


def _tiles(T):
    mla_k = 512
    return _Tiles(proj=min(1024, T), proj_sub=256, mix=512, mla_q=1024, mla_k=mla_k, strip=256,
                  mla_unroll=min(16, T // mla_k))


def _cparams(sem):
    return pltpu.CompilerParams(dimension_semantics=sem, vmem_limit_bytes=VMEM_LIMIT)


def _const_spec(shape):
    nd = len(shape)
    return pl.BlockSpec(shape, lambda *_: (0,) * nd, pipeline_mode=pl.Buffered(1))


def _split_bf16(x):
    hi = x.astype(BF16)
    lo = (x - hi.astype(F32)).astype(BF16)
    return hi, lo


def _ada_kernel(c_ref, w_ref, b_ref, o_ref):
    c = c_ref[...]
    a = c * jax.nn.sigmoid(c)
    a_hi, a_lo = _split_bf16(a)
    w_hi, w_lo = _split_bf16(w_ref[0])
    acc = jnp.dot(a_hi, w_hi, preferred_element_type=F32)
    acc += jnp.dot(a_lo, w_hi, preferred_element_type=F32)
    acc += jnp.dot(a_hi, w_lo, preferred_element_type=F32)
    o_ref[0] = acc + b_ref[0]


def _ada_modulation(c, w_ada, b_ada):
    L, D, N = w_ada.shape
    B = c.shape[0]
    rows = 8
    c_pad = jnp.zeros((rows, D), F32).at[:B].set(c)
    tn = 1536
    out = pl.pallas_call(
        _ada_kernel,
        grid=(L, N // tn),
        in_specs=[
            pl.BlockSpec((rows, D), lambda l, j: (0, 0)),
            pl.BlockSpec((1, D, tn), lambda l, j: (l, 0, j)),
            pl.BlockSpec((1, 1, tn), lambda l, j: (l, 0, j)),
        ],
        out_specs=pl.BlockSpec((1, rows, tn), lambda l, j: (l, 0, j)),
        out_shape=jax.ShapeDtypeStruct((L, rows, N), F32),
        compiler_params=_cparams(("parallel", "parallel")),
        name="ada_mod",
    )(c_pad, w_ada, b_ada.reshape(L, 1, N))
    return out[:, :B].reshape(L, B, 6, D)


def _rope_kernel(pos_ref, inv_ref, cos_ref, sin_ref):
    ang = pos_ref[0].astype(F32) * inv_ref[...]
    c = jnp.cos(ang)
    s = jnp.sin(ang)
    tm = ang.shape[1]
    pad = LANES - MLA_NOPE - MLA_ROPE
    cos_t = jnp.concatenate([jnp.ones((MLA_NOPE, tm), F32), c, c, jnp.ones((pad, tm), F32)], axis=0)
    sin_t = jnp.concatenate([jnp.zeros((MLA_NOPE, tm), F32), -s, s, jnp.zeros((pad, tm), F32)], axis=0)
    cos_ref[0] = cos_t.T
    sin_ref[0] = sin_t.T


def _rope_tables(positions):
    B, T = positions.shape
    half = MLA_ROPE // 2
    inv = ROPE_THETA ** (-jnp.arange(0, MLA_ROPE, 2, dtype=F32) / MLA_ROPE)
    tm = min(T, 2048)
    spec = pl.BlockSpec((1, tm, LANES), lambda b, i: (b, i, 0))
    return pl.pallas_call(
        _rope_kernel,
        grid=(B, T // tm),
        in_specs=[
            pl.BlockSpec((1, 1, tm), lambda b, i: (b, 0, i)),
            _const_spec((half, tm)),
        ],
        out_specs=[spec, spec],
        out_shape=[jax.ShapeDtypeStruct((B, T, LANES), F32)] * 2,
        compiler_params=_cparams(("parallel", "parallel")),
        name="rope_tables",
    )(positions.reshape(B, 1, T), jnp.broadcast_to(inv[:, None], (half, tm)))


def _pair_ms(x, m_ref):
    return jnp.dot((x * x).astype(BF16), m_ref[...], preferred_element_type=F32)


def _slab_rope(xs, cos, sin, first_half):
    half = MLA_ROPE // 2
    w = xs.shape[1]
    partner = jnp.where(first_half,
                        pltpu.roll(xs, w - half, 1),
                        pltpu.roll(xs, half, 1))
    return xs * cos + partner * sin


def _store_values(v_ref, tok, vt, heads, width):
    ones = jnp.ones((ONES_ROWS, vt.shape[1]), BF16)
    rows = width + ONES_ROWS
    for hd in range(heads):
        v_ref[0, hd * rows:hd * rows + width, tok] = vt[hd * width:(hd + 1) * width]
        v_ref[0, hd * rows + width:(hd + 1) * rows, tok] = ones


def _row_rms(x, gain):
    ms = jnp.mean(x * x, axis=-1, keepdims=True)
    return x * lax.rsqrt(ms + EPS) * gain


def _proj_kernel(x_ref, mod_ref, g1_ref, win_ref, wuq_ref, wuk_ref, wuv_ref, m2_ref, mna_ref,
                 gqna_ref, gkna_ref, gqa_ref, gkva_ref, gq_ref, gk_ref, gkr_ref, cos_ref, sin_ref,
                 qna_ref, kna_ref, vna_ref, qm_ref, km_ref, vm_ref, u_ref, gb_ref, *, sub):
    for t0 in range(0, x_ref.shape[1], sub):
        _proj_sub_tile(slice(t0, t0 + sub), x_ref, mod_ref, g1_ref, win_ref, wuq_ref, wuk_ref, wuv_ref,
                       m2_ref, mna_ref, gqna_ref, gkna_ref, gqa_ref, gkva_ref, gq_ref, gk_ref, gkr_ref,
                       cos_ref, sin_ref, qna_ref, kna_ref, vna_ref, qm_ref, km_ref, vm_ref, u_ref, gb_ref)


def _proj_sub_tile(tok, x_ref, mod_ref, g1_ref, win_ref, wuq_ref, wuk_ref, wuv_ref, m2_ref, mna_ref,
                   gqna_ref, gkna_ref, gqa_ref, gkva_ref, gq_ref, gk_ref, gkr_ref, cos_ref, sin_ref,
                   qna_ref, kna_ref, vna_ref, qm_ref, km_ref, vm_ref, u_ref, gb_ref):
    x = x_ref[0, tok, :]
    sh = mod_ref[0, 0:1, :]
    sc = mod_ref[0, 1:2, :]
    h = _row_rms(x, g1_ref[...]) * (1.0 + sc) + sh
    hb = h.astype(BF16)

    nq = NA_HEADS * HEAD_DIM
    pair = 2 * LANES
    cw = CONV_WIDTH
    pairs = range(0, MLA_HEADS * LANES, pair)

    pall = jnp.dot(hb, win_ref[...], preferred_element_type=F32)
    o = 0
    pq = pall[:, o:o + nq]; o += nq
    pk = pall[:, o:o + nq]; o += nq
    pv = pall[:, o:o + nq]; o += nq
    cq = pall[:, o:o + MLA_Q_RANK]; o += MLA_Q_RANK
    ckv = pall[:, o:o + MLA_KV_RANK]; o += MLA_KV_RANK
    kr = pall[:, o:o + LANES]; o += LANES
    pc = pall[:, o:o + 3 * cw]

    cq = _row_rms(cq, gqa_ref[...]).astype(BF16)
    ckv = _row_rms(ckv, gkva_ref[...]).astype(BF16)
    q = jnp.dot(cq, wuq_ref[...], preferred_element_type=F32)
    kn = jnp.dot(ckv, wuk_ref[...], preferred_element_type=F32)
    vt = jnp.dot(ckv, wuv_ref[...], preferred_element_type=F32)

    ms_qna = _pair_ms(pq, mna_ref)
    ms_kna = _pair_ms(pk, mna_ref)
    ms_q = [_pair_ms(q[:, s0:s0 + pair], m2_ref) for s0 in pairs]
    ms_k = [_pair_ms(kn[:, s0:s0 + pair], m2_ref) for s0 in pairs]

    qna_ref[0, :, tok] = (pq * lax.rsqrt(ms_qna + EPS) * gqna_ref[...]).T.astype(BF16)
    kna_ref[0, tok, :] = (pk * lax.rsqrt(ms_kna + EPS) * gkna_ref[...]).astype(BF16)
    _store_values(vna_ref, tok, pv.T.astype(BF16), NA_HEADS, HEAD_DIM)

    cos = cos_ref[0, tok, :]
    sin = sin_ref[0, tok, :]
    cos2 = jnp.concatenate([cos, cos], axis=1)
    sin2 = jnp.concatenate([sin, sin], axis=1)
    lane = lax.broadcasted_iota(jnp.int32, (1, pair), 1) % LANES
    first_half = lane < MLA_NOPE + MLA_ROPE // 2

    for s0, ms in zip(pairs, ms_q):
        qs = q[:, s0:s0 + pair] * lax.rsqrt(ms + EPS) * gq_ref[...]
        qm_ref[0, s0:s0 + pair, tok] = _slab_rope(qs, cos2, sin2, first_half).T.astype(BF16)

    kr = kr * lax.rsqrt(jnp.sum(kr * kr, axis=-1, keepdims=True) * (1.0 / MLA_ROPE) + EPS) * gkr_ref[...]
    kr = _slab_rope(kr, cos, sin, first_half[:, :LANES])
    kr2 = jnp.concatenate([kr, kr], axis=1)
    for s0, ms in zip(pairs, ms_k):
        kn_s = kn[:, s0:s0 + pair] * lax.rsqrt(ms + EPS) * gk_ref[...]
        km_ref[0, tok, s0:s0 + pair] = (kn_s + kr2).astype(BF16)
    _store_values(vm_ref, tok, vt.T.astype(BF16), MLA_HEADS, MLA_V)

    u_ref[0, :, tok] = (pc[:, 2 * cw:3 * cw] * pc[:, 0:cw]).T
    gb_ref[0, :, tok] = pc[:, cw:2 * cw].T


def _proj_call(x, mod, cos_t, sin_t, p, tm, sub):
    B, T, D = x.shape
    nq = NA_HEADS * HEAD_DIM
    nm = MLA_HEADS * LANES
    tok = lambda w: pl.BlockSpec((1, tm, w), lambda b, i: (b, i, 0))
    chan = lambda c: pl.BlockSpec((1, c, tm), lambda b, i: (b, 0, i))
    consts = [p["g1"], p["w_in"], p["w_uq"], p["w_uk"], p["w_uv"], p["m2"], p["m_na"], p["g_qna"],
              p["g_kna"], p["g_qa"], p["g_kva"], p["g_q"], p["g_k"], p["g_kr"]]
    in_specs = ([tok(D), pl.BlockSpec((1, 6, D), lambda b, i: (b, 0, 0))]
                + [_const_spec(a.shape) for a in consts] + [tok(LANES), tok(LANES)])
    out_shape = [
        jax.ShapeDtypeStruct((B, nq, T), BF16), jax.ShapeDtypeStruct((B, T, nq), BF16),
        jax.ShapeDtypeStruct((B, NA_HEADS * NA_VROWS, T), BF16),
        jax.ShapeDtypeStruct((B, nm, T), BF16), jax.ShapeDtypeStruct((B, T, nm), BF16),
        jax.ShapeDtypeStruct((B, MLA_HEADS * MLA_VROWS, T), BF16),
        jax.ShapeDtypeStruct((B, CONV_WIDTH, T), F32), jax.ShapeDtypeStruct((B, CONV_WIDTH, T), F32),
    ]
    out_specs = [chan(nq), tok(nq), chan(NA_HEADS * NA_VROWS), chan(nm), tok(nm),
                 chan(MLA_HEADS * MLA_VROWS), chan(CONV_WIDTH), chan(CONV_WIDTH)]
    return pl.pallas_call(
        functools.partial(_proj_kernel, sub=sub),
        grid=(B, T // tm),
        in_specs=in_specs,
        out_specs=out_specs,
        out_shape=out_shape,
        compiler_params=_cparams(("parallel", "parallel")),
        name="in_proj",
    )(x, mod, *consts, cos_t, sin_t)


def _na_kernel(q_ref, k_ref, v_ref, bias_ref, o_ref, s_ref, mx_ref, *, rows, sw):
    rb = pl.program_id(1)
    kstart = jnp.clip(rb * NA_Q_ROWS - (NA_K_ROWS - NA_Q_ROWS) // 2, 0, rows - NA_K_ROWS) * GRID_W
    kstart = pl.multiple_of(kstart, 256)
    nk = NA_K_ROWS * GRID_W
    kh = nk // 2
    nq = NA_Q_ROWS * GRID_W
    units = [(h, c) for h in range(NA_HEADS) for c in range(nq // sw)]
    chan = lax.broadcasted_iota(jnp.int32, (LANES, 1), 0)

    def score(i):
        h, c = units[i]
        slab = slice((h // 2) * LANES, (h // 2 + 1) * LANES)
        q = q_ref[0, slab, c * sw:(c + 1) * sw]
        q = jnp.where(chan >= HEAD_DIM if h % 2 else chan < HEAD_DIM, q, jnp.zeros_like(q))
        mx = None
        for r0 in range(0, nk, kh):
            k = k_ref[0, pl.ds(pl.multiple_of(kstart + r0, 256), kh), slab]
            s = (jnp.dot(k, q, preferred_element_type=F32)
                 + bias_ref[0, 0, h, r0:r0 + kh, c * sw:(c + 1) * sw])
            s_ref[i % 3, r0:r0 + kh, :] = s
            part = jnp.max(s, axis=0, keepdims=True)
            mx = part if mx is None else jnp.maximum(mx, part)
        mx_ref[i % 3] = mx

    score(0)
    score(1)
    for i, (h, c) in enumerate(units):
        p = jnp.exp2(s_ref[i % 3] - mx_ref[i % 3]).astype(BF16)
        v = v_ref[0, h * NA_VROWS:(h + 1) * NA_VROWS, pl.ds(kstart, nk)]
        acc = jnp.dot(v, p, preferred_element_type=F32)
        o_ref[0, h * HEAD_DIM:(h + 1) * HEAD_DIM, c * sw:(c + 1) * sw] = (
            acc[:HEAD_DIM] / acc[HEAD_DIM:HEAD_DIM + 1])
        if i + 2 < len(units):
            score(i + 2)


def _na_bias_kernel(e_ref, o_ref, *, rows):
    half = (NA_K_ROWS - NA_Q_ROWS) // 2
    lane = lax.broadcasted_iota(jnp.int32, (GRID_W, LANES), 1)
    masked = jnp.full((GRID_W, LANES), MASK_VALUE, F32)
    for v, (r0, ks) in enumerate(((0, 0), (NA_Q_ROWS, NA_Q_ROWS - half), (rows - NA_Q_ROWS, rows - NA_K_ROWS))):
        for kr in range(NA_K_ROWS):
            for pair in range(NA_Q_ROWS // 2):
                blocks = []
                for qr in (2 * pair, 2 * pair + 1):
                    r, k = r0 + qr, ks + kr
                    row_start = min(max(r - NA_KR // 2, 0), rows - NA_KR)
                    ok = row_start <= k < row_start + NA_KR
                    blocks.append(e_ref[0, 0, k - r + NA_KR - 1] if ok else masked)
                o_ref[0, v, 0, kr * GRID_W:(kr + 1) * GRID_W, pair * LANES:(pair + 1) * LANES] = (
                    jnp.where(lane < GRID_W, blocks[0], blocks[1]))


def _na_bias_tables(rpb, rows):
    L, H = rpb.shape[:2]
    cols = np.arange(GRID_W)
    col_start = np.clip(cols - NA_KC // 2, 0, GRID_W - NA_KC)
    col_ok = (cols[None, :] >= col_start[:, None]) & (cols[None, :] < col_start[:, None] + NA_KC)
    dc = np.clip(cols[None, :] - cols[:, None] + NA_KC - 1, 0, 2 * NA_KC - 2)
    sel = np.eye(2 * NA_KC - 1, dtype=np.float32)[dc.T]
    sel = np.concatenate([sel, sel], axis=1)
    ok = np.concatenate([col_ok.T, col_ok.T], axis=1)
    e = jnp.einsum("lhij,wqj->lhiwq", rpb.astype(F32) * float(np.log2(np.e)), jnp.asarray(sel),
                   precision=lax.Precision.HIGHEST)
    e = jnp.where(jnp.asarray(ok), e, MASK_VALUE)
    nk, nq = NA_K_ROWS * GRID_W, NA_Q_ROWS * GRID_W
    return pl.pallas_call(
        functools.partial(_na_bias_kernel, rows=rows),
        grid=(L, H),
        in_specs=[pl.BlockSpec((1, 1, 2 * NA_KR - 1, GRID_W, LANES), lambda l, h: (l, h, 0, 0, 0))],
        out_specs=pl.BlockSpec((1, 3, 1, nk, nq), lambda l, h: (l, 0, h, 0, 0)),
        out_shape=jax.ShapeDtypeStruct((L, 3, H, nk, nq), F32),
        compiler_params=_cparams(("parallel", "parallel")),
        name="na_bias",
    )(e)


def _na_call(qna, kna, vna_t, bias, layer, sw):
    B, T, _ = kna.shape
    rows = T // GRID_W
    nrb = rows // NA_Q_ROWS
    nq = NA_Q_ROWS * GRID_W
    nk = NA_K_ROWS * GRID_W

    def bias_map(b, rb):
        return (layer, jnp.where(rb == 0, 0, jnp.where(rb == nrb - 1, 2, 1)), 0, 0, 0)

    return pl.pallas_call(
        functools.partial(_na_kernel, rows=rows, sw=sw),
        grid=(B, nrb),
        in_specs=[
            pl.BlockSpec((1, NA_HEADS * HEAD_DIM, nq), lambda b, rb: (b, 0, rb)),
            pl.BlockSpec((1, T, NA_HEADS * HEAD_DIM), lambda b, rb: (b, 0, 0)),
            pl.BlockSpec((1, NA_HEADS * NA_VROWS, T), lambda b, rb: (b, 0, 0)),
            pl.BlockSpec((1, 1, NA_HEADS, nk, nq), bias_map),
        ],
        out_specs=pl.BlockSpec((1, NA_HEADS * HEAD_DIM, nq), lambda b, rb: (b, 0, rb)),
        out_shape=jax.ShapeDtypeStruct((B, NA_HEADS * HEAD_DIM, T), F32),
        scratch_shapes=[pltpu.VMEM((3, nk, sw), F32), pltpu.VMEM((3, 1, sw), F32)],
        compiler_params=_cparams(("parallel", "arbitrary")),
        name="na_attn",
    )(qna, kna, vna_t, bias)


def _mla_kernel(q_ref, k_ref, v_ref, o_ref, s_ref, mx_ref, *, tq, tk, sw, unroll):
    T = k_ref.shape[1]
    nq = T // tq
    nkv = T // tk
    strips = [slice(c * sw, (c + 1) * sw) for c in range(tq // sw)]
    ns = len(strips)
    hk = tk // 2

    def score(slot, qi, j, c):
        k = k_ref[0, pl.ds(pl.multiple_of(j * tk, tk), tk), :]
        q = q_ref[0, :, pl.ds(pl.multiple_of(qi * tq + c * sw, sw), sw)]
        s = jnp.dot(k, q, preferred_element_type=F32)
        s_ref[slot, :, strips[c]] = s
        mx_ref[slot, :, strips[c]] = jnp.max(s, axis=0, keepdims=True)

    def step(slot, j, nxt_qi, nxt_j, m, acc):
        v = v_ref[0, :, pl.ds(pl.multiple_of(j * tk, tk), tk)]
        ms, accs = [], []
        for c in range(min(2, ns)):
            score(1 - slot, nxt_qi, nxt_j, c)
        for c, sl in enumerate(strips):
            m_old = m[:, sl]
            m_new = jnp.maximum(m_old, mx_ref[slot, :, sl])
            p = jnp.exp2(s_ref[slot, :, sl] - m_new).astype(BF16)
            pv = jnp.dot(v[:, :hk], p[:hk], preferred_element_type=F32)
            if c + 2 < ns:
                score(1 - slot, nxt_qi, nxt_j, c + 2)
            pv = pv + jnp.dot(v[:, hk:], p[hk:], preferred_element_type=F32)
            accs.append(jnp.exp2(m_old - m_new) * acc[:, sl] + pv)
            ms.append(m_new)
        return jnp.concatenate(ms, axis=1), jnp.concatenate(accs, axis=1)

    def body(bi, carry):
        m, acc = carry
        t0 = bi * unroll
        qi = t0 // nkv
        j0 = t0 % nkv
        fresh = j0 == 0
        m = jnp.where(fresh, -jnp.inf, m)
        acc = jnp.where(fresh, 0.0, acc)
        for u in range(unroll):
            if u + 1 < unroll:
                nxt_qi, nxt_j = qi, j0 + u + 1
            else:
                nxt_qi = jnp.minimum(qi + (j0 + unroll) // nkv, nq - 1)
                nxt_j = (j0 + unroll) % nkv
            m, acc = step(u % 2, j0 + u, nxt_qi, nxt_j, m, acc)

        @pl.when(j0 + unroll == nkv)
        def _():
            o_ref[0, :, pl.ds(pl.multiple_of(qi * tq, tq), tq)] = acc[:MLA_V] / acc[MLA_V:MLA_V + 1]

        return m, acc

    for c in range(ns):
        score(0, 0, 0, c)
    init = (jnp.full((1, tq), -jnp.inf, F32), jnp.zeros((MLA_VROWS, tq), F32))
    lax.fori_loop(0, nq * nkv // unroll, body, init)


def _mla_call(qm, km, vm_t, tq, tk, sw, unroll):
    B, T, _ = km.shape
    assert unroll % 2 == 0 and (T // tk) % unroll == 0 and T % tq == 0 and tq % sw == 0
    return pl.pallas_call(
        functools.partial(_mla_kernel, tq=tq, tk=tk, sw=sw, unroll=unroll),
        grid=(B, MLA_HEADS),
        in_specs=[
            pl.BlockSpec((1, LANES, T), lambda b, h: (b, h, 0)),
            pl.BlockSpec((1, T, LANES), lambda b, h: (b, 0, h)),
            pl.BlockSpec((1, MLA_VROWS, T), lambda b, h: (b, h, 0)),
        ],
        out_specs=pl.BlockSpec((1, MLA_V, T), lambda b, h: (b, h, 0)),
        out_shape=jax.ShapeDtypeStruct((B, MLA_HEADS * MLA_V, T), F32),
        scratch_shapes=[pltpu.VMEM((2, tk, tq), F32), pltpu.VMEM((2, 1, tq), F32)],
        compiler_params=_cparams(("parallel", "parallel")),
        name="mla_attn",
    )(qm, km, vm_t)


def _group_rms_rows(x):
    c, tm = x.shape
    xg = x.reshape(c // HEAD_DIM, HEAD_DIM, tm)
    ms = jnp.mean(xg * xg, axis=1, keepdims=True)
    return (xg * lax.rsqrt(ms + EPS)).reshape(c, tm)


def _mix_ffn_kernel(x_ref, mod_ref, yna_ref, ym_ref, u_ref, up_ref, un_ref, gb_ref, cw_ref, cb_ref,
                    og_ref, wout_ref, g2_ref, wg_ref, wu_ref, wd_ref, o_ref, act_ref, *, chunk):
    i = pl.program_id(1)
    last = pl.num_programs(1) - 1
    u = u_ref[0]
    tm = u.shape[1]
    prev = jnp.where(i > 0, up_ref[0], 0.0)
    nxt = jnp.where(i < last, un_ref[0], 0.0)
    ext = jnp.concatenate([prev, u, nxt], axis=1)
    w = ext.shape[1]
    u_m1 = pltpu.roll(ext, 1, 1)[:, LANES:LANES + tm]
    u_p1 = pltpu.roll(ext, w - 1, 1)[:, LANES:LANES + tm]
    y = cw_ref[0] * u_m1 + cw_ref[1] * u + cw_ref[2] * u_p1 + cb_ref[...]
    yc = gb_ref[0] * y
    mixed = jnp.concatenate([_group_rms_rows(yna_ref[0]), _group_rms_rows(ym_ref[0]),
                             _group_rms_rows(yc)], axis=0)
    mixed = (mixed.T * og_ref[...]).astype(BF16)
    x = x_ref[0] + mod_ref[0, 2:3, :] * jnp.dot(mixed, wout_ref[...], preferred_element_type=F32)
    sh = mod_ref[0, 3:4, :]
    sc = mod_ref[0, 4:5, :]
    hb = (_row_rms(x, g2_ref[...]) * (1.0 + sc) + sh).astype(BF16)
    dff = wg_ref.shape[1]
    for c in range(dff // chunk):
        sl = slice(c * chunk, (c + 1) * chunk)
        g = jnp.dot(hb, wg_ref[:, sl], preferred_element_type=F32)
        up = jnp.dot(hb, wu_ref[:, sl], preferred_element_type=F32)
        act_ref[:, sl] = (g * jax.nn.sigmoid(g) * up).astype(BF16)
    out = jnp.dot(act_ref[...], wd_ref[...], preferred_element_type=F32)
    o_ref[0] = x + mod_ref[0, 5:6, :] * out


def _mix_ffn_call(x, mod, yna_t, ym_t, u_t, gb_t, p, tm):
    B, T, D = x.shape
    nb = tm // LANES
    nlb = T // LANES
    dff = p["w_g"].shape[1]
    chan = lambda c: pl.BlockSpec((1, c, tm), lambda b, i: (b, 0, i))
    consts = [p["conv_w"], p["conv_b"], p["out_g"], p["w_out"], p["g2"], p["w_g"], p["w_u"], p["w_d"]]
    return pl.pallas_call(
        functools.partial(_mix_ffn_kernel, chunk=256),
        grid=(B, T // tm),
        in_specs=[
            pl.BlockSpec((1, tm, D), lambda b, i: (b, i, 0)),
            pl.BlockSpec((1, 6, D), lambda b, i: (b, 0, 0)),
            chan(NA_HEADS * HEAD_DIM), chan(MLA_HEADS * MLA_V), chan(CONV_WIDTH),
            pl.BlockSpec((1, CONV_WIDTH, LANES), lambda b, i: (b, 0, jnp.maximum(i * nb - 1, 0))),
            pl.BlockSpec((1, CONV_WIDTH, LANES), lambda b, i: (b, 0, jnp.minimum((i + 1) * nb, nlb - 1))),
            chan(CONV_WIDTH),
        ] + [_const_spec(a.shape) for a in consts],
        out_specs=pl.BlockSpec((1, tm, D), lambda b, i: (b, i, 0)),
        out_shape=jax.ShapeDtypeStruct((B, T, D), F32),
        scratch_shapes=[pltpu.VMEM((tm, dff), BF16)],
        compiler_params=_cparams(("parallel", "parallel")),
        name="mix_ffn",
    )(x, mod, yna_t, ym_t, u_t, u_t, u_t, gb_t, *consts)


def _pad_heads(w, heads, width):
    k = w.shape[0]
    w = w.reshape(k, heads, width)
    return jnp.pad(w, ((0, 0), (0, 0), (0, LANES - width))).reshape(k, heads * LANES)


def _lane_row(parts, repeat=1):
    row = jnp.zeros((1, LANES), F32)
    for off, v in parts:
        row = row.at[0, off:off + v.shape[0]].set(v.astype(F32))
    return jnp.tile(row, (1, repeat))


def _norm_matrices():
    mla = np.zeros((LANES, LANES), np.float32)
    mla[:MLA_NOPE, :MLA_NOPE] = 1.0 / MLA_NOPE
    mla[MLA_NOPE:MLA_NOPE + MLA_ROPE, MLA_NOPE:MLA_NOPE + MLA_ROPE] = 1.0 / MLA_ROPE
    na = np.zeros((LANES, LANES), np.float32)
    na[:HEAD_DIM, :HEAD_DIM] = 1.0 / HEAD_DIM
    na[HEAD_DIM:, HEAD_DIM:] = 1.0 / HEAD_DIM
    z = np.zeros((LANES, LANES), np.float32)
    pair = lambda m: jnp.asarray(np.block([[m, z], [z, m]]), BF16)
    return pair(mla), pair(na)


def _layer_params(l, tm, norm1_g, norm2_g, w_in, na_q_g, na_k_g, mla_q_a_g, mla_kv_a_g, mla_w_uq,
                  mla_w_ukv, mla_qn_g, mla_kn_g, mla_qr_g, mla_kr_g, conv_w, conv_b, out_norm_g,
                  w_out, w_gu, w_down):
    naw = NA_HEADS * HEAD_DIM
    i0 = 3 * naw
    i1 = i0 + MLA_Q_RANK
    i2 = i1 + MLA_KV_RANK
    i3 = i2 + MLA_ROPE
    w = w_in[l]
    d = w.shape[0]
    kr_slab = jnp.pad(w[:, i2:i3], ((0, 0), (MLA_NOPE, LANES - MLA_NOPE - MLA_ROPE)))
    w_in_r = jnp.concatenate([w[:, :i2], kr_slab, w[:, i3:]], axis=1).astype(BF16)
    ukv = mla_w_ukv[l].reshape(MLA_KV_RANK, MLA_HEADS, MLA_NOPE + MLA_V)
    w_uk = _pad_heads(ukv[:, :, :MLA_NOPE].reshape(MLA_KV_RANK, -1), MLA_HEADS, MLA_NOPE).astype(BF16)
    w_uv = ukv[:, :, MLA_NOPE:].reshape(MLA_KV_RANK, -1).astype(BF16)
    w_uq = _pad_heads(mla_w_uq[l], MLA_HEADS, MLA_NOPE + MLA_ROPE).astype(BF16)
    na_scale = HEAD_DIM ** -0.5 * float(np.log2(np.e))
    mla_scale = (MLA_NOPE + MLA_ROPE) ** -0.5 * float(np.log2(np.e))
    dff = w_down.shape[1]
    m_mla, m_na = _norm_matrices()
    return {
        "g1": norm1_g[l].reshape(1, d), "g2": norm2_g[l].reshape(1, d),
        "w_in": w_in_r, "w_uq": w_uq, "w_uk": w_uk, "w_uv": w_uv, "m2": m_mla, "m_na": m_na,
        "g_qna": _lane_row([(0, na_q_g[l] * na_scale), (HEAD_DIM, na_q_g[l] * na_scale)], 2),
        "g_kna": _lane_row([(0, na_k_g[l]), (HEAD_DIM, na_k_g[l])], 2),
        "g_qa": mla_q_a_g[l].reshape(1, -1), "g_kva": mla_kv_a_g[l].reshape(1, -1),
        "g_q": _lane_row([(0, mla_qn_g[l] * mla_scale), (MLA_NOPE, mla_qr_g[l] * mla_scale)], 2),
        "g_k": _lane_row([(0, mla_kn_g[l])], 2),
        "g_kr": _lane_row([(MLA_NOPE, mla_kr_g[l])]),
        "conv_w": jnp.broadcast_to(conv_w[l][:, :, None], (3, CONV_WIDTH, tm)),
        "conv_b": jnp.broadcast_to(conv_b[l][:, None], (CONV_WIDTH, tm)),
        "out_g": out_norm_g[l].reshape(1, -1),
        "w_out": w_out[l].astype(BF16),
        "w_g": w_gu[l][:, :dff].astype(BF16),
        "w_u": w_gu[l][:, dff:].astype(BF16),
        "w_d": w_down[l].astype(BF16),
    }


def kernel(x, c, positions, norm1_g, norm2_g, w_ada, b_ada, w_in, na_q_g, na_k_g, na_rpb, mla_q_a_g,
           mla_kv_a_g, mla_w_uq, mla_w_ukv, mla_qn_g, mla_kn_g, mla_qr_g, mla_kr_g, conv_w, conv_b,
           out_norm_g, w_out, w_gu, w_down):
    B, T, D = x.shape
    depth = w_in.shape[0]
    rows = T // GRID_W
    t = _tiles(T)
    mod = _ada_modulation(c, w_ada, b_ada)
    cos_t, sin_t = _rope_tables(positions)
    na_bias = _na_bias_tables(na_rpb, rows)
    for l in range(depth):
        p = _layer_params(l, t.mix, norm1_g, norm2_g, w_in, na_q_g, na_k_g, mla_q_a_g, mla_kv_a_g,
                          mla_w_uq, mla_w_ukv, mla_qn_g, mla_kn_g, mla_qr_g, mla_kr_g, conv_w,
                          conv_b, out_norm_g, w_out, w_gu, w_down)
        qna, kna, vna_t, qm, km, vm_t, u_t, gb_t = _proj_call(x, mod[l], cos_t, sin_t, p, t.proj, t.proj_sub)
        yna_t = _na_call(qna, kna, vna_t, na_bias, l, sw=t.strip)
        ym_t = _mla_call(qm, km, vm_t, tq=t.mla_q, tk=t.mla_k, sw=t.strip, unroll=t.mla_unroll)
        x = _mix_ffn_call(x, mod[l], yna_t, ym_t, u_t, gb_t, p, t.mix)
    return x
```

```python
import functools
from typing import NamedTuple

import jax
import jax.numpy as jnp
import numpy as np
from jax import lax
from jax.experimental import pallas as pl
from jax.experimental.pallas import tpu as pltpu

F32 = jnp.float32
BF16 = jnp.bfloat16

GRID_W = 64
HEAD_DIM = 64
NA_HEADS = 4
NA_KR = 8
NA_KC = 16
MLA_HEADS = 8
MLA_NOPE = 64
MLA_ROPE = 32
MLA_V = 64
MLA_Q_RANK = 384
MLA_KV_RANK = 256
CONV_WIDTH = 256
ROPE_THETA = 10000.0
EPS = 1e-6

LANES = 128
NA_Q_ROWS = 8
NA_SUB_ROWS = 4
ONES_ROWS = 16
MLA_VROWS = MLA_V + ONES_ROWS
NA_VROWS = HEAD_DIM + ONES_ROWS
NA_K_ROWS = 12
MASK_VALUE = -1e30
VMEM_LIMIT = 56 * 1024 * 1024


class _Tiles(NamedTuple):
    proj: int
    proj_sub: int
    mix: int
    mla_q: int
    mla_k: int
    strip: int
    mla_unroll: int


def _tiles(T):
    mla_k = 512
    return _Tiles(proj=min(1024, T), proj_sub=256, mix=512, mla_q=1024, mla_k=mla_k, strip=256,
                  mla_unroll=min(16, T // mla_k))


def _cparams(sem):
    return pltpu.CompilerParams(dimension_semantics=sem, vmem_limit_bytes=VMEM_LIMIT)


def _const_spec(shape):
    nd = len(shape)
    return pl.BlockSpec(shape, lambda *_: (0,) * nd, pipeline_mode=pl.Buffered(1))


def _split_bf16(x):
    hi = x.astype(BF16)
    lo = (x - hi.astype(F32)).astype(BF16)
    return hi, lo


def _ada_kernel(c_ref, w_ref, b_ref, o_ref):
    c = c_ref[...]
    a = c * jax.nn.sigmoid(c)
    a_hi, a_lo = _split_bf16(a)
    w_hi, w_lo = _split_bf16(w_ref[0])
    acc = jnp.dot(a_hi, w_hi, preferred_element_type=F32)
    acc += jnp.dot(a_lo, w_hi, preferred_element_type=F32)
    acc += jnp.dot(a_hi, w_lo, preferred_element_type=F32)
    o_ref[0] = acc + b_ref[0]


def _ada_modulation(c, w_ada, b_ada):
    L, D, N = w_ada.shape
    B = c.shape[0]
    rows = 8
    c_pad = jnp.zeros((rows, D), F32).at[:B].set(c)
    tn = 1536
    out = pl.pallas_call(
        _ada_kernel,
        grid=(L, N // tn),
        in_specs=[
            pl.BlockSpec((rows, D), lambda l, j: (0, 0)),
            pl.BlockSpec((1, D, tn), lambda l, j: (l, 0, j)),
            pl.BlockSpec((1, 1, tn), lambda l, j: (l, 0, j)),
        ],
        out_specs=pl.BlockSpec((1, rows, tn), lambda l, j: (l, 0, j)),
        out_shape=jax.ShapeDtypeStruct((L, rows, N), F32),
        compiler_params=_cparams(("parallel", "parallel")),
        name="ada_mod",
    )(c_pad, w_ada, b_ada.reshape(L, 1, N))
    return out[:, :B].reshape(L, B, 6, D)


def _rope_kernel(pos_ref, inv_ref, cos_ref, sin_ref):
    ang = pos_ref[0].astype(F32) * inv_ref[...]
    c = jnp.cos(ang)
    s = jnp.sin(ang)
    tm = ang.shape[1]
    pad = LANES - MLA_NOPE - MLA_ROPE
    cos_t = jnp.concatenate([jnp.ones((MLA_NOPE, tm), F32), c, c, jnp.ones((pad, tm), F32)], axis=0)
    sin_t = jnp.concatenate([jnp.zeros((MLA_NOPE, tm), F32), -s, s, jnp.zeros((pad, tm), F32)], axis=0)
    cos_ref[0] = cos_t.T
    sin_ref[0] = sin_t.T


def _rope_tables(positions):
    B, T = positions.shape
    half = MLA_ROPE // 2
    inv = ROPE_THETA ** (-jnp.arange(0, MLA_ROPE, 2, dtype=F32) / MLA_ROPE)
    tm = min(T, 2048)
    spec = pl.BlockSpec((1, tm, LANES), lambda b, i: (b, i, 0))
    return pl.pallas_call(
        _rope_kernel,
        grid=(B, T // tm),
        in_specs=[
            pl.BlockSpec((1, 1, tm), lambda b, i: (b, 0, i)),
            _const_spec((half, tm)),
        ],
        out_specs=[spec, spec],
        out_shape=[jax.ShapeDtypeStruct((B, T, LANES), F32)] * 2,
        compiler_params=_cparams(("parallel", "parallel")),
        name="rope_tables",
    )(positions.reshape(B, 1, T), jnp.broadcast_to(inv[:, None], (half, tm)))


def _pair_ms(x, m_ref):
    return jnp.dot((x * x).astype(BF16), m_ref[...], preferred_element_type=F32)


def _slab_rope(xs, cos, sin, first_half):
    half = MLA_ROPE // 2
    w = xs.shape[1]
    partner = jnp.where(first_half,
                        pltpu.roll(xs, w - half, 1),
                        pltpu.roll(xs, half, 1))
    return xs * cos + partner * sin


def _store_values(v_ref, tok, vt, heads, width):
    ones = jnp.ones((ONES_ROWS, vt.shape[1]), BF16)
    rows = width + ONES_ROWS
    for hd in range(heads):
        v_ref[0, hd * rows:hd * rows + width, tok] = vt[hd * width:(hd + 1) * width]
        v_ref[0, hd * rows + width:(hd + 1) * rows, tok] = ones


def _row_rms(x, gain):
    ms = jnp.mean(x * x, axis=-1, keepdims=True)
    return x * lax.rsqrt(ms + EPS) * gain


def _proj_kernel(x_ref, mod_ref, g1_ref, win_ref, wuq_ref, wuk_ref, wuv_ref, m2_ref, mna_ref,
                 gqna_ref, gkna_ref, gqa_ref, gkva_ref, gq_ref, gk_ref, gkr_ref, cos_ref, sin_ref,
                 qna_ref, kna_ref, vna_ref, qm_ref, km_ref, vm_ref, u_ref, gb_ref, *, sub):
    for t0 in range(0, x_ref.shape[1], sub):
        _proj_sub_tile(slice(t0, t0 + sub), x_ref, mod_ref, g1_ref, win_ref, wuq_ref, wuk_ref, wuv_ref,
                       m2_ref, mna_ref, gqna_ref, gkna_ref, gqa_ref, gkva_ref, gq_ref, gk_ref, gkr_ref,
                       cos_ref, sin_ref, qna_ref, kna_ref, vna_ref, qm_ref, km_ref, vm_ref, u_ref, gb_ref)


def _proj_sub_tile(tok, x_ref, mod_ref, g1_ref, win_ref, wuq_ref, wuk_ref, wuv_ref, m2_ref, mna_ref,
                   gqna_ref, gkna_ref, gqa_ref, gkva_ref, gq_ref, gk_ref, gkr_ref, cos_ref, sin_ref,
                   qna_ref, kna_ref, vna_ref, qm_ref, km_ref, vm_ref, u_ref, gb_ref):
    x = x_ref[0, tok, :]
    sh = mod_ref[0, 0:1, :]
    sc = mod_ref[0, 1:2, :]
    h = _row_rms(x, g1_ref[...]) * (1.0 + sc) + sh
    hb = h.astype(BF16)

    nq = NA_HEADS * HEAD_DIM
    pair = 2 * LANES
    cw = CONV_WIDTH
    pairs = range(0, MLA_HEADS * LANES, pair)

    pall = jnp.dot(hb, win_ref[...], preferred_element_type=F32)
    o = 0
    pq = pall[:, o:o + nq]; o += nq
    pk = pall[:, o:o + nq]; o += nq
    pv = pall[:, o:o + nq]; o += nq
    cq = pall[:, o:o + MLA_Q_RANK]; o += MLA_Q_RANK
    ckv = pall[:, o:o + MLA_KV_RANK]; o += MLA_KV_RANK
    kr = pall[:, o:o + LANES]; o += LANES
    pc = pall[:, o:o + 3 * cw]

    cq = _row_rms(cq, gqa_ref[...]).astype(BF16)
    ckv = _row_rms(ckv, gkva_ref[...]).astype(BF16)
    q = jnp.dot(cq, wuq_ref[...], preferred_element_type=F32)
    kn = jnp.dot(ckv, wuk_ref[...], preferred_element_type=F32)
    vt = jnp.dot(ckv, wuv_ref[...], preferred_element_type=F32)

    ms_qna = _pair_ms(pq, mna_ref)
    ms_kna = _pair_ms(pk, mna_ref)
    ms_q = [_pair_ms(q[:, s0:s0 + pair], m2_ref) for s0 in pairs]
    ms_k = [_pair_ms(kn[:, s0:s0 + pair], m2_ref) for s0 in pairs]

    qna_ref[0, :, tok] = (pq * lax.rsqrt(ms_qna + EPS) * gqna_ref[...]).T.astype(BF16)
    kna_ref[0, tok, :] = (pk * lax.rsqrt(ms_kna + EPS) * gkna_ref[...]).astype(BF16)
    _store_values(vna_ref, tok, pv.T.astype(BF16), NA_HEADS, HEAD_DIM)

    cos = cos_ref[0, tok, :]
    sin = sin_ref[0, tok, :]
    cos2 = jnp.concatenate([cos, cos], axis=1)
    sin2 = jnp.concatenate([sin, sin], axis=1)
    lane = lax.broadcasted_iota(jnp.int32, (1, pair), 1) % LANES
    first_half = lane < MLA_NOPE + MLA_ROPE // 2

    for s0, ms in zip(pairs, ms_q):
        qs = q[:, s0:s0 + pair] * lax.rsqrt(ms + EPS) * gq_ref[...]
        qm_ref[0, s0:s0 + pair, tok] = _slab_rope(qs, cos2, sin2, first_half).T.astype(BF16)

    kr = kr * lax.rsqrt(jnp.sum(kr * kr, axis=-1, keepdims=True) * (1.0 / MLA_ROPE) + EPS) * gkr_ref[...]
    kr = _slab_rope(kr, cos, sin, first_half[:, :LANES])
    kr2 = jnp.concatenate([kr, kr], axis=1)
    for s0, ms in zip(pairs, ms_k):
        kn_s = kn[:, s0:s0 + pair] * lax.rsqrt(ms + EPS) * gk_ref[...]
        km_ref[0, tok, s0:s0 + pair] = (kn_s + kr2).astype(BF16)
    _store_values(vm_ref, tok, vt.T.astype(BF16), MLA_HEADS, MLA_V)

    u_ref[0, :, tok] = (pc[:, 2 * cw:3 * cw] * pc[:, 0:cw]).T
    gb_ref[0, :, tok] = pc[:, cw:2 * cw].T


def _proj_call(x, mod, cos_t, sin_t, p, tm, sub):
    B, T, D = x.shape
    nq = NA_HEADS * HEAD_DIM
    nm = MLA_HEADS * LANES
    tok = lambda w: pl.BlockSpec((1, tm, w), lambda b, i: (b, i, 0))
    chan = lambda c: pl.BlockSpec((1, c, tm), lambda b, i: (b, 0, i))
    consts = [p["g1"], p["w_in"], p["w_uq"], p["w_uk"], p["w_uv"], p["m2"], p["m_na"], p["g_qna"],
              p["g_kna"], p["g_qa"], p["g_kva"], p["g_q"], p["g_k"], p["g_kr"]]
    in_specs = ([tok(D), pl.BlockSpec((1, 6, D), lambda b, i: (b, 0, 0))]
                + [_const_spec(a.shape) for a in consts] + [tok(LANES), tok(LANES)])
    out_shape = [
        jax.ShapeDtypeStruct((B, nq, T), BF16), jax.ShapeDtypeStruct((B, T, nq), BF16),
        jax.ShapeDtypeStruct((B, NA_HEADS * NA_VROWS, T), BF16),
        jax.ShapeDtypeStruct((B, nm, T), BF16), jax.ShapeDtypeStruct((B, T, nm), BF16),
        jax.ShapeDtypeStruct((B, MLA_HEADS * MLA_VROWS, T), BF16),
        jax.ShapeDtypeStruct((B, CONV_WIDTH, T), F32), jax.ShapeDtypeStruct((B, CONV_WIDTH, T), F32),
    ]
    out_specs = [chan(nq), tok(nq), chan(NA_HEADS * NA_VROWS), chan(nm), tok(nm),
                 chan(MLA_HEADS * MLA_VROWS), chan(CONV_WIDTH), chan(CONV_WIDTH)]
    return pl.pallas_call(
        functools.partial(_proj_kernel, sub=sub),
        grid=(B, T // tm),
        in_specs=in_specs,
        out_specs=out_specs,
        out_shape=out_shape,
        compiler_params=_cparams(("parallel", "parallel")),
        name="in_proj",
    )(x, mod, *consts, cos_t, sin_t)


def _na_variant(sb, nsb):
    return jnp.where(sb == 0, 0, jnp.where(sb == nsb - 1, 2, 1))


def _na_kernel(q_ref, k_ref, v_ref, *refs, rows):
    nsub = NA_Q_ROWS // NA_SUB_ROWS
    bias_refs, (o_ref, s_ref, mx_ref) = refs[:nsub], refs[nsub:]
    rb = pl.program_id(1)
    sw = NA_SUB_ROWS * GRID_W
    nk = NA_K_ROWS * GRID_W
    kh = nk // 2
    kstarts = [pl.multiple_of(jnp.clip(rb * NA_Q_ROWS + c * NA_SUB_ROWS - NA_KR // 2, 0, rows - NA_K_ROWS)
                              * GRID_W, 2 * LANES) for c in range(nsub)]
    units = [(h, c) for h in range(NA_HEADS) for c in range(nsub)]
    chan = lax.broadcasted_iota(jnp.int32, (LANES, 1), 0)

    def score(i):
        h, c = units[i]
        slab = slice((h // 2) * LANES, (h // 2 + 1) * LANES)
        q = q_ref[0, slab, c * sw:(c + 1) * sw]
        q = jnp.where(chan >= HEAD_DIM if h % 2 else chan < HEAD_DIM, q, jnp.zeros_like(q))
        mx = None
        for r0 in range(0, nk, kh):
            k = k_ref[0, pl.ds(pl.multiple_of(kstarts[c] + r0, LANES), kh), slab]
            s = jnp.dot(k, q, preferred_element_type=F32) + bias_refs[c][0, 0, h, r0:r0 + kh, :]
            s_ref[i % 3, r0:r0 + kh, :] = s
            part = jnp.max(s, axis=0, keepdims=True)
            mx = part if mx is None else jnp.maximum(mx, part)
        mx_ref[i % 3] = mx

    score(0)
    score(1)
    for i, (h, c) in enumerate(units):
        p = jnp.exp2(s_ref[i % 3] - mx_ref[i % 3]).astype(BF16)
        v = v_ref[0, h * NA_VROWS:(h + 1) * NA_VROWS, pl.ds(kstarts[c], nk)]
        acc = jnp.dot(v, p, preferred_element_type=F32)
        o_ref[0, h * HEAD_DIM:(h + 1) * HEAD_DIM, c * sw:(c + 1) * sw] = (
            acc[:HEAD_DIM] / acc[HEAD_DIM:HEAD_DIM + 1])
        if i + 2 < len(units):
            score(i + 2)


def _na_bias_kernel(e_ref, o_ref, *, rows):
    lane = lax.broadcasted_iota(jnp.int32, (GRID_W, LANES), 1)
    masked = jnp.full((GRID_W, LANES), MASK_VALUE, F32)
    variants = ((0, 0), (NA_SUB_ROWS, NA_SUB_ROWS - NA_KR // 2), (rows - NA_SUB_ROWS, rows - NA_K_ROWS))
    for v, (r0, ks) in enumerate(variants):
        for kr in range(NA_K_ROWS):
            for pair in range(NA_SUB_ROWS // 2):
                blocks = []
                for qr in (2 * pair, 2 * pair + 1):
                    r, k = r0 + qr, ks + kr
                    row_start = min(max(r - NA_KR // 2, 0), rows - NA_KR)
                    ok = row_start <= k < row_start + NA_KR
                    blocks.append(e_ref[0, 0, k - r + NA_KR - 1] if ok else masked)
                o_ref[0, v, 0, kr * GRID_W:(kr + 1) * GRID_W, pair * LANES:(pair + 1) * LANES] = (
                    jnp.where(lane < GRID_W, blocks[0], blocks[1]))


def _na_bias_tables(rpb, rows):
    L, H = rpb.shape[:2]
    cols = np.arange(GRID_W)
    col_start = np.clip(cols - NA_KC // 2, 0, GRID_W - NA_KC)
    col_ok = (cols[None, :] >= col_start[:, None]) & (cols[None, :] < col_start[:, None] + NA_KC)
    dc = np.clip(cols[None, :] - cols[:, None] + NA_KC - 1, 0, 2 * NA_KC - 2)
    sel = np.eye(2 * NA_KC - 1, dtype=np.float32)[dc.T]
    sel = np.concatenate([sel, sel], axis=1)
    ok = np.concatenate([col_ok.T, col_ok.T], axis=1)
    e = jnp.einsum("lhij,wqj->lhiwq", rpb.astype(F32) * float(np.log2(np.e)), jnp.asarray(sel),
                   precision=lax.Precision.HIGHEST)
    e = jnp.where(jnp.asarray(ok), e, MASK_VALUE)
    nk, sw = NA_K_ROWS * GRID_W, NA_SUB_ROWS * GRID_W
    return pl.pallas_call(
        functools.partial(_na_bias_kernel, rows=rows),
        grid=(L, H),
        in_specs=[pl.BlockSpec((1, 1, 2 * NA_KR - 1, GRID_W, LANES), lambda l, h: (l, h, 0, 0, 0))],
        out_specs=pl.BlockSpec((1, 3, 1, nk, sw), lambda l, h: (l, 0, h, 0, 0)),
        out_shape=jax.ShapeDtypeStruct((L, 3, H, nk, sw), F32),
        compiler_params=_cparams(("parallel", "parallel")),
        name="na_bias",
    )(e)


def _na_call(qna, kna, vna_t, bias, layer):
    B, T, _ = kna.shape
    rows = T // GRID_W
    nrb = rows // NA_Q_ROWS
    nsub = NA_Q_ROWS // NA_SUB_ROWS
    nq = NA_Q_ROWS * GRID_W
    sw = NA_SUB_ROWS * GRID_W
    nk = NA_K_ROWS * GRID_W
    bias_specs = [
        pl.BlockSpec((1, 1, NA_HEADS, nk, sw),
                     lambda b, rb, c=c: (layer, _na_variant(rb * nsub + c, nrb * nsub), 0, 0, 0))
        for c in range(nsub)]
    return pl.pallas_call(
        functools.partial(_na_kernel, rows=rows),
        grid=(B, nrb),
        in_specs=[
            pl.BlockSpec((1, NA_HEADS * HEAD_DIM, nq), lambda b, rb: (b, 0, rb)),
            pl.BlockSpec((1, T, NA_HEADS * HEAD_DIM), lambda b, rb: (b, 0, 0)),
            pl.BlockSpec((1, NA_HEADS * NA_VROWS, T), lambda b, rb: (b, 0, 0)),
        ] + bias_specs,
        out_specs=pl.BlockSpec((1, NA_HEADS * HEAD_DIM, nq), lambda b, rb: (b, 0, rb)),
        out_shape=jax.ShapeDtypeStruct((B, NA_HEADS * HEAD_DIM, T), F32),
        scratch_shapes=[pltpu.VMEM((3, nk, sw), F32), pltpu.VMEM((3, 1, sw), F32)],
        compiler_params=_cparams(("parallel", "arbitrary")),
        name="na_attn",
    )(qna, kna, vna_t, *([bias] * nsub))


def _mla_kernel(q_ref, k_ref, v_ref, o_ref, s_ref, mx_ref, *, tq, tk, sw, unroll):
    T = k_ref.shape[1]
    nq = T // tq
    nkv = T // tk
    strips = [slice(c * sw, (c + 1) * sw) for c in range(tq // sw)]
    ns = len(strips)
    hk = tk // 2

    def score(slot, qi, j, c):
        k = k_ref[0, pl.ds(pl.multiple_of(j * tk, tk), tk), :]
        q = q_ref[0, :, pl.ds(pl.multiple_of(qi * tq + c * sw, sw), sw)]
        s = jnp.dot(k, q, preferred_element_type=F32)
        s_ref[slot, :, strips[c]] = s
        mx_ref[slot, :, strips[c]] = jnp.max(s, axis=0, keepdims=True)

    def step(slot, j, nxt_qi, nxt_j, m, acc):
        v = v_ref[0, :, pl.ds(pl.multiple_of(j * tk, tk), tk)]
        ms, accs = [], []
        for c in range(min(2, ns)):
            score(1 - slot, nxt_qi, nxt_j, c)
        for c, sl in enumerate(strips):
            m_old = m[:, sl]
            m_new = jnp.maximum(m_old, mx_ref[slot, :, sl])
            p = jnp.exp2(s_ref[slot, :, sl] - m_new).astype(BF16)
            pv = jnp.dot(v[:, :hk], p[:hk], preferred_element_type=F32)
            if c + 2 < ns:
                score(1 - slot, nxt_qi, nxt_j, c + 2)
            pv = pv + jnp.dot(v[:, hk:], p[hk:], preferred_element_type=F32)
            accs.append(jnp.exp2(m_old - m_new) * acc[:, sl] + pv)
            ms.append(m_new)
        return jnp.concatenate(ms, axis=1), jnp.concatenate(accs, axis=1)

    def body(bi, carry):
        m, acc = carry
        t0 = bi * unroll
        qi = t0 // nkv
        j0 = t0 % nkv
        fresh = j0 == 0
        m = jnp.where(fresh, -jnp.inf, m)
        acc = jnp.where(fresh, 0.0, acc)
        for u in range(unroll):
            if u + 1 < unroll:
                nxt_qi, nxt_j = qi, j0 + u + 1
            else:
                nxt_qi = jnp.minimum(qi + (j0 + unroll) // nkv, nq - 1)
                nxt_j = (j0 + unroll) % nkv
            m, acc = step(u % 2, j0 + u, nxt_qi, nxt_j, m, acc)

        @pl.when(j0 + unroll == nkv)
        def _():
            o_ref[0, :, pl.ds(pl.multiple_of(qi * tq, tq), tq)] = acc[:MLA_V] / acc[MLA_V:MLA_V + 1]

        return m, acc

    for c in range(ns):
        score(0, 0, 0, c)
    init = (jnp.full((1, tq), -jnp.inf, F32), jnp.zeros((MLA_VROWS, tq), F32))
    lax.fori_loop(0, nq * nkv // unroll, body, init)


def _mla_call(qm, km, vm_t, tq, tk, sw, unroll):
    B, T, _ = km.shape
    assert unroll % 2 == 0 and (T // tk) % unroll == 0 and T % tq == 0 and tq % sw == 0
    return pl.pallas_call(
        functools.partial(_mla_kernel, tq=tq, tk=tk, sw=sw, unroll=unroll),
        grid=(B, MLA_HEADS),
        in_specs=[
            pl.BlockSpec((1, LANES, T), lambda b, h: (b, h, 0)),
            pl.BlockSpec((1, T, LANES), lambda b, h: (b, 0, h)),
            pl.BlockSpec((1, MLA_VROWS, T), lambda b, h: (b, h, 0)),
        ],
        out_specs=pl.BlockSpec((1, MLA_V, T), lambda b, h: (b, h, 0)),
        out_shape=jax.ShapeDtypeStruct((B, MLA_HEADS * MLA_V, T), F32),
        scratch_shapes=[pltpu.VMEM((2, tk, tq), F32), pltpu.VMEM((2, 1, tq), F32)],
        compiler_params=_cparams(("parallel", "parallel")),
        name="mla_attn",
    )(qm, km, vm_t)


def _group_rms_rows(x):
    c, tm = x.shape
    xg = x.reshape(c // HEAD_DIM, HEAD_DIM, tm)
    ms = jnp.mean(xg * xg, axis=1, keepdims=True)
    return (xg * lax.rsqrt(ms + EPS)).reshape(c, tm)


def _mix_ffn_kernel(x_ref, mod_ref, yna_ref, ym_ref, u_ref, up_ref, un_ref, gb_ref, cw_ref, cb_ref,
                    og_ref, wout_ref, g2_ref, wg_ref, wu_ref, wd_ref, o_ref, act_ref, *, chunk):
    i = pl.program_id(1)
    last = pl.num_programs(1) - 1
    u = u_ref[0]
    tm = u.shape[1]
    prev = jnp.where(i > 0, up_ref[0], 0.0)
    nxt = jnp.where(i < last, un_ref[0], 0.0)
    ext = jnp.concatenate([prev, u, nxt], axis=1)
    w = ext.shape[1]
    u_m1 = pltpu.roll(ext, 1, 1)[:, LANES:LANES + tm]
    u_p1 = pltpu.roll(ext, w - 1, 1)[:, LANES:LANES + tm]
    y = cw_ref[0] * u_m1 + cw_ref[1] * u + cw_ref[2] * u_p1 + cb_ref[...]
    yc = gb_ref[0] * y
    mixed = jnp.concatenate([_group_rms_rows(yna_ref[0]), _group_rms_rows(ym_ref[0]),
                             _group_rms_rows(yc)], axis=0)
    mixed = (mixed.T * og_ref[...]).astype(BF16)
    x = x_ref[0] + mod_ref[0, 2:3, :] * jnp.dot(mixed, wout_ref[...], preferred_element_type=F32)
    sh = mod_ref[0, 3:4, :]
    sc = mod_ref[0, 4:5, :]
    hb = (_row_rms(x, g2_ref[...]) * (1.0 + sc) + sh).astype(BF16)
    dff = wg_ref.shape[1]
    for c in range(dff // chunk):
        sl = slice(c * chunk, (c + 1) * chunk)
        g = jnp.dot(hb, wg_ref[:, sl], preferred_element_type=F32)
        up = jnp.dot(hb, wu_ref[:, sl], preferred_element_type=F32)
        act_ref[:, sl] = (g * jax.nn.sigmoid(g) * up).astype(BF16)
    out = jnp.dot(act_ref[...], wd_ref[...], preferred_element_type=F32)
    o_ref[0] = x + mod_ref[0, 5:6, :] * out


def _mix_ffn_call(x, mod, yna_t, ym_t, u_t, gb_t, p, tm):
    B, T, D = x.shape
    nb = tm // LANES
    nlb = T // LANES
    dff = p["w_g"].shape[1]
    chan = lambda c: pl.BlockSpec((1, c, tm), lambda b, i: (b, 0, i))
    consts = [p["conv_w"], p["conv_b"], p["out_g"], p["w_out"], p["g2"], p["w_g"], p["w_u"], p["w_d"]]
    return pl.pallas_call(
        functools.partial(_mix_ffn_kernel, chunk=256),
        grid=(B, T // tm),
        in_specs=[
            pl.BlockSpec((1, tm, D), lambda b, i: (b, i, 0)),
            pl.BlockSpec((1, 6, D), lambda b, i: (b, 0, 0)),
            chan(NA_HEADS * HEAD_DIM), chan(MLA_HEADS * MLA_V), chan(CONV_WIDTH),
            pl.BlockSpec((1, CONV_WIDTH, LANES), lambda b, i: (b, 0, jnp.maximum(i * nb - 1, 0))),
            pl.BlockSpec((1, CONV_WIDTH, LANES), lambda b, i: (b, 0, jnp.minimum((i + 1) * nb, nlb - 1))),
            chan(CONV_WIDTH),
        ] + [_const_spec(a.shape) for a in consts],
        out_specs=pl.BlockSpec((1, tm, D), lambda b, i: (b, i, 0)),
        out_shape=jax.ShapeDtypeStruct((B, T, D), F32),
        scratch_shapes=[pltpu.VMEM((tm, dff), BF16)],
        compiler_params=_cparams(("parallel", "parallel")),
        name="mix_ffn",
    )(x, mod, yna_t, ym_t, u_t, u_t, u_t, gb_t, *consts)


def _pad_heads(w, heads, width):
    k = w.shape[0]
    w = w.reshape(k, heads, width)
    return jnp.pad(w, ((0, 0), (0, 0), (0, LANES - width))).reshape(k, heads * LANES)


def _lane_row(parts, repeat=1):
    row = jnp.zeros((1, LANES), F32)
    for off, v in parts:
        row = row.at[0, off:off + v.shape[0]].set(v.astype(F32))
    return jnp.tile(row, (1, repeat))


def _norm_matrices():
    mla = np.zeros((LANES, LANES), np.float32)
    mla[:MLA_NOPE, :MLA_NOPE] = 1.0 / MLA_NOPE
    mla[MLA_NOPE:MLA_NOPE + MLA_ROPE, MLA_NOPE:MLA_NOPE + MLA_ROPE] = 1.0 / MLA_ROPE
    na = np.zeros((LANES, LANES), np.float32)
    na[:HEAD_DIM, :HEAD_DIM] = 1.0 / HEAD_DIM
    na[HEAD_DIM:, HEAD_DIM:] = 1.0 / HEAD_DIM
    z = np.zeros((LANES, LANES), np.float32)
    pair = lambda m: jnp.asarray(np.block([[m, z], [z, m]]), BF16)
    return pair(mla), pair(na)


def _layer_params(l, tm, norm1_g, norm2_g, w_in, na_q_g, na_k_g, mla_q_a_g, mla_kv_a_g, mla_w_uq,
                  mla_w_ukv, mla_qn_g, mla_kn_g, mla_qr_g, mla_kr_g, conv_w, conv_b, out_norm_g,
                  w_out, w_gu, w_down):
    naw = NA_HEADS * HEAD_DIM
    i0 = 3 * naw
    i1 = i0 + MLA_Q_RANK
    i2 = i1 + MLA_KV_RANK
    i3 = i2 + MLA_ROPE
    w = w_in[l]
    d = w.shape[0]
    kr_slab = jnp.pad(w[:, i2:i3], ((0, 0), (MLA_NOPE, LANES - MLA_NOPE - MLA_ROPE)))
    w_in_r = jnp.concatenate([w[:, :i2], kr_slab, w[:, i3:]], axis=1).astype(BF16)
    ukv = mla_w_ukv[l].reshape(MLA_KV_RANK, MLA_HEADS, MLA_NOPE + MLA_V)
    w_uk = _pad_heads(ukv[:, :, :MLA_NOPE].reshape(MLA_KV_RANK, -1), MLA_HEADS, MLA_NOPE).astype(BF16)
    w_uv = ukv[:, :, MLA_NOPE:].reshape(MLA_KV_RANK, -1).astype(BF16)
    w_uq = _pad_heads(mla_w_uq[l], MLA_HEADS, MLA_NOPE + MLA_ROPE).astype(BF16)
    na_scale = HEAD_DIM ** -0.5 * float(np.log2(np.e))
    mla_scale = (MLA_NOPE + MLA_ROPE) ** -0.5 * float(np.log2(np.e))
    dff = w_down.shape[1]
    m_mla, m_na = _norm_matrices()
    return {
        "g1": norm1_g[l].reshape(1, d), "g2": norm2_g[l].reshape(1, d),
        "w_in": w_in_r, "w_uq": w_uq, "w_uk": w_uk, "w_uv": w_uv, "m2": m_mla, "m_na": m_na,
        "g_qna": _lane_row([(0, na_q_g[l] * na_scale), (HEAD_DIM, na_q_g[l] * na_scale)], 2),
        "g_kna": _lane_row([(0, na_k_g[l]), (HEAD_DIM, na_k_g[l])], 2),
        "g_qa": mla_q_a_g[l].reshape(1, -1), "g_kva": mla_kv_a_g[l].reshape(1, -1),
        "g_q": _lane_row([(0, mla_qn_g[l] * mla_scale), (MLA_NOPE, mla_qr_g[l] * mla_scale)], 2),
        "g_k": _lane_row([(0, mla_kn_g[l])], 2),
        "g_kr": _lane_row([(MLA_NOPE, mla_kr_g[l])]),
        "conv_w": jnp.broadcast_to(conv_w[l][:, :, None], (3, CONV_WIDTH, tm)),
        "conv_b": jnp.broadcast_to(conv_b[l][:, None], (CONV_WIDTH, tm)),
        "out_g": out_norm_g[l].reshape(1, -1),
        "w_out": w_out[l].astype(BF16),
        "w_g": w_gu[l][:, :dff].astype(BF16),
        "w_u": w_gu[l][:, dff:].astype(BF16),
        "w_d": w_down[l].astype(BF16),
    }


def kernel(x, c, positions, norm1_g, norm2_g, w_ada, b_ada, w_in, na_q_g, na_k_g, na_rpb, mla_q_a_g,
           mla_kv_a_g, mla_w_uq, mla_w_ukv, mla_qn_g, mla_kn_g, mla_qr_g, mla_kr_g, conv_w, conv_b,
           out_norm_g, w_out, w_gu, w_down):
    B, T, D = x.shape
    depth = w_in.shape[0]
    rows = T // GRID_W
    t = _tiles(T)
    mod = _ada_modulation(c, w_ada, b_ada)
    cos_t, sin_t = _rope_tables(positions)
    na_bias = _na_bias_tables(na_rpb, rows)
    for l in range(depth):
        p = _layer_params(l, t.mix, norm1_g, norm2_g, w_in, na_q_g, na_k_g, mla_q_a_g, mla_kv_a_g,
                          mla_w_uq, mla_w_ukv, mla_qn_g, mla_kn_g, mla_qr_g, mla_kr_g, conv_w,
                          conv_b, out_norm_g, w_out, w_gu, w_down)
        qna, kna, vna_t, qm, km, vm_t, u_t, gb_t = _proj_call(x, mod[l], cos_t, sin_t, p, t.proj, t.proj_sub)
        yna_t = _na_call(qna, kna, vna_t, na_bias, l)
        ym_t = _mla_call(qm, km, vm_t, tq=t.mla_q, tk=t.mla_k, sw=t.strip, unroll=t.mla_unroll)
        x = _mix_ffn_call(x, mod[l], yna_t, ym_t, u_t, gb_t, p, t.mix)
    return x
```

```python
import functools
from typing import NamedTuple

import jax
import jax.numpy as jnp
import numpy as np
from jax import lax
from jax.experimental import pallas as pl
from jax.experimental.pallas import tpu as pltpu

F32 = jnp.float32
BF16 = jnp.bfloat16

GRID_W = 64
HEAD_DIM = 64
NA_HEADS = 4
NA_KR = 8
NA_KC = 16
MLA_HEADS = 8
MLA_NOPE = 64
MLA_ROPE = 32
MLA_V = 64
MLA_Q_RANK = 384
MLA_KV_RANK = 256
CONV_WIDTH = 256
ROPE_THETA = 10000.0
EPS = 1e-6

LANES = 128
SUBLANES = 8
MXU_WIDTH = 256
NA_Q_ROWS = 8
NA_SUB_ROWS = MXU_WIDTH // GRID_W
ONES_ROWS = 16
MLA_VROWS = MLA_V + ONES_ROWS
NA_VROWS = HEAD_DIM + ONES_ROWS
NA_K_ROWS = 12
MASK_VALUE = -1e30
VMEM_LIMIT = 56 * 1024 * 1024


class _Tiles(NamedTuple):
    proj: int
    proj_sub: int
    mix: int
    mla_q: int
    mla_k: int
    strip: int
    mla_unroll: int
    ffn_chunk: int
    rope: int
    ada_cols: int


def _tiles(T):
    mla_k = 512
    return _Tiles(proj=min(1024, T), proj_sub=MXU_WIDTH, mix=512, mla_q=1024, mla_k=mla_k, strip=MXU_WIDTH,
                  mla_unroll=min(16, T // mla_k), ffn_chunk=MXU_WIDTH, rope=min(2048, T), ada_cols=1536)


def _cparams(sem):
    return pltpu.CompilerParams(dimension_semantics=sem, vmem_limit_bytes=VMEM_LIMIT)


def _const_spec(shape):
    nd = len(shape)
    return pl.BlockSpec(shape, lambda *_: (0,) * nd, pipeline_mode=pl.Buffered(1))


def _split_bf16(x):
    hi = x.astype(BF16)
    lo = (x - hi.astype(F32)).astype(BF16)
    return hi, lo


def _ada_kernel(c_ref, w_ref, b_ref, o_ref):
    c = c_ref[...]
    a = c * jax.nn.sigmoid(c)
    a_hi, a_lo = _split_bf16(a)
    w_hi, w_lo = _split_bf16(w_ref[0])
    acc = jnp.dot(a_hi, w_hi, preferred_element_type=F32)
    acc += jnp.dot(a_lo, w_hi, preferred_element_type=F32)
    acc += jnp.dot(a_hi, w_lo, preferred_element_type=F32)
    o_ref[0] = acc + b_ref[0]


def _ada_modulation(c, w_ada, b_ada, tn):
    L, D, N = w_ada.shape
    B = c.shape[0]
    rows = SUBLANES * pl.cdiv(B, SUBLANES)
    c_pad = jnp.zeros((rows, D), F32).at[:B].set(c)
    out = pl.pallas_call(
        _ada_kernel,
        grid=(L, N // tn),
        in_specs=[
            pl.BlockSpec((rows, D), lambda l, j: (0, 0)),
            pl.BlockSpec((1, D, tn), lambda l, j: (l, 0, j)),
            pl.BlockSpec((1, 1, tn), lambda l, j: (l, 0, j)),
        ],
        out_specs=pl.BlockSpec((1, rows, tn), lambda l, j: (l, 0, j)),
        out_shape=jax.ShapeDtypeStruct((L, rows, N), F32),
        compiler_params=_cparams(("parallel", "parallel")),
        name="ada_mod",
    )(c_pad, w_ada, b_ada.reshape(L, 1, N))
    return out[:, :B].reshape(L, B, 6, D)


def _rope_kernel(pos_ref, inv_ref, cos_ref, sin_ref):
    ang = pos_ref[0].astype(F32) * inv_ref[...]
    c = jnp.cos(ang)
    s = jnp.sin(ang)
    tm = ang.shape[1]
    pad = LANES - MLA_NOPE - MLA_ROPE
    cos_t = jnp.concatenate([jnp.ones((MLA_NOPE, tm), F32), c, c, jnp.ones((pad, tm), F32)], axis=0)
    sin_t = jnp.concatenate([jnp.zeros((MLA_NOPE, tm), F32), -s, s, jnp.zeros((pad, tm), F32)], axis=0)
    cos_ref[0] = cos_t.T
    sin_ref[0] = sin_t.T


def _rope_tables(positions, tm):
    B, T = positions.shape
    half = MLA_ROPE // 2
    inv = ROPE_THETA ** (-jnp.arange(0, MLA_ROPE, 2, dtype=F32) / MLA_ROPE)
    spec = pl.BlockSpec((1, tm, LANES), lambda b, i: (b, i, 0))
    return pl.pallas_call(
        _rope_kernel,
        grid=(B, T // tm),
        in_specs=[
            pl.BlockSpec((1, 1, tm), lambda b, i: (b, 0, i)),
            _const_spec((half, tm)),
        ],
        out_specs=[spec, spec],
        out_shape=[jax.ShapeDtypeStruct((B, T, LANES), F32)] * 2,
        compiler_params=_cparams(("parallel", "parallel")),
        name="rope_tables",
    )(positions.reshape(B, 1, T), jnp.broadcast_to(inv[:, None], (half, tm)))


def _pair_ms(x, m_ref):
    return jnp.dot((x * x).astype(BF16), m_ref[...], preferred_element_type=F32)


def _slab_rope(xs, cos, sin, first_half):
    half = MLA_ROPE // 2
    w = xs.shape[1]
    partner = jnp.where(first_half,
                        pltpu.roll(xs, w - half, 1),
                        pltpu.roll(xs, half, 1))
    return xs * cos + partner * sin


def _store_values(v_ref, tok, vt, heads, width):
    ones = jnp.ones((ONES_ROWS, vt.shape[1]), BF16)
    rows = width + ONES_ROWS
    for hd in range(heads):
        v_ref[0, hd * rows:hd * rows + width, tok] = vt[hd * width:(hd + 1) * width]
        v_ref[0, hd * rows + width:(hd + 1) * rows, tok] = ones


def _row_rms(x, gain):
    ms = jnp.mean(x * x, axis=-1, keepdims=True)
    return x * lax.rsqrt(ms + EPS) * gain


def _proj_kernel(x_ref, mod_ref, g1_ref, win_ref, wuq_ref, wuk_ref, wuv_ref, m2_ref, mna_ref,
                 gqna_ref, gkna_ref, gqa_ref, gkva_ref, gq_ref, gk_ref, gkr_ref, cos_ref, sin_ref,
                 qna_ref, kna_ref, vna_ref, qm_ref, km_ref, vm_ref, u_ref, gb_ref, *, sub):
    for t0 in range(0, x_ref.shape[1], sub):
        _proj_sub_tile(slice(t0, t0 + sub), x_ref, mod_ref, g1_ref, win_ref, wuq_ref, wuk_ref, wuv_ref,
                       m2_ref, mna_ref, gqna_ref, gkna_ref, gqa_ref, gkva_ref, gq_ref, gk_ref, gkr_ref,
                       cos_ref, sin_ref, qna_ref, kna_ref, vna_ref, qm_ref, km_ref, vm_ref, u_ref, gb_ref)


def _proj_sub_tile(tok, x_ref, mod_ref, g1_ref, win_ref, wuq_ref, wuk_ref, wuv_ref, m2_ref, mna_ref,
                   gqna_ref, gkna_ref, gqa_ref, gkva_ref, gq_ref, gk_ref, gkr_ref, cos_ref, sin_ref,
                   qna_ref, kna_ref, vna_ref, qm_ref, km_ref, vm_ref, u_ref, gb_ref):
    x = x_ref[0, tok, :]
    sh = mod_ref[0, 0:1, :]
    sc = mod_ref[0, 1:2, :]
    h = _row_rms(x, g1_ref[...]) * (1.0 + sc) + sh
    hb = h.astype(BF16)

    nq = NA_HEADS * HEAD_DIM
    pair = 2 * LANES
    cw = CONV_WIDTH
    pairs = range(0, MLA_HEADS * LANES, pair)

    pall = jnp.dot(hb, win_ref[...], preferred_element_type=F32)
    o = 0
    pq = pall[:, o:o + nq]; o += nq
    pk = pall[:, o:o + nq]; o += nq
    pv = pall[:, o:o + nq]; o += nq
    cq = pall[:, o:o + MLA_Q_RANK]; o += MLA_Q_RANK
    ckv = pall[:, o:o + MLA_KV_RANK]; o += MLA_KV_RANK
    kr = pall[:, o:o + LANES]; o += LANES
    pc = pall[:, o:o + 3 * cw]

    cq = _row_rms(cq, gqa_ref[...]).astype(BF16)
    ckv = _row_rms(ckv, gkva_ref[...]).astype(BF16)
    q = jnp.dot(cq, wuq_ref[...], preferred_element_type=F32)
    kn = jnp.dot(ckv, wuk_ref[...], preferred_element_type=F32)
    vt = jnp.dot(ckv, wuv_ref[...], preferred_element_type=F32)

    ms_qna = _pair_ms(pq, mna_ref)
    ms_kna = _pair_ms(pk, mna_ref)
    ms_q = [_pair_ms(q[:, s0:s0 + pair], m2_ref) for s0 in pairs]
    ms_k = [_pair_ms(kn[:, s0:s0 + pair], m2_ref) for s0 in pairs]

    qna_ref[0, :, tok] = (pq * lax.rsqrt(ms_qna + EPS) * gqna_ref[...]).T.astype(BF16)
    kna_ref[0, tok, :] = (pk * lax.rsqrt(ms_kna + EPS) * gkna_ref[...]).astype(BF16)
    _store_values(vna_ref, tok, pv.T.astype(BF16), NA_HEADS, HEAD_DIM)

    cos = cos_ref[0, tok, :]
    sin = sin_ref[0, tok, :]
    cos2 = jnp.concatenate([cos, cos], axis=1)
    sin2 = jnp.concatenate([sin, sin], axis=1)
    lane = lax.broadcasted_iota(jnp.int32, (1, pair), 1) % LANES
    first_half = lane < MLA_NOPE + MLA_ROPE // 2

    for s0, ms in zip(pairs, ms_q):
        qs = q[:, s0:s0 + pair] * lax.rsqrt(ms + EPS) * gq_ref[...]
        qm_ref[0, s0:s0 + pair, tok] = _slab_rope(qs, cos2, sin2, first_half).T.astype(BF16)

    kr = kr * lax.rsqrt(jnp.sum(kr * kr, axis=-1, keepdims=True) * (1.0 / MLA_ROPE) + EPS) * gkr_ref[...]
    kr = _slab_rope(kr, cos, sin, first_half[:, :LANES])
    kr2 = jnp.concatenate([kr, kr], axis=1)
    for s0, ms in zip(pairs, ms_k):
        kn_s = kn[:, s0:s0 + pair] * lax.rsqrt(ms + EPS) * gk_ref[...]
        km_ref[0, tok, s0:s0 + pair] = (kn_s + kr2).astype(BF16)
    _store_values(vm_ref, tok, vt.T.astype(BF16), MLA_HEADS, MLA_V)

    u_ref[0, :, tok] = (pc[:, 2 * cw:3 * cw] * pc[:, 0:cw]).T
    gb_ref[0, :, tok] = pc[:, cw:2 * cw].T


def _proj_call(x, mod, cos_t, sin_t, p, tm, sub):
    B, T, D = x.shape
    nq = NA_HEADS * HEAD_DIM
    nm = MLA_HEADS * LANES
    tok = lambda w: pl.BlockSpec((1, tm, w), lambda b, i: (b, i, 0))
    chan = lambda c: pl.BlockSpec((1, c, tm), lambda b, i: (b, 0, i))
    consts = [p["g1"], p["w_in"], p["w_uq"], p["w_uk"], p["w_uv"], p["m2"], p["m_na"], p["g_qna"],
              p["g_kna"], p["g_qa"], p["g_kva"], p["g_q"], p["g_k"], p["g_kr"]]
    in_specs = ([tok(D), pl.BlockSpec((1, 6, D), lambda b, i: (b, 0, 0))]
                + [_const_spec(a.shape) for a in consts] + [tok(LANES), tok(LANES)])
    out_shape = [
        jax.ShapeDtypeStruct((B, nq, T), BF16), jax.ShapeDtypeStruct((B, T, nq), BF16),
        jax.ShapeDtypeStruct((B, NA_HEADS * NA_VROWS, T), BF16),
        jax.ShapeDtypeStruct((B, nm, T), BF16), jax.ShapeDtypeStruct((B, T, nm), BF16),
        jax.ShapeDtypeStruct((B, MLA_HEADS * MLA_VROWS, T), BF16),
        jax.ShapeDtypeStruct((B, CONV_WIDTH, T), F32), jax.ShapeDtypeStruct((B, CONV_WIDTH, T), F32),
    ]
    out_specs = [chan(nq), tok(nq), chan(NA_HEADS * NA_VROWS), chan(nm), tok(nm),
                 chan(MLA_HEADS * MLA_VROWS), chan(CONV_WIDTH), chan(CONV_WIDTH)]
    return pl.pallas_call(
        functools.partial(_proj_kernel, sub=sub),
        grid=(B, T // tm),
        in_specs=in_specs,
        out_specs=out_specs,
        out_shape=out_shape,
        compiler_params=_cparams(("parallel", "parallel")),
        name="in_proj",
    )(x, mod, *consts, cos_t, sin_t)


def _na_variant(sb, nsb):
    return jnp.where(sb == 0, 0, jnp.where(sb == nsb - 1, 2, 1))


def _na_kernel(q_ref, k_ref, v_ref, *refs, rows):
    nsub = NA_Q_ROWS // NA_SUB_ROWS
    bias_refs, (o_ref, s_ref, mx_ref) = refs[:nsub], refs[nsub:]
    rb = pl.program_id(1)
    sw = NA_SUB_ROWS * GRID_W
    nk = NA_K_ROWS * GRID_W
    kh = nk // 2
    kstarts = [pl.multiple_of(jnp.clip(rb * NA_Q_ROWS + c * NA_SUB_ROWS - NA_KR // 2, 0, rows - NA_K_ROWS)
                              * GRID_W, MXU_WIDTH) for c in range(nsub)]
    units = [(h, c) for h in range(NA_HEADS) for c in range(nsub)]
    chan = lax.broadcasted_iota(jnp.int32, (LANES, 1), 0)

    def score(i):
        h, c = units[i]
        slab = slice((h // 2) * LANES, (h // 2 + 1) * LANES)
        q = q_ref[0, slab, c * sw:(c + 1) * sw]
        q = jnp.where(chan >= HEAD_DIM if h % 2 else chan < HEAD_DIM, q, jnp.zeros_like(q))
        mx = None
        for r0 in range(0, nk, kh):
            k = k_ref[0, pl.ds(pl.multiple_of(kstarts[c] + r0, LANES), kh), slab]
            s = jnp.dot(k, q, preferred_element_type=F32) + bias_refs[c][0, 0, h, r0:r0 + kh, :]
            s_ref[i % 3, r0:r0 + kh, :] = s
            part = jnp.max(s, axis=0, keepdims=True)
            mx = part if mx is None else jnp.maximum(mx, part)
        mx_ref[i % 3] = mx

    score(0)
    score(1)
    for i, (h, c) in enumerate(units):
        p = jnp.exp2(s_ref[i % 3] - mx_ref[i % 3]).astype(BF16)
        v = v_ref[0, h * NA_VROWS:(h + 1) * NA_VROWS, pl.ds(kstarts[c], nk)]
        acc = jnp.dot(v, p, preferred_element_type=F32)
        o_ref[0, h * HEAD_DIM:(h + 1) * HEAD_DIM, c * sw:(c + 1) * sw] = (
            acc[:HEAD_DIM] / acc[HEAD_DIM:HEAD_DIM + 1])
        if i + 2 < len(units):
            score(i + 2)


def _na_bias_kernel(e_ref, o_ref, *, rows):
    lane = lax.broadcasted_iota(jnp.int32, (GRID_W, LANES), 1)
    masked = jnp.full((GRID_W, LANES), MASK_VALUE, F32)
    variants = ((0, 0), (NA_SUB_ROWS, NA_SUB_ROWS - NA_KR // 2), (rows - NA_SUB_ROWS, rows - NA_K_ROWS))
    for v, (r0, ks) in enumerate(variants):
        for kr in range(NA_K_ROWS):
            for pair in range(NA_SUB_ROWS // 2):
                blocks = []
                for qr in (2 * pair, 2 * pair + 1):
                    r, k = r0 + qr, ks + kr
                    row_start = min(max(r - NA_KR // 2, 0), rows - NA_KR)
                    ok = row_start <= k < row_start + NA_KR
                    blocks.append(e_ref[0, 0, k - r + NA_KR - 1] if ok else masked)
                o_ref[0, v, 0, kr * GRID_W:(kr + 1) * GRID_W, pair * LANES:(pair + 1) * LANES] = (
                    jnp.where(lane < GRID_W, blocks[0], blocks[1]))


def _na_bias_tables(rpb, rows):
    L, H = rpb.shape[:2]
    cols = np.arange(GRID_W)
    col_start = np.clip(cols - NA_KC // 2, 0, GRID_W - NA_KC)
    col_ok = (cols[None, :] >= col_start[:, None]) & (cols[None, :] < col_start[:, None] + NA_KC)
    dc = np.clip(cols[None, :] - cols[:, None] + NA_KC - 1, 0, 2 * NA_KC - 2)
    sel = np.eye(2 * NA_KC - 1, dtype=np.float32)[dc.T]
    sel = np.concatenate([sel, sel], axis=1)
    ok = np.concatenate([col_ok.T, col_ok.T], axis=1)
    e = jnp.einsum("lhij,wqj->lhiwq", rpb.astype(F32) * float(np.log2(np.e)), jnp.asarray(sel),
                   precision=lax.Precision.HIGHEST)
    e = jnp.where(jnp.asarray(ok), e, MASK_VALUE)
    nk, sw = NA_K_ROWS * GRID_W, NA_SUB_ROWS * GRID_W
    return pl.pallas_call(
        functools.partial(_na_bias_kernel, rows=rows),
        grid=(L, H),
        in_specs=[pl.BlockSpec((1, 1, 2 * NA_KR - 1, GRID_W, LANES), lambda l, h: (l, h, 0, 0, 0))],
        out_specs=pl.BlockSpec((1, 3, 1, nk, sw), lambda l, h: (l, 0, h, 0, 0)),
        out_shape=jax.ShapeDtypeStruct((L, 3, H, nk, sw), F32),
        compiler_params=_cparams(("parallel", "parallel")),
        name="na_bias",
    )(e)


def _na_call(qna, kna, vna_t, bias, layer):
    B, T, _ = kna.shape
    rows = T // GRID_W
    nrb = rows // NA_Q_ROWS
    nsub = NA_Q_ROWS // NA_SUB_ROWS
    nq = NA_Q_ROWS * GRID_W
    sw = NA_SUB_ROWS * GRID_W
    nk = NA_K_ROWS * GRID_W
    bias_specs = [
        pl.BlockSpec((1, 1, NA_HEADS, nk, sw),
                     lambda b, rb, c=c: (layer, _na_variant(rb * nsub + c, nrb * nsub), 0, 0, 0))
        for c in range(nsub)]
    return pl.pallas_call(
        functools.partial(_na_kernel, rows=rows),
        grid=(B, nrb),
        in_specs=[
            pl.BlockSpec((1, NA_HEADS * HEAD_DIM, nq), lambda b, rb: (b, 0, rb)),
            pl.BlockSpec((1, T, NA_HEADS * HEAD_DIM), lambda b, rb: (b, 0, 0)),
            pl.BlockSpec((1, NA_HEADS * NA_VROWS, T), lambda b, rb: (b, 0, 0)),
        ] + bias_specs,
        out_specs=pl.BlockSpec((1, NA_HEADS * HEAD_DIM, nq), lambda b, rb: (b, 0, rb)),
        out_shape=jax.ShapeDtypeStruct((B, NA_HEADS * HEAD_DIM, T), F32),
        scratch_shapes=[pltpu.VMEM((3, nk, sw), F32), pltpu.VMEM((3, 1, sw), F32)],
        compiler_params=_cparams(("parallel", "arbitrary")),
        name="na_attn",
    )(qna, kna, vna_t, *([bias] * nsub))


def _mla_kernel(q_ref, k_ref, v_ref, o_ref, s_ref, mx_ref, *, tq, tk, sw, unroll):
    T = k_ref.shape[1]
    nq = T // tq
    nkv = T // tk
    strips = [slice(c * sw, (c + 1) * sw) for c in range(tq // sw)]
    ns = len(strips)
    hk = tk // 2

    def score(slot, qi, j, c):
        k = k_ref[0, pl.ds(pl.multiple_of(j * tk, tk), tk), :]
        q = q_ref[0, :, pl.ds(pl.multiple_of(qi * tq + c * sw, sw), sw)]
        s = jnp.dot(k, q, preferred_element_type=F32)
        s_ref[slot, :, strips[c]] = s
        mx_ref[slot, :, strips[c]] = jnp.max(s, axis=0, keepdims=True)

    def step(slot, j, nxt_qi, nxt_j, m, acc):
        v = v_ref[0, :, pl.ds(pl.multiple_of(j * tk, tk), tk)]
        ms, accs = [], []
        for c in range(min(2, ns)):
            score(1 - slot, nxt_qi, nxt_j, c)
        for c, sl in enumerate(strips):
            m_old = m[:, sl]
            m_new = jnp.maximum(m_old, mx_ref[slot, :, sl])
            p = jnp.exp2(s_ref[slot, :, sl] - m_new).astype(BF16)
            pv = jnp.dot(v[:, :hk], p[:hk], preferred_element_type=F32)
            if c + 2 < ns:
                score(1 - slot, nxt_qi, nxt_j, c + 2)
            pv = pv + jnp.dot(v[:, hk:], p[hk:], preferred_element_type=F32)
            accs.append(jnp.exp2(m_old - m_new) * acc[:, sl] + pv)
            ms.append(m_new)
        return jnp.concatenate(ms, axis=1), jnp.concatenate(accs, axis=1)

    def body(bi, carry):
        m, acc = carry
        t0 = bi * unroll
        qi = t0 // nkv
        j0 = t0 % nkv
        fresh = j0 == 0
        m = jnp.where(fresh, -jnp.inf, m)
        acc = jnp.where(fresh, 0.0, acc)
        for u in range(unroll):
            if u + 1 < unroll:
                nxt_qi, nxt_j = qi, j0 + u + 1
            else:
                nxt_qi = jnp.minimum(qi + (j0 + unroll) // nkv, nq - 1)
                nxt_j = (j0 + unroll) % nkv
            m, acc = step(u % 2, j0 + u, nxt_qi, nxt_j, m, acc)

        @pl.when(j0 + unroll == nkv)
        def _():
            o_ref[0, :, pl.ds(pl.multiple_of(qi * tq, tq), tq)] = acc[:MLA_V] / acc[MLA_V:MLA_V + 1]

        return m, acc

    for c in range(ns):
        score(0, 0, 0, c)
    init = (jnp.full((1, tq), -jnp.inf, F32), jnp.zeros((MLA_VROWS, tq), F32))
    lax.fori_loop(0, nq * nkv // unroll, body, init)


def _mla_call(qm, km, vm_t, tq, tk, sw, unroll):
    B, T, _ = km.shape
    assert unroll % 2 == 0 and (T // tk) % unroll == 0 and T % tq == 0 and tq % sw == 0
    return pl.pallas_call(
        functools.partial(_mla_kernel, tq=tq, tk=tk, sw=sw, unroll=unroll),
        grid=(B, MLA_HEADS),
        in_specs=[
            pl.BlockSpec((1, LANES, T), lambda b, h: (b, h, 0)),
            pl.BlockSpec((1, T, LANES), lambda b, h: (b, 0, h)),
            pl.BlockSpec((1, MLA_VROWS, T), lambda b, h: (b, h, 0)),
        ],
        out_specs=pl.BlockSpec((1, MLA_V, T), lambda b, h: (b, h, 0)),
        out_shape=jax.ShapeDtypeStruct((B, MLA_HEADS * MLA_V, T), F32),
        scratch_shapes=[pltpu.VMEM((2, tk, tq), F32), pltpu.VMEM((2, 1, tq), F32)],
        compiler_params=_cparams(("parallel", "parallel")),
        name="mla_attn",
    )(qm, km, vm_t)


def _group_rms_rows(x):
    c, tm = x.shape
    xg = x.reshape(c // HEAD_DIM, HEAD_DIM, tm)
    ms = jnp.mean(xg * xg, axis=1, keepdims=True)
    return (xg * lax.rsqrt(ms + EPS)).reshape(c, tm)


def _mix_ffn_kernel(x_ref, mod_ref, yna_ref, ym_ref, u_ref, up_ref, un_ref, gb_ref, cw_ref, cb_ref,
                    og_ref, wout_ref, g2_ref, wg_ref, wu_ref, wd_ref, o_ref, act_ref, *, chunk):
    i = pl.program_id(1)
    last = pl.num_programs(1) - 1
    u = u_ref[0]
    tm = u.shape[1]
    prev = jnp.where(i > 0, up_ref[0], 0.0)
    nxt = jnp.where(i < last, un_ref[0], 0.0)
    ext = jnp.concatenate([prev, u, nxt], axis=1)
    w = ext.shape[1]
    u_m1 = pltpu.roll(ext, 1, 1)[:, LANES:LANES + tm]
    u_p1 = pltpu.roll(ext, w - 1, 1)[:, LANES:LANES + tm]
    y = cw_ref[0] * u_m1 + cw_ref[1] * u + cw_ref[2] * u_p1 + cb_ref[...]
    yc = gb_ref[0] * y
    mixed = jnp.concatenate([_group_rms_rows(yna_ref[0]), _group_rms_rows(ym_ref[0]),
                             _group_rms_rows(yc)], axis=0)
    mixed = (mixed.T * og_ref[...]).astype(BF16)
    x = x_ref[0] + mod_ref[0, 2:3, :] * jnp.dot(mixed, wout_ref[...], preferred_element_type=F32)
    sh = mod_ref[0, 3:4, :]
    sc = mod_ref[0, 4:5, :]
    hb = (_row_rms(x, g2_ref[...]) * (1.0 + sc) + sh).astype(BF16)
    dff = wg_ref.shape[1]
    for c in range(dff // chunk):
        sl = slice(c * chunk, (c + 1) * chunk)
        g = jnp.dot(hb, wg_ref[:, sl], preferred_element_type=F32)
        up = jnp.dot(hb, wu_ref[:, sl], preferred_element_type=F32)
        act_ref[:, sl] = (g * jax.nn.sigmoid(g) * up).astype(BF16)
    out = jnp.dot(act_ref[...], wd_ref[...], preferred_element_type=F32)
    o_ref[0] = x + mod_ref[0, 5:6, :] * out


def _mix_ffn_call(x, mod, yna_t, ym_t, u_t, gb_t, p, tm, chunk):
    B, T, D = x.shape
    nb = tm // LANES
    nlb = T // LANES
    dff = p["w_g"].shape[1]
    chan = lambda c: pl.BlockSpec((1, c, tm), lambda b, i: (b, 0, i))
    consts = [p["conv_w"], p["conv_b"], p["out_g"], p["w_out"], p["g2"], p["w_g"], p["w_u"], p["w_d"]]
    return pl.pallas_call(
        functools.partial(_mix_ffn_kernel, chunk=chunk),
        grid=(B, T // tm),
        in_specs=[
            pl.BlockSpec((1, tm, D), lambda b, i: (b, i, 0)),
            pl.BlockSpec((1, 6, D), lambda b, i: (b, 0, 0)),
            chan(NA_HEADS * HEAD_DIM), chan(MLA_HEADS * MLA_V), chan(CONV_WIDTH),
            pl.BlockSpec((1, CONV_WIDTH, LANES), lambda b, i: (b, 0, jnp.maximum(i * nb - 1, 0))),
            pl.BlockSpec((1, CONV_WIDTH, LANES), lambda b, i: (b, 0, jnp.minimum((i + 1) * nb, nlb - 1))),
            chan(CONV_WIDTH),
        ] + [_const_spec(a.shape) for a in consts],
        out_specs=pl.BlockSpec((1, tm, D), lambda b, i: (b, i, 0)),
        out_shape=jax.ShapeDtypeStruct((B, T, D), F32),
        scratch_shapes=[pltpu.VMEM((tm, dff), BF16)],
        compiler_params=_cparams(("parallel", "parallel")),
        name="mix_ffn",
    )(x, mod, yna_t, ym_t, u_t, u_t, u_t, gb_t, *consts)


def _pad_heads(w, heads, width):
    k = w.shape[0]
    w = w.reshape(k, heads, width)
    return jnp.pad(w, ((0, 0), (0, 0), (0, LANES - width))).reshape(k, heads * LANES)


def _lane_row(parts, repeat=1):
    row = jnp.zeros((1, LANES), F32)
    for off, v in parts:
        row = row.at[0, off:off + v.shape[0]].set(v.astype(F32))
    return jnp.tile(row, (1, repeat))


def _norm_matrices():
    mla = np.zeros((LANES, LANES), np.float32)
    mla[:MLA_NOPE, :MLA_NOPE] = 1.0 / MLA_NOPE
    mla[MLA_NOPE:MLA_NOPE + MLA_ROPE, MLA_NOPE:MLA_NOPE + MLA_ROPE] = 1.0 / MLA_ROPE
    na = np.zeros((LANES, LANES), np.float32)
    na[:HEAD_DIM, :HEAD_DIM] = 1.0 / HEAD_DIM
    na[HEAD_DIM:, HEAD_DIM:] = 1.0 / HEAD_DIM
    z = np.zeros((LANES, LANES), np.float32)
    pair = lambda m: jnp.asarray(np.block([[m, z], [z, m]]), BF16)
    return pair(mla), pair(na)


def _layer_params(l, tm, norm1_g, norm2_g, w_in, na_q_g, na_k_g, mla_q_a_g, mla_kv_a_g, mla_w_uq,
                  mla_w_ukv, mla_qn_g, mla_kn_g, mla_qr_g, mla_kr_g, conv_w, conv_b, out_norm_g,
                  w_out, w_gu, w_down):
    naw = NA_HEADS * HEAD_DIM
    i0 = 3 * naw
    i1 = i0 + MLA_Q_RANK
    i2 = i1 + MLA_KV_RANK
    i3 = i2 + MLA_ROPE
    w = w_in[l]
    d = w.shape[0]
    kr_slab = jnp.pad(w[:, i2:i3], ((0, 0), (MLA_NOPE, LANES - MLA_NOPE - MLA_ROPE)))
    w_in_r = jnp.concatenate([w[:, :i2], kr_slab, w[:, i3:]], axis=1).astype(BF16)
    ukv = mla_w_ukv[l].reshape(MLA_KV_RANK, MLA_HEADS, MLA_NOPE + MLA_V)
    w_uk = _pad_heads(ukv[:, :, :MLA_NOPE].reshape(MLA_KV_RANK, -1), MLA_HEADS, MLA_NOPE).astype(BF16)
    w_uv = ukv[:, :, MLA_NOPE:].reshape(MLA_KV_RANK, -1).astype(BF16)
    w_uq = _pad_heads(mla_w_uq[l], MLA_HEADS, MLA_NOPE + MLA_ROPE).astype(BF16)
    na_scale = HEAD_DIM ** -0.5 * float(np.log2(np.e))
    mla_scale = (MLA_NOPE + MLA_ROPE) ** -0.5 * float(np.log2(np.e))
    dff = w_down.shape[1]
    m_mla, m_na = _norm_matrices()
    return {
        "g1": norm1_g[l].reshape(1, d), "g2": norm2_g[l].reshape(1, d),
        "w_in": w_in_r, "w_uq": w_uq, "w_uk": w_uk, "w_uv": w_uv, "m2": m_mla, "m_na": m_na,
        "g_qna": _lane_row([(0, na_q_g[l] * na_scale), (HEAD_DIM, na_q_g[l] * na_scale)], 2),
        "g_kna": _lane_row([(0, na_k_g[l]), (HEAD_DIM, na_k_g[l])], 2),
        "g_qa": mla_q_a_g[l].reshape(1, -1), "g_kva": mla_kv_a_g[l].reshape(1, -1),
        "g_q": _lane_row([(0, mla_qn_g[l] * mla_scale), (MLA_NOPE, mla_qr_g[l] * mla_scale)], 2),
        "g_k": _lane_row([(0, mla_kn_g[l])], 2),
        "g_kr": _lane_row([(MLA_NOPE, mla_kr_g[l])]),
        "conv_w": jnp.broadcast_to(conv_w[l][:, :, None], (3, CONV_WIDTH, tm)),
        "conv_b": jnp.broadcast_to(conv_b[l][:, None], (CONV_WIDTH, tm)),
        "out_g": out_norm_g[l].reshape(1, -1),
        "w_out": w_out[l].astype(BF16),
        "w_g": w_gu[l][:, :dff].astype(BF16),
        "w_u": w_gu[l][:, dff:].astype(BF16),
        "w_d": w_down[l].astype(BF16),
    }


def kernel(x, c, positions, norm1_g, norm2_g, w_ada, b_ada, w_in, na_q_g, na_k_g, na_rpb, mla_q_a_g,
           mla_kv_a_g, mla_w_uq, mla_w_ukv, mla_qn_g, mla_kn_g, mla_qr_g, mla_kr_g, conv_w, conv_b,
           out_norm_g, w_out, w_gu, w_down):
    B, T, D = x.shape
    depth = w_in.shape[0]
    rows = T // GRID_W
    t = _tiles(T)
    assert T % t.mla_q == 0 and T % t.proj == 0 and T % t.mix == 0 and T % t.rope == 0
    assert rows % NA_Q_ROWS == 0 and rows >= NA_K_ROWS + NA_SUB_ROWS
    mod = _ada_modulation(c, w_ada, b_ada, t.ada_cols)
    cos_t, sin_t = _rope_tables(positions, t.rope)
    na_bias = _na_bias_tables(na_rpb, rows)
    for l in range(depth):
        p = _layer_params(l, t.mix, norm1_g, norm2_g, w_in, na_q_g, na_k_g, mla_q_a_g, mla_kv_a_g,
                          mla_w_uq, mla_w_ukv, mla_qn_g, mla_kn_g, mla_qr_g, mla_kr_g, conv_w,
                          conv_b, out_norm_g, w_out, w_gu, w_down)
        qna, kna, vna_t, qm, km, vm_t, u_t, gb_t = _proj_call(x, mod[l], cos_t, sin_t, p, t.proj, t.proj_sub)
        yna_t = _na_call(qna, kna, vna_t, na_bias, l)
        ym_t = _mla_call(qm, km, vm_t, tq=t.mla_q, tk=t.mla_k, sw=t.strip, unroll=t.mla_unroll)
        x = _mix_ffn_call(x, mod[l], yna_t, ym_t, u_t, gb_t, p, t.mix, t.ffn_chunk)
    return x
```

```python
import functools
from typing import NamedTuple

import jax
import jax.numpy as jnp
import numpy as np
from jax import lax
from jax.experimental import pallas as pl
from jax.experimental.pallas import tpu as pltpu

F32 = jnp.float32
BF16 = jnp.bfloat16

GRID_W = 64
HEAD_DIM = 64
NA_HEADS = 4
NA_KR = 8
NA_KC = 16
MLA_HEADS = 8
MLA_NOPE = 64
MLA_ROPE = 32
MLA_V = 64
MLA_Q_RANK = 384
MLA_KV_RANK = 256
CONV_WIDTH = 256
ROPE_THETA = 10000.0
EPS = 1e-6

LANES = 128
SUBLANES = 8
MXU_WIDTH = 256
NA_Q_ROWS = 8
NA_SUB_ROWS = MXU_WIDTH // GRID_W
ONES_ROWS = 16
MLA_VROWS = MLA_V + ONES_ROWS
NA_VROWS = HEAD_DIM + ONES_ROWS
NA_K_ROWS = 12
MASK_VALUE = -1e30
VMEM_LIMIT = 56 * 1024 * 1024


class _Tiles(NamedTuple):
    proj: int
    proj_sub: int
    mix: int
    mla_q: int
    mla_k: int
    strip: int
    mla_unroll: int
    ffn_chunk: int
    rope: int
    ada_cols: int


def _tiles(T):
    mla_k = 512
    return _Tiles(proj=min(1024, T), proj_sub=MXU_WIDTH, mix=512, mla_q=1024, mla_k=mla_k, strip=MXU_WIDTH,
                  mla_unroll=min(16, T // mla_k), ffn_chunk=MXU_WIDTH, rope=min(2048, T), ada_cols=1536)


def _cparams(sem):
    return pltpu.CompilerParams(dimension_semantics=sem, vmem_limit_bytes=VMEM_LIMIT)


def _const_spec(shape):
    nd = len(shape)
    return pl.BlockSpec(shape, lambda *_: (0,) * nd, pipeline_mode=pl.Buffered(1))


def _split_bf16(x):
    hi = x.astype(BF16)
    lo = (x - hi.astype(F32)).astype(BF16)
    return hi, lo


def _ada_kernel(c_ref, w_ref, b_ref, o_ref):
    c = c_ref[...]
    a = c * jax.nn.sigmoid(c)
    a_hi, a_lo = _split_bf16(a)
    w_hi, w_lo = _split_bf16(w_ref[0])
    acc = jnp.dot(a_hi, w_hi, preferred_element_type=F32)
    acc += jnp.dot(a_lo, w_hi, preferred_element_type=F32)
    acc += jnp.dot(a_hi, w_lo, preferred_element_type=F32)
    o_ref[0] = acc + b_ref[0]


def _ada_modulation(c, w_ada, b_ada, tn):
    L, D, N = w_ada.shape
    B = c.shape[0]
    rows = SUBLANES * pl.cdiv(B, SUBLANES)
    c_pad = jnp.zeros((rows, D), F32).at[:B].set(c)
    out = pl.pallas_call(
        _ada_kernel,
        grid=(L, N // tn),
        in_specs=[
            pl.BlockSpec((rows, D), lambda l, j: (0, 0)),
            pl.BlockSpec((1, D, tn), lambda l, j: (l, 0, j)),
            pl.BlockSpec((1, 1, tn), lambda l, j: (l, 0, j)),
        ],
        out_specs=pl.BlockSpec((1, rows, tn), lambda l, j: (l, 0, j)),
        out_shape=jax.ShapeDtypeStruct((L, rows, N), F32),
        compiler_params=_cparams(("parallel", "parallel")),
        name="ada_mod",
    )(c_pad, w_ada, b_ada.reshape(L, 1, N))
    return out[:, :B].reshape(L, B, 6, D)


def _rope_kernel(pos_ref, inv_ref, cos_ref, sin_ref):
    ang = pos_ref[0].astype(F32) * inv_ref[...]
    c = jnp.cos(ang)
    s = jnp.sin(ang)
    tm = ang.shape[1]
    pad = LANES - MLA_NOPE - MLA_ROPE
    cos_t = jnp.concatenate([jnp.ones((MLA_NOPE, tm), F32), c, c, jnp.ones((pad, tm), F32)], axis=0)
    sin_t = jnp.concatenate([jnp.zeros((MLA_NOPE, tm), F32), -s, s, jnp.zeros((pad, tm), F32)], axis=0)
    cos_ref[0] = cos_t.T
    sin_ref[0] = sin_t.T


def _rope_tables(positions, tm):
    B, T = positions.shape
    half = MLA_ROPE // 2
    inv = ROPE_THETA ** (-jnp.arange(0, MLA_ROPE, 2, dtype=F32) / MLA_ROPE)
    spec = pl.BlockSpec((1, tm, LANES), lambda b, i: (b, i, 0))
    return pl.pallas_call(
        _rope_kernel,
        grid=(B, T // tm),
        in_specs=[
            pl.BlockSpec((1, 1, tm), lambda b, i: (b, 0, i)),
            _const_spec((half, tm)),
        ],
        out_specs=[spec, spec],
        out_shape=[jax.ShapeDtypeStruct((B, T, LANES), F32)] * 2,
        compiler_params=_cparams(("parallel", "parallel")),
        name="rope_tables",
    )(positions.reshape(B, 1, T), jnp.broadcast_to(inv[:, None], (half, tm)))


def _pair_ms(x, m_ref):
    return jnp.dot((x * x).astype(BF16), m_ref[...], preferred_element_type=F32)


def _slab_rope(xs, cos, sin, first_half):
    half = MLA_ROPE // 2
    w = xs.shape[1]
    partner = jnp.where(first_half,
                        pltpu.roll(xs, w - half, 1),
                        pltpu.roll(xs, half, 1))
    return xs * cos + partner * sin


def _store_values(v_ref, tok, vt, heads, width):
    ones = jnp.ones((ONES_ROWS, vt.shape[1]), BF16)
    rows = width + ONES_ROWS
    for hd in range(heads):
        v_ref[0, hd * rows:hd * rows + width, tok] = vt[hd * width:(hd + 1) * width]
        v_ref[0, hd * rows + width:(hd + 1) * rows, tok] = ones


def _row_rms(x, gain):
    ms = jnp.mean(x * x, axis=-1, keepdims=True)
    return x * lax.rsqrt(ms + EPS) * gain


def _proj_kernel(x_ref, mod_ref, g1_ref, win_ref, wuq_ref, wuk_ref, wuv_ref, m2_ref, mna_ref,
                 gqna_ref, gkna_ref, gqa_ref, gkva_ref, gq_ref, gk_ref, gkr_ref, cos_ref, sin_ref,
                 qna_ref, kna_ref, vna_ref, qm_ref, km_ref, vm_ref, u_ref, gb_ref, *, sub):
    for t0 in range(0, x_ref.shape[1], sub):
        _proj_sub_tile(slice(t0, t0 + sub), x_ref, mod_ref, g1_ref, win_ref, wuq_ref, wuk_ref, wuv_ref,
                       m2_ref, mna_ref, gqna_ref, gkna_ref, gqa_ref, gkva_ref, gq_ref, gk_ref, gkr_ref,
                       cos_ref, sin_ref, qna_ref, kna_ref, vna_ref, qm_ref, km_ref, vm_ref, u_ref, gb_ref)


def _proj_sub_tile(tok, x_ref, mod_ref, g1_ref, win_ref, wuq_ref, wuk_ref, wuv_ref, m2_ref, mna_ref,
                   gqna_ref, gkna_ref, gqa_ref, gkva_ref, gq_ref, gk_ref, gkr_ref, cos_ref, sin_ref,
                   qna_ref, kna_ref, vna_ref, qm_ref, km_ref, vm_ref, u_ref, gb_ref):
    x = x_ref[0, tok, :]
    sh = mod_ref[0, 0:1, :]
    sc = mod_ref[0, 1:2, :]
    h = _row_rms(x, g1_ref[...]) * (1.0 + sc) + sh
    hb = h.astype(BF16)

    nq = NA_HEADS * HEAD_DIM
    pair = 2 * LANES
    cw = CONV_WIDTH
    pairs = range(0, MLA_HEADS * LANES, pair)

    pall = jnp.dot(hb, win_ref[...], preferred_element_type=F32)
    o = 0
    pq = pall[:, o:o + nq]; o += nq
    pk = pall[:, o:o + nq]; o += nq
    pv = pall[:, o:o + nq]; o += nq
    cq = pall[:, o:o + MLA_Q_RANK]; o += MLA_Q_RANK
    ckv = pall[:, o:o + MLA_KV_RANK]; o += MLA_KV_RANK
    kr = pall[:, o:o + LANES]; o += LANES
    pc = pall[:, o:o + 3 * cw]

    cq = _row_rms(cq, gqa_ref[...]).astype(BF16)
    ckv = _row_rms(ckv, gkva_ref[...]).astype(BF16)
    q = jnp.dot(cq, wuq_ref[...], preferred_element_type=F32)
    kn = jnp.dot(ckv, wuk_ref[...], preferred_element_type=F32)
    vt = jnp.dot(ckv, wuv_ref[...], preferred_element_type=F32)

    ms_qna = _pair_ms(pq, mna_ref)
    ms_kna = _pair_ms(pk, mna_ref)
    ms_q = [_pair_ms(q[:, s0:s0 + pair], m2_ref) for s0 in pairs]
    ms_k = [_pair_ms(kn[:, s0:s0 + pair], m2_ref) for s0 in pairs]

    qna_ref[0, :, tok] = (pq * lax.rsqrt(ms_qna + EPS) * gqna_ref[...]).T.astype(BF16)
    kna_ref[0, tok, :] = (pk * lax.rsqrt(ms_kna + EPS) * gkna_ref[...]).astype(BF16)
    _store_values(vna_ref, tok, pv.T.astype(BF16), NA_HEADS, HEAD_DIM)

    cos = cos_ref[0, tok, :]
    sin = sin_ref[0, tok, :]
    cos2 = jnp.concatenate([cos, cos], axis=1)
    sin2 = jnp.concatenate([sin, sin], axis=1)
    lane = lax.broadcasted_iota(jnp.int32, (1, pair), 1) % LANES
    first_half = lane < MLA_NOPE + MLA_ROPE // 2

    for s0, ms in zip(pairs, ms_q):
        qs = q[:, s0:s0 + pair] * lax.rsqrt(ms + EPS) * gq_ref[...]
        qm_ref[0, s0:s0 + pair, tok] = _slab_rope(qs, cos2, sin2, first_half).T.astype(BF16)

    kr = kr * lax.rsqrt(jnp.sum(kr * kr, axis=-1, keepdims=True) * (1.0 / MLA_ROPE) + EPS) * gkr_ref[...]
    kr = _slab_rope(kr, cos, sin, first_half[:, :LANES])
    kr2 = jnp.concatenate([kr, kr], axis=1)
    for s0, ms in zip(pairs, ms_k):
        kn_s = kn[:, s0:s0 + pair] * lax.rsqrt(ms + EPS) * gk_ref[...]
        km_ref[0, tok, s0:s0 + pair] = (kn_s + kr2).astype(BF16)
    _store_values(vm_ref, tok, vt.T.astype(BF16), MLA_HEADS, MLA_V)

    u_ref[0, :, tok] = (pc[:, 2 * cw:3 * cw] * pc[:, 0:cw]).T
    gb_ref[0, :, tok] = pc[:, cw:2 * cw].T


def _proj_call(x, mod, cos_t, sin_t, p, tm, sub):
    B, T, D = x.shape
    nq = NA_HEADS * HEAD_DIM
    nm = MLA_HEADS * LANES
    tok = lambda w: pl.BlockSpec((1, tm, w), lambda b, i: (b, i, 0))
    chan = lambda c: pl.BlockSpec((1, c, tm), lambda b, i: (b, 0, i))
    consts = [p["g1"], p["w_in"], p["w_uq"], p["w_uk"], p["w_uv"], p["m2"], p["m_na"], p["g_qna"],
              p["g_kna"], p["g_qa"], p["g_kva"], p["g_q"], p["g_k"], p["g_kr"]]
    in_specs = ([tok(D), pl.BlockSpec((1, 6, D), lambda b, i: (b, 0, 0))]
                + [_const_spec(a.shape) for a in consts] + [tok(LANES), tok(LANES)])
    out_shape = [
        jax.ShapeDtypeStruct((B, nq, T), BF16), jax.ShapeDtypeStruct((B, T, nq), BF16),
        jax.ShapeDtypeStruct((B, NA_HEADS * NA_VROWS, T), BF16),
        jax.ShapeDtypeStruct((B, nm, T), BF16), jax.ShapeDtypeStruct((B, T, nm), BF16),
        jax.ShapeDtypeStruct((B, MLA_HEADS * MLA_VROWS, T), BF16),
        jax.ShapeDtypeStruct((B, CONV_WIDTH, T), F32), jax.ShapeDtypeStruct((B, CONV_WIDTH, T), F32),
    ]
    out_specs = [chan(nq), tok(nq), chan(NA_HEADS * NA_VROWS), chan(nm), tok(nm),
                 chan(MLA_HEADS * MLA_VROWS), chan(CONV_WIDTH), chan(CONV_WIDTH)]
    return pl.pallas_call(
        functools.partial(_proj_kernel, sub=sub),
        grid=(B, T // tm),
        in_specs=in_specs,
        out_specs=out_specs,
        out_shape=out_shape,
        compiler_params=_cparams(("parallel", "parallel")),
        name="in_proj",
    )(x, mod, *consts, cos_t, sin_t)


def _na_variant(sb, nsb):
    return jnp.where(sb == 0, 0, jnp.where(sb == nsb - 1, 2, 1))


def _na_kernel(q_ref, k_ref, v_ref, *refs, rows):
    nsub = NA_Q_ROWS // NA_SUB_ROWS
    bias_refs, (o_ref, s_ref, mx_ref) = refs[:nsub], refs[nsub:]
    rb = pl.program_id(1)
    sw = NA_SUB_ROWS * GRID_W
    nk = NA_K_ROWS * GRID_W
    kh = nk // 2
    kstarts = [pl.multiple_of(jnp.clip(rb * NA_Q_ROWS + c * NA_SUB_ROWS - NA_KR // 2, 0, rows - NA_K_ROWS)
                              * GRID_W, MXU_WIDTH) for c in range(nsub)]
    units = [(h, c) for h in range(NA_HEADS) for c in range(nsub)]
    chan = lax.broadcasted_iota(jnp.int32, (LANES, 1), 0)

    def score(i):
        h, c = units[i]
        slab = slice((h // 2) * LANES, (h // 2 + 1) * LANES)
        q = q_ref[0, slab, c * sw:(c + 1) * sw]
        q = jnp.where(chan >= HEAD_DIM if h % 2 else chan < HEAD_DIM, q, jnp.zeros_like(q))
        mx = None
        for r0 in range(0, nk, kh):
            k = k_ref[0, pl.ds(pl.multiple_of(kstarts[c] + r0, LANES), kh), slab]
            s = jnp.dot(k, q, preferred_element_type=F32) + bias_refs[c][0, 0, h, r0:r0 + kh, :]
            s_ref[i % 3, r0:r0 + kh, :] = s
            part = jnp.max(s, axis=0, keepdims=True)
            mx = part if mx is None else jnp.maximum(mx, part)
        mx_ref[i % 3] = mx

    score(0)
    score(1)
    for i, (h, c) in enumerate(units):
        p = jnp.exp2(s_ref[i % 3] - mx_ref[i % 3]).astype(BF16)
        v = v_ref[0, h * NA_VROWS:(h + 1) * NA_VROWS, pl.ds(kstarts[c], nk)]
        acc = jnp.dot(v, p, preferred_element_type=F32)
        o_ref[0, h * HEAD_DIM:(h + 1) * HEAD_DIM, c * sw:(c + 1) * sw] = (
            acc[:HEAD_DIM] / acc[HEAD_DIM:HEAD_DIM + 1])
        if i + 2 < len(units):
            score(i + 2)


def _na_bias_kernel(e_ref, o_ref, *, rows):
    lane = lax.broadcasted_iota(jnp.int32, (GRID_W, LANES), 1)
    masked = jnp.full((GRID_W, LANES), MASK_VALUE, F32)
    variants = ((0, 0), (NA_SUB_ROWS, NA_SUB_ROWS - NA_KR // 2), (rows - NA_SUB_ROWS, rows - NA_K_ROWS))
    for v, (r0, ks) in enumerate(variants):
        for kr in range(NA_K_ROWS):
            for pair in range(NA_SUB_ROWS // 2):
                blocks = []
                for qr in (2 * pair, 2 * pair + 1):
                    r, k = r0 + qr, ks + kr
                    row_start = min(max(r - NA_KR // 2, 0), rows - NA_KR)
                    ok = row_start <= k < row_start + NA_KR
                    blocks.append(e_ref[0, 0, k - r + NA_KR - 1] if ok else masked)
                o_ref[0, v, 0, kr * GRID_W:(kr + 1) * GRID_W, pair * LANES:(pair + 1) * LANES] = (
                    jnp.where(lane < GRID_W, blocks[0], blocks[1]))


def _na_bias_tables(rpb, rows):
    L, H = rpb.shape[:2]
    cols = np.arange(GRID_W)
    col_start = np.clip(cols - NA_KC // 2, 0, GRID_W - NA_KC)
    col_ok = (cols[None, :] >= col_start[:, None]) & (cols[None, :] < col_start[:, None] + NA_KC)
    dc = np.clip(cols[None, :] - cols[:, None] + NA_KC - 1, 0, 2 * NA_KC - 2)
    sel = np.eye(2 * NA_KC - 1, dtype=np.float32)[dc.T]
    sel = np.concatenate([sel, sel], axis=1)
    ok = np.concatenate([col_ok.T, col_ok.T], axis=1)
    e = jnp.einsum("lhij,wqj->lhiwq", rpb.astype(F32) * float(np.log2(np.e)), jnp.asarray(sel),
                   precision=lax.Precision.HIGHEST)
    e = jnp.where(jnp.asarray(ok), e, MASK_VALUE)
    nk, sw = NA_K_ROWS * GRID_W, NA_SUB_ROWS * GRID_W
    return pl.pallas_call(
        functools.partial(_na_bias_kernel, rows=rows),
        grid=(L, H),
        in_specs=[pl.BlockSpec((1, 1, 2 * NA_KR - 1, GRID_W, LANES), lambda l, h: (l, h, 0, 0, 0))],
        out_specs=pl.BlockSpec((1, 3, 1, nk, sw), lambda l, h: (l, 0, h, 0, 0)),
        out_shape=jax.ShapeDtypeStruct((L, 3, H, nk, sw), F32),
        compiler_params=_cparams(("parallel", "parallel")),
        name="na_bias",
    )(e)


def _na_call(qna, kna, vna_t, bias, layer):
    B, T, _ = kna.shape
    rows = T // GRID_W
    nrb = rows // NA_Q_ROWS
    nsub = NA_Q_ROWS // NA_SUB_ROWS
    nq = NA_Q_ROWS * GRID_W
    sw = NA_SUB_ROWS * GRID_W
    nk = NA_K_ROWS * GRID_W
    bias_specs = [
        pl.BlockSpec((1, 1, NA_HEADS, nk, sw),
                     lambda b, rb, c=c: (layer, _na_variant(rb * nsub + c, nrb * nsub), 0, 0, 0))
        for c in range(nsub)]
    return pl.pallas_call(
        functools.partial(_na_kernel, rows=rows),
        grid=(B, nrb),
        in_specs=[
            pl.BlockSpec((1, NA_HEADS * HEAD_DIM, nq), lambda b, rb: (b, 0, rb)),
            pl.BlockSpec((1, T, NA_HEADS * HEAD_DIM), lambda b, rb: (b, 0, 0)),
            pl.BlockSpec((1, NA_HEADS * NA_VROWS, T), lambda b, rb: (b, 0, 0)),
        ] + bias_specs,
        out_specs=pl.BlockSpec((1, NA_HEADS * HEAD_DIM, nq), lambda b, rb: (b, 0, rb)),
        out_shape=jax.ShapeDtypeStruct((B, NA_HEADS * HEAD_DIM, T), F32),
        scratch_shapes=[pltpu.VMEM((3, nk, sw), F32), pltpu.VMEM((3, 1, sw), F32)],
        compiler_params=_cparams(("parallel", "arbitrary")),
        name="na_attn",
    )(qna, kna, vna_t, *([bias] * nsub))


def _mla_kernel(q_ref, k_ref, v_ref, o_ref, s_ref, mx_ref, *, tq, tk, sw, unroll):
    T = k_ref.shape[1]
    nq = T // tq
    nkv = T // tk
    strips = [slice(c * sw, (c + 1) * sw) for c in range(tq // sw)]
    ns = len(strips)
    hk = tk // 2

    def score(slot, qi, j, c):
        k = k_ref[0, pl.ds(pl.multiple_of(j * tk, tk), tk), :]
        q = q_ref[0, :, pl.ds(pl.multiple_of(qi * tq + c * sw, sw), sw)]
        s = jnp.dot(k, q, preferred_element_type=F32)
        s_ref[slot, c] = s
        mx_ref[slot, :, strips[c]] = jnp.max(s, axis=0, keepdims=True)

    def step(slot, j, nxt_qi, nxt_j, m, acc):
        v = v_ref[0, :, pl.ds(pl.multiple_of(j * tk, tk), tk)]
        ms, accs = [], []
        for c in range(min(2, ns)):
            score(1 - slot, nxt_qi, nxt_j, c)
        for c, sl in enumerate(strips):
            m_old = m[:, sl]
            m_new = jnp.maximum(m_old, mx_ref[slot, :, sl])
            p = jnp.exp2(s_ref[slot, c] - m_new).astype(BF16)
            pv = jnp.dot(v[:, :hk], p[:hk], preferred_element_type=F32)
            if c + 2 < ns:
                score(1 - slot, nxt_qi, nxt_j, c + 2)
            pv = pv + jnp.dot(v[:, hk:], p[hk:], preferred_element_type=F32)
            accs.append(jnp.exp2(m_old - m_new) * acc[:, sl] + pv)
            ms.append(m_new)
        return jnp.concatenate(ms, axis=1), jnp.concatenate(accs, axis=1)

    def body(bi, carry):
        m, acc = carry
        t0 = bi * unroll
        qi = t0 // nkv
        j0 = t0 % nkv
        fresh = j0 == 0
        m = jnp.where(fresh, -jnp.inf, m)
        acc = jnp.where(fresh, 0.0, acc)
        for u in range(unroll):
            if u + 1 < unroll:
                nxt_qi, nxt_j = qi, j0 + u + 1
            else:
                nxt_qi = jnp.minimum(qi + (j0 + unroll) // nkv, nq - 1)
                nxt_j = (j0 + unroll) % nkv
            m, acc = step(u % 2, j0 + u, nxt_qi, nxt_j, m, acc)

        @pl.when(j0 + unroll == nkv)
        def _():
            o_ref[0, :, pl.ds(pl.multiple_of(qi * tq, tq), tq)] = acc[:MLA_V] / acc[MLA_V:MLA_V + 1]

        return m, acc

    for c in range(ns):
        score(0, 0, 0, c)
    init = (jnp.full((1, tq), -jnp.inf, F32), jnp.zeros((MLA_VROWS, tq), F32))
    lax.fori_loop(0, nq * nkv // unroll, body, init)


def _mla_call(qm, km, vm_t, tq, tk, sw, unroll):
    B, T, _ = km.shape
    assert unroll % 2 == 0 and (T // tk) % unroll == 0 and T % tq == 0 and tq % sw == 0
    return pl.pallas_call(
        functools.partial(_mla_kernel, tq=tq, tk=tk, sw=sw, unroll=unroll),
        grid=(B, MLA_HEADS),
        in_specs=[
            pl.BlockSpec((1, LANES, T), lambda b, h: (b, h, 0)),
            pl.BlockSpec((1, T, LANES), lambda b, h: (b, 0, h)),
            pl.BlockSpec((1, MLA_VROWS, T), lambda b, h: (b, h, 0)),
        ],
        out_specs=pl.BlockSpec((1, MLA_V, T), lambda b, h: (b, h, 0)),
        out_shape=jax.ShapeDtypeStruct((B, MLA_HEADS * MLA_V, T), F32),
        scratch_shapes=[pltpu.VMEM((2, tq // sw, tk, sw), F32), pltpu.VMEM((2, 1, tq), F32)],
        compiler_params=_cparams(("parallel", "parallel")),
        name="mla_attn",
    )(qm, km, vm_t)


def _group_rms_rows(x):
    c, tm = x.shape
    xg = x.reshape(c // HEAD_DIM, HEAD_DIM, tm)
    ms = jnp.mean(xg * xg, axis=1, keepdims=True)
    return (xg * lax.rsqrt(ms + EPS)).reshape(c, tm)


def _mix_ffn_kernel(x_ref, mod_ref, yna_ref, ym_ref, u_ref, up_ref, un_ref, gb_ref, cw_ref, cb_ref,
                    og_ref, wout_ref, g2_ref, wg_ref, wu_ref, wd_ref, o_ref, act_ref, *, chunk):
    i = pl.program_id(1)
    last = pl.num_programs(1) - 1
    u = u_ref[0]
    tm = u.shape[1]
    prev = jnp.where(i > 0, up_ref[0], 0.0)
    nxt = jnp.where(i < last, un_ref[0], 0.0)
    ext = jnp.concatenate([prev, u, nxt], axis=1)
    w = ext.shape[1]
    u_m1 = pltpu.roll(ext, 1, 1)[:, LANES:LANES + tm]
    u_p1 = pltpu.roll(ext, w - 1, 1)[:, LANES:LANES + tm]
    y = cw_ref[0] * u_m1 + cw_ref[1] * u + cw_ref[2] * u_p1 + cb_ref[...]
    yc = gb_ref[0] * y
    mixed = jnp.concatenate([_group_rms_rows(yna_ref[0]), _group_rms_rows(ym_ref[0]),
                             _group_rms_rows(yc)], axis=0)
    mixed = (mixed.T * og_ref[...]).astype(BF16)
    x = x_ref[0] + mod_ref[0, 2:3, :] * jnp.dot(mixed, wout_ref[...], preferred_element_type=F32)
    sh = mod_ref[0, 3:4, :]
    sc = mod_ref[0, 4:5, :]
    hb = (_row_rms(x, g2_ref[...]) * (1.0 + sc) + sh).astype(BF16)
    dff = wg_ref.shape[1]
    for c in range(dff // chunk):
        sl = slice(c * chunk, (c + 1) * chunk)
        g = jnp.dot(hb, wg_ref[:, sl], preferred_element_type=F32)
        up = jnp.dot(hb, wu_ref[:, sl], preferred_element_type=F32)
        act_ref[:, sl] = (g * jax.nn.sigmoid(g) * up).astype(BF16)
    out = jnp.dot(act_ref[...], wd_ref[...], preferred_element_type=F32)
    o_ref[0] = x + mod_ref[0, 5:6, :] * out


def _mix_ffn_call(x, mod, yna_t, ym_t, u_t, gb_t, p, tm, chunk):
    B, T, D = x.shape
    nb = tm // LANES
    nlb = T // LANES
    dff = p["w_g"].shape[1]
    chan = lambda c: pl.BlockSpec((1, c, tm), lambda b, i: (b, 0, i))
    consts = [p["conv_w"], p["conv_b"], p["out_g"], p["w_out"], p["g2"], p["w_g"], p["w_u"], p["w_d"]]
    return pl.pallas_call(
        functools.partial(_mix_ffn_kernel, chunk=chunk),
        grid=(B, T // tm),
        in_specs=[
            pl.BlockSpec((1, tm, D), lambda b, i: (b, i, 0)),
            pl.BlockSpec((1, 6, D), lambda b, i: (b, 0, 0)),
            chan(NA_HEADS * HEAD_DIM), chan(MLA_HEADS * MLA_V), chan(CONV_WIDTH),
            pl.BlockSpec((1, CONV_WIDTH, LANES), lambda b, i: (b, 0, jnp.maximum(i * nb - 1, 0))),
            pl.BlockSpec((1, CONV_WIDTH, LANES), lambda b, i: (b, 0, jnp.minimum((i + 1) * nb, nlb - 1))),
            chan(CONV_WIDTH),
        ] + [_const_spec(a.shape) for a in consts],
        out_specs=pl.BlockSpec((1, tm, D), lambda b, i: (b, i, 0)),
        out_shape=jax.ShapeDtypeStruct((B, T, D), F32),
        scratch_shapes=[pltpu.VMEM((tm, dff), BF16)],
        compiler_params=_cparams(("parallel", "parallel")),
        name="mix_ffn",
    )(x, mod, yna_t, ym_t, u_t, u_t, u_t, gb_t, *consts)


def _pad_heads(w, heads, width):
    k = w.shape[0]
    w = w.reshape(k, heads, width)
    return jnp.pad(w, ((0, 0), (0, 0), (0, LANES - width))).reshape(k, heads * LANES)


def _lane_row(parts, repeat=1):
    row = jnp.zeros((1, LANES), F32)
    for off, v in parts:
        row = row.at[0, off:off + v.shape[0]].set(v.astype(F32))
    return jnp.tile(row, (1, repeat))


def _norm_matrices():
    mla = np.zeros((LANES, LANES), np.float32)
    mla[:MLA_NOPE, :MLA_NOPE] = 1.0 / MLA_NOPE
    mla[MLA_NOPE:MLA_NOPE + MLA_ROPE, MLA_NOPE:MLA_NOPE + MLA_ROPE] = 1.0 / MLA_ROPE
    na = np.zeros((LANES, LANES), np.float32)
    na[:HEAD_DIM, :HEAD_DIM] = 1.0 / HEAD_DIM
    na[HEAD_DIM:, HEAD_DIM:] = 1.0 / HEAD_DIM
    z = np.zeros((LANES, LANES), np.float32)
    pair = lambda m: jnp.asarray(np.block([[m, z], [z, m]]), BF16)
    return pair(mla), pair(na)


def _layer_params(l, tm, norm1_g, norm2_g, w_in, na_q_g, na_k_g, mla_q_a_g, mla_kv_a_g, mla_w_uq,
                  mla_w_ukv, mla_qn_g, mla_kn_g, mla_qr_g, mla_kr_g, conv_w, conv_b, out_norm_g,
                  w_out, w_gu, w_down):
    naw = NA_HEADS * HEAD_DIM
    i0 = 3 * naw
    i1 = i0 + MLA_Q_RANK
    i2 = i1 + MLA_KV_RANK
    i3 = i2 + MLA_ROPE
    w = w_in[l]
    d = w.shape[0]
    kr_slab = jnp.pad(w[:, i2:i3], ((0, 0), (MLA_NOPE, LANES - MLA_NOPE - MLA_ROPE)))
    w_in_r = jnp.concatenate([w[:, :i2], kr_slab, w[:, i3:]], axis=1).astype(BF16)
    ukv = mla_w_ukv[l].reshape(MLA_KV_RANK, MLA_HEADS, MLA_NOPE + MLA_V)
    w_uk = _pad_heads(ukv[:, :, :MLA_NOPE].reshape(MLA_KV_RANK, -1), MLA_HEADS, MLA_NOPE).astype(BF16)
    w_uv = ukv[:, :, MLA_NOPE:].reshape(MLA_KV_RANK, -1).astype(BF16)
    w_uq = _pad_heads(mla_w_uq[l], MLA_HEADS, MLA_NOPE + MLA_ROPE).astype(BF16)
    na_scale = HEAD_DIM ** -0.5 * float(np.log2(np.e))
    mla_scale = (MLA_NOPE + MLA_ROPE) ** -0.5 * float(np.log2(np.e))
    dff = w_down.shape[1]
    m_mla, m_na = _norm_matrices()
    return {
        "g1": norm1_g[l].reshape(1, d), "g2": norm2_g[l].reshape(1, d),
        "w_in": w_in_r, "w_uq": w_uq, "w_uk": w_uk, "w_uv": w_uv, "m2": m_mla, "m_na": m_na,
        "g_qna": _lane_row([(0, na_q_g[l] * na_scale), (HEAD_DIM, na_q_g[l] * na_scale)], 2),
        "g_kna": _lane_row([(0, na_k_g[l]), (HEAD_DIM, na_k_g[l])], 2),
        "g_qa": mla_q_a_g[l].reshape(1, -1), "g_kva": mla_kv_a_g[l].reshape(1, -1),
        "g_q": _lane_row([(0, mla_qn_g[l] * mla_scale), (MLA_NOPE, mla_qr_g[l] * mla_scale)], 2),
        "g_k": _lane_row([(0, mla_kn_g[l])], 2),
        "g_kr": _lane_row([(MLA_NOPE, mla_kr_g[l])]),
        "conv_w": jnp.broadcast_to(conv_w[l][:, :, None], (3, CONV_WIDTH, tm)),
        "conv_b": jnp.broadcast_to(conv_b[l][:, None], (CONV_WIDTH, tm)),
        "out_g": out_norm_g[l].reshape(1, -1),
        "w_out": w_out[l].astype(BF16),
        "w_g": w_gu[l][:, :dff].astype(BF16),
        "w_u": w_gu[l][:, dff:].astype(BF16),
        "w_d": w_down[l].astype(BF16),
    }


def kernel(x, c, positions, norm1_g, norm2_g, w_ada, b_ada, w_in, na_q_g, na_k_g, na_rpb, mla_q_a_g,
           mla_kv_a_g, mla_w_uq, mla_w_ukv, mla_qn_g, mla_kn_g, mla_qr_g, mla_kr_g, conv_w, conv_b,
           out_norm_g, w_out, w_gu, w_down):
    B, T, D = x.shape
    depth = w_in.shape[0]
    rows = T // GRID_W
    t = _tiles(T)
    assert T % t.mla_q == 0 and T % t.proj == 0 and T % t.mix == 0 and T % t.rope == 0
    assert rows % NA_Q_ROWS == 0 and rows >= NA_K_ROWS + NA_SUB_ROWS
    mod = _ada_modulation(c, w_ada, b_ada, t.ada_cols)
    cos_t, sin_t = _rope_tables(positions, t.rope)
    na_bias = _na_bias_tables(na_rpb, rows)
    for l in range(depth):
        p = _layer_params(l, t.mix, norm1_g, norm2_g, w_in, na_q_g, na_k_g, mla_q_a_g, mla_kv_a_g,
                          mla_w_uq, mla_w_ukv, mla_qn_g, mla_kn_g, mla_qr_g, mla_kr_g, conv_w,
                          conv_b, out_norm_g, w_out, w_gu, w_down)
        qna, kna, vna_t, qm, km, vm_t, u_t, gb_t = _proj_call(x, mod[l], cos_t, sin_t, p, t.proj, t.proj_sub)
        yna_t = _na_call(qna, kna, vna_t, na_bias, l)
        ym_t = _mla_call(qm, km, vm_t, tq=t.mla_q, tk=t.mla_k, sw=t.strip, unroll=t.mla_unroll)
        x = _mix_ffn_call(x, mod[l], yna_t, ym_t, u_t, gb_t, p, t.mix, t.ffn_chunk)
    return x
```

```python
import functools
from typing import NamedTuple

import jax
import jax.numpy as jnp
import numpy as np
from jax import lax
from jax.experimental import pallas as pl
from jax.experimental.pallas import tpu as pltpu

F32 = jnp.float32
BF16 = jnp.bfloat16

GRID_W = 64
HEAD_DIM = 64
NA_HEADS = 4
NA_KR = 8
NA_KC = 16
MLA_HEADS = 8
MLA_NOPE = 64
MLA_ROPE = 32
MLA_V = 64
MLA_Q_RANK = 384
MLA_KV_RANK = 256
CONV_WIDTH = 256
ROPE_THETA = 10000.0
EPS = 1e-6

LANES = 128
SUBLANES = 8
MXU_WIDTH = 256
NA_Q_ROWS = 32
NA_SUB_ROWS = MXU_WIDTH // GRID_W
ONES_ROWS = 16
MLA_VROWS = MLA_V + ONES_ROWS
NA_VROWS = HEAD_DIM + ONES_ROWS
NA_K_ROWS = 12
MASK_VALUE = -1e30
VMEM_LIMIT = 56 * 1024 * 1024


class _Tiles(NamedTuple):
    proj: int
    proj_sub: int
    mix: int
    mla_q: int
    mla_k: int
    strip: int
    mla_unroll: int
    ffn_chunk: int
    rope: int
    ada_cols: int


def _tiles(T):
    mla_k = 512
    return _Tiles(proj=min(1024, T), proj_sub=MXU_WIDTH, mix=512, mla_q=1024, mla_k=mla_k, strip=MXU_WIDTH,
                  mla_unroll=min(16, T // mla_k), ffn_chunk=MXU_WIDTH, rope=min(2048, T), ada_cols=1536)


def _cparams(sem):
    return pltpu.CompilerParams(dimension_semantics=sem, vmem_limit_bytes=VMEM_LIMIT)


def _const_spec(shape):
    nd = len(shape)
    return pl.BlockSpec(shape, lambda *_: (0,) * nd, pipeline_mode=pl.Buffered(1))


def _split_bf16(x):
    hi = x.astype(BF16)
    lo = (x - hi.astype(F32)).astype(BF16)
    return hi, lo


def _ada_kernel(c_ref, w_ref, b_ref, o_ref):
    c = c_ref[...]
    a = c * jax.nn.sigmoid(c)
    a_hi, a_lo = _split_bf16(a)
    w_hi, w_lo = _split_bf16(w_ref[0])
    acc = jnp.dot(a_hi, w_hi, preferred_element_type=F32)
    acc += jnp.dot(a_lo, w_hi, preferred_element_type=F32)
    acc += jnp.dot(a_hi, w_lo, preferred_element_type=F32)
    o_ref[0] = acc + b_ref[0]


def _ada_modulation(c, w_ada, b_ada, tn):
    L, D, N = w_ada.shape
    B = c.shape[0]
    rows = SUBLANES * pl.cdiv(B, SUBLANES)
    c_pad = jnp.zeros((rows, D), F32).at[:B].set(c)
    out = pl.pallas_call(
        _ada_kernel,
        grid=(L, N // tn),
        in_specs=[
            pl.BlockSpec((rows, D), lambda l, j: (0, 0)),
            pl.BlockSpec((1, D, tn), lambda l, j: (l, 0, j)),
            pl.BlockSpec((1, 1, tn), lambda l, j: (l, 0, j)),
        ],
        out_specs=pl.BlockSpec((1, rows, tn), lambda l, j: (l, 0, j)),
        out_shape=jax.ShapeDtypeStruct((L, rows, N), F32),
        compiler_params=_cparams(("parallel", "parallel")),
        name="ada_mod",
    )(c_pad, w_ada, b_ada.reshape(L, 1, N))
    return out[:, :B].reshape(L, B, 6, D)


def _rope_kernel(pos_ref, inv_ref, cos_ref, sin_ref):
    ang = pos_ref[0].astype(F32) * inv_ref[...]
    c = jnp.cos(ang)
    s = jnp.sin(ang)
    tm = ang.shape[1]
    pad = LANES - MLA_NOPE - MLA_ROPE
    cos_t = jnp.concatenate([jnp.ones((MLA_NOPE, tm), F32), c, c, jnp.ones((pad, tm), F32)], axis=0)
    sin_t = jnp.concatenate([jnp.zeros((MLA_NOPE, tm), F32), -s, s, jnp.zeros((pad, tm), F32)], axis=0)
    cos_ref[0] = cos_t.T
    sin_ref[0] = sin_t.T


def _rope_tables(positions, tm):
    B, T = positions.shape
    half = MLA_ROPE // 2
    inv = ROPE_THETA ** (-jnp.arange(0, MLA_ROPE, 2, dtype=F32) / MLA_ROPE)
    spec = pl.BlockSpec((1, tm, LANES), lambda b, i: (b, i, 0))
    return pl.pallas_call(
        _rope_kernel,
        grid=(B, T // tm),
        in_specs=[
            pl.BlockSpec((1, 1, tm), lambda b, i: (b, 0, i)),
            _const_spec((half, tm)),
        ],
        out_specs=[spec, spec],
        out_shape=[jax.ShapeDtypeStruct((B, T, LANES), F32)] * 2,
        compiler_params=_cparams(("parallel", "parallel")),
        name="rope_tables",
    )(positions.reshape(B, 1, T), jnp.broadcast_to(inv[:, None], (half, tm)))


def _pair_ms(x, m_ref):
    return jnp.dot((x * x).astype(BF16), m_ref[...], preferred_element_type=F32)


def _slab_rope(xs, cos, sin, first_half):
    half = MLA_ROPE // 2
    w = xs.shape[1]
    partner = jnp.where(first_half,
                        pltpu.roll(xs, w - half, 1),
                        pltpu.roll(xs, half, 1))
    return xs * cos + partner * sin


def _store_values(v_ref, tok, vt, heads, width):
    ones = jnp.ones((ONES_ROWS, vt.shape[1]), BF16)
    rows = width + ONES_ROWS
    for hd in range(heads):
        v_ref[0, hd * rows:hd * rows + width, tok] = vt[hd * width:(hd + 1) * width]
        v_ref[0, hd * rows + width:(hd + 1) * rows, tok] = ones


def _row_rms(x, gain):
    ms = jnp.mean(x * x, axis=-1, keepdims=True)
    return x * lax.rsqrt(ms + EPS) * gain


def _proj_kernel(x_ref, mod_ref, g1_ref, win_ref, wuq_ref, wuk_ref, wuv_ref, m2_ref, mna_ref,
                 gqna_ref, gkna_ref, gqa_ref, gkva_ref, gq_ref, gk_ref, gkr_ref, cos_ref, sin_ref,
                 qna_ref, kna_ref, vna_ref, qm_ref, km_ref, vm_ref, u_ref, gb_ref, *, sub):
    for t0 in range(0, x_ref.shape[1], sub):
        _proj_sub_tile(slice(t0, t0 + sub), x_ref, mod_ref, g1_ref, win_ref, wuq_ref, wuk_ref, wuv_ref,
                       m2_ref, mna_ref, gqna_ref, gkna_ref, gqa_ref, gkva_ref, gq_ref, gk_ref, gkr_ref,
                       cos_ref, sin_ref, qna_ref, kna_ref, vna_ref, qm_ref, km_ref, vm_ref, u_ref, gb_ref)


def _proj_sub_tile(tok, x_ref, mod_ref, g1_ref, win_ref, wuq_ref, wuk_ref, wuv_ref, m2_ref, mna_ref,
                   gqna_ref, gkna_ref, gqa_ref, gkva_ref, gq_ref, gk_ref, gkr_ref, cos_ref, sin_ref,
                   qna_ref, kna_ref, vna_ref, qm_ref, km_ref, vm_ref, u_ref, gb_ref):
    x = x_ref[0, tok, :]
    sh = mod_ref[0, 0:1, :]
    sc = mod_ref[0, 1:2, :]
    h = _row_rms(x, g1_ref[...]) * (1.0 + sc) + sh
    hb = h.astype(BF16)

    nq = NA_HEADS * HEAD_DIM
    pair = 2 * LANES
    cw = CONV_WIDTH
    pairs = range(0, MLA_HEADS * LANES, pair)

    pall = jnp.dot(hb, win_ref[...], preferred_element_type=F32)
    o = 0
    pq = pall[:, o:o + nq]; o += nq
    pk = pall[:, o:o + nq]; o += nq
    pv = pall[:, o:o + nq]; o += nq
    cq = pall[:, o:o + MLA_Q_RANK]; o += MLA_Q_RANK
    ckv = pall[:, o:o + MLA_KV_RANK]; o += MLA_KV_RANK
    kr = pall[:, o:o + LANES]; o += LANES
    pc = pall[:, o:o + 3 * cw]

    cq = _row_rms(cq, gqa_ref[...]).astype(BF16)
    ckv = _row_rms(ckv, gkva_ref[...]).astype(BF16)
    q = jnp.dot(cq, wuq_ref[...], preferred_element_type=F32)
    kn = jnp.dot(ckv, wuk_ref[...], preferred_element_type=F32)
    vt = jnp.dot(ckv, wuv_ref[...], preferred_element_type=F32)

    ms_qna = _pair_ms(pq, mna_ref)
    ms_kna = _pair_ms(pk, mna_ref)
    ms_q = [_pair_ms(q[:, s0:s0 + pair], m2_ref) for s0 in pairs]
    ms_k = [_pair_ms(kn[:, s0:s0 + pair], m2_ref) for s0 in pairs]

    qna_ref[0, :, tok] = (pq * lax.rsqrt(ms_qna + EPS) * gqna_ref[...]).T.astype(BF16)
    kna_ref[0, tok, :] = (pk * lax.rsqrt(ms_kna + EPS) * gkna_ref[...]).astype(BF16)
    _store_values(vna_ref, tok, pv.T.astype(BF16), NA_HEADS, HEAD_DIM)

    cos = cos_ref[0, tok, :]
    sin = sin_ref[0, tok, :]
    cos2 = jnp.concatenate([cos, cos], axis=1)
    sin2 = jnp.concatenate([sin, sin], axis=1)
    lane = lax.broadcasted_iota(jnp.int32, (1, pair), 1) % LANES
    first_half = lane < MLA_NOPE + MLA_ROPE // 2

    for s0, ms in zip(pairs, ms_q):
        qs = q[:, s0:s0 + pair] * lax.rsqrt(ms + EPS) * gq_ref[...]
        qm_ref[0, s0:s0 + pair, tok] = _slab_rope(qs, cos2, sin2, first_half).T.astype(BF16)

    kr = kr * lax.rsqrt(jnp.sum(kr * kr, axis=-1, keepdims=True) * (1.0 / MLA_ROPE) + EPS) * gkr_ref[...]
    kr = _slab_rope(kr, cos, sin, first_half[:, :LANES])
    kr2 = jnp.concatenate([kr, kr], axis=1)
    for s0, ms in zip(pairs, ms_k):
        kn_s = kn[:, s0:s0 + pair] * lax.rsqrt(ms + EPS) * gk_ref[...]
        km_ref[0, tok, s0:s0 + pair] = (kn_s + kr2).astype(BF16)
    _store_values(vm_ref, tok, vt.T.astype(BF16), MLA_HEADS, MLA_V)

    u_ref[0, :, tok] = (pc[:, 2 * cw:3 * cw] * pc[:, 0:cw]).T
    gb_ref[0, :, tok] = pc[:, cw:2 * cw].T


def _proj_call(x, mod, cos_t, sin_t, p, tm, sub):
    B, T, D = x.shape
    nq = NA_HEADS * HEAD_DIM
    nm = MLA_HEADS * LANES
    tok = lambda w: pl.BlockSpec((1, tm, w), lambda b, i: (b, i, 0))
    chan = lambda c: pl.BlockSpec((1, c, tm), lambda b, i: (b, 0, i))
    consts = [p["g1"], p["w_in"], p["w_uq"], p["w_uk"], p["w_uv"], p["m2"], p["m_na"], p["g_qna"],
              p["g_kna"], p["g_qa"], p["g_kva"], p["g_q"], p["g_k"], p["g_kr"]]
    in_specs = ([tok(D), pl.BlockSpec((1, 6, D), lambda b, i: (b, 0, 0))]
                + [_const_spec(a.shape) for a in consts] + [tok(LANES), tok(LANES)])
    out_shape = [
        jax.ShapeDtypeStruct((B, nq, T), BF16), jax.ShapeDtypeStruct((B, T, nq), BF16),
        jax.ShapeDtypeStruct((B, NA_HEADS * NA_VROWS, T), BF16),
        jax.ShapeDtypeStruct((B, nm, T), BF16), jax.ShapeDtypeStruct((B, T, nm), BF16),
        jax.ShapeDtypeStruct((B, MLA_HEADS * MLA_VROWS, T), BF16),
        jax.ShapeDtypeStruct((B, CONV_WIDTH, T), F32), jax.ShapeDtypeStruct((B, CONV_WIDTH, T), F32),
    ]
    out_specs = [chan(nq), tok(nq), chan(NA_HEADS * NA_VROWS), chan(nm), tok(nm),
                 chan(MLA_HEADS * MLA_VROWS), chan(CONV_WIDTH), chan(CONV_WIDTH)]
    return pl.pallas_call(
        functools.partial(_proj_kernel, sub=sub),
        grid=(B, T // tm),
        in_specs=in_specs,
        out_specs=out_specs,
        out_shape=out_shape,
        compiler_params=_cparams(("parallel", "parallel")),
        name="in_proj",
    )(x, mod, *consts, cos_t, sin_t)


def _na_variant(sb, nsb):
    return jnp.where(sb == 0, 0, jnp.where(sb == nsb - 1, 2, 1))


def _na_kernel(q_ref, k_ref, v_ref, *refs, rows):
    nsub = NA_Q_ROWS // NA_SUB_ROWS
    bias_first, bias_mid, bias_last, o_ref, s_ref, mx_ref = refs
    bias_refs = [bias_first] + [bias_mid] * (nsub - 2) + [bias_last]
    rb = pl.program_id(1)
    sw = NA_SUB_ROWS * GRID_W
    nk = NA_K_ROWS * GRID_W
    kh = nk // 2
    kstarts = [pl.multiple_of(jnp.clip(rb * NA_Q_ROWS + c * NA_SUB_ROWS - NA_KR // 2, 0, rows - NA_K_ROWS)
                              * GRID_W, MXU_WIDTH) for c in range(nsub)]
    units = [(h, c) for h in range(NA_HEADS) for c in range(nsub)]
    chan = lax.broadcasted_iota(jnp.int32, (LANES, 1), 0)

    def score(i):
        h, c = units[i]
        slab = slice((h // 2) * LANES, (h // 2 + 1) * LANES)
        q = q_ref[0, slab, c * sw:(c + 1) * sw]
        q = jnp.where(chan >= HEAD_DIM if h % 2 else chan < HEAD_DIM, q, jnp.zeros_like(q))
        mx = None
        for r0 in range(0, nk, kh):
            k = k_ref[0, pl.ds(pl.multiple_of(kstarts[c] + r0, LANES), kh), slab]
            s = jnp.dot(k, q, preferred_element_type=F32) + bias_refs[c][0, 0, h, r0:r0 + kh, :]
            s_ref[i % 3, r0:r0 + kh, :] = s
            part = jnp.max(s, axis=0, keepdims=True)
            mx = part if mx is None else jnp.maximum(mx, part)
        mx_ref[i % 3] = mx

    score(0)
    score(1)
    for i, (h, c) in enumerate(units):
        p = jnp.exp2(s_ref[i % 3] - mx_ref[i % 3]).astype(BF16)
        v = v_ref[0, h * NA_VROWS:(h + 1) * NA_VROWS, pl.ds(kstarts[c], nk)]
        acc = jnp.dot(v, p, preferred_element_type=F32)
        o_ref[0, h * HEAD_DIM:(h + 1) * HEAD_DIM, c * sw:(c + 1) * sw] = (
            acc[:HEAD_DIM] / acc[HEAD_DIM:HEAD_DIM + 1])
        if i + 2 < len(units):
            score(i + 2)


def _na_bias_kernel(e_ref, o_ref, *, rows):
    lane = lax.broadcasted_iota(jnp.int32, (GRID_W, LANES), 1)
    masked = jnp.full((GRID_W, LANES), MASK_VALUE, F32)
    variants = ((0, 0), (NA_SUB_ROWS, NA_SUB_ROWS - NA_KR // 2), (rows - NA_SUB_ROWS, rows - NA_K_ROWS))
    for v, (r0, ks) in enumerate(variants):
        for kr in range(NA_K_ROWS):
            for pair in range(NA_SUB_ROWS // 2):
                blocks = []
                for qr in (2 * pair, 2 * pair + 1):
                    r, k = r0 + qr, ks + kr
                    row_start = min(max(r - NA_KR // 2, 0), rows - NA_KR)
                    ok = row_start <= k < row_start + NA_KR
                    blocks.append(e_ref[0, 0, k - r + NA_KR - 1] if ok else masked)
                o_ref[0, v, 0, kr * GRID_W:(kr + 1) * GRID_W, pair * LANES:(pair + 1) * LANES] = (
                    jnp.where(lane < GRID_W, blocks[0], blocks[1]))


def _na_bias_tables(rpb, rows):
    L, H = rpb.shape[:2]
    cols = np.arange(GRID_W)
    col_start = np.clip(cols - NA_KC // 2, 0, GRID_W - NA_KC)
    col_ok = (cols[None, :] >= col_start[:, None]) & (cols[None, :] < col_start[:, None] + NA_KC)
    dc = np.clip(cols[None, :] - cols[:, None] + NA_KC - 1, 0, 2 * NA_KC - 2)
    sel = np.eye(2 * NA_KC - 1, dtype=np.float32)[dc.T]
    sel = np.concatenate([sel, sel], axis=1)
    ok = np.concatenate([col_ok.T, col_ok.T], axis=1)
    e = jnp.einsum("lhij,wqj->lhiwq", rpb.astype(F32) * float(np.log2(np.e)), jnp.asarray(sel),
                   precision=lax.Precision.HIGHEST)
    e = jnp.where(jnp.asarray(ok), e, MASK_VALUE)
    nk, sw = NA_K_ROWS * GRID_W, NA_SUB_ROWS * GRID_W
    return pl.pallas_call(
        functools.partial(_na_bias_kernel, rows=rows),
        grid=(L, H),
        in_specs=[pl.BlockSpec((1, 1, 2 * NA_KR - 1, GRID_W, LANES), lambda l, h: (l, h, 0, 0, 0))],
        out_specs=pl.BlockSpec((1, 3, 1, nk, sw), lambda l, h: (l, 0, h, 0, 0)),
        out_shape=jax.ShapeDtypeStruct((L, 3, H, nk, sw), F32),
        compiler_params=_cparams(("parallel", "parallel")),
        name="na_bias",
    )(e)


def _na_call(qna, kna, vna_t, bias, layer):
    B, T, _ = kna.shape
    rows = T // GRID_W
    nrb = rows // NA_Q_ROWS
    nsub = NA_Q_ROWS // NA_SUB_ROWS
    nq = NA_Q_ROWS * GRID_W
    sw = NA_SUB_ROWS * GRID_W
    nk = NA_K_ROWS * GRID_W
    assert nsub >= 2
    bias_block = (1, 1, NA_HEADS, nk, sw)
    bias_specs = [
        pl.BlockSpec(bias_block, lambda b, rb: (layer, _na_variant(rb * nsub, nrb * nsub), 0, 0, 0)),
        pl.BlockSpec(bias_block, lambda b, rb: (layer, 1, 0, 0, 0), pipeline_mode=pl.Buffered(1)),
        pl.BlockSpec(bias_block, lambda b, rb: (layer, _na_variant(rb * nsub + nsub - 1, nrb * nsub), 0, 0, 0)),
    ]
    return pl.pallas_call(
        functools.partial(_na_kernel, rows=rows),
        grid=(B, nrb),
        in_specs=[
            pl.BlockSpec((1, NA_HEADS * HEAD_DIM, nq), lambda b, rb: (b, 0, rb)),
            pl.BlockSpec((1, T, NA_HEADS * HEAD_DIM), lambda b, rb: (b, 0, 0)),
            pl.BlockSpec((1, NA_HEADS * NA_VROWS, T), lambda b, rb: (b, 0, 0)),
        ] + bias_specs,
        out_specs=pl.BlockSpec((1, NA_HEADS * HEAD_DIM, nq), lambda b, rb: (b, 0, rb)),
        out_shape=jax.ShapeDtypeStruct((B, NA_HEADS * HEAD_DIM, T), F32),
        scratch_shapes=[pltpu.VMEM((3, nk, sw), F32), pltpu.VMEM((3, 1, sw), F32)],
        compiler_params=_cparams(("parallel", "arbitrary")),
        name="na_attn",
    )(qna, kna, vna_t, bias, bias, bias)


def _mla_kernel(q_ref, k_ref, v_ref, o_ref, s_ref, mx_ref, *, tq, tk, sw, unroll):
    T = k_ref.shape[1]
    nq = T // tq
    nkv = T // tk
    strips = [slice(c * sw, (c + 1) * sw) for c in range(tq // sw)]
    ns = len(strips)
    hk = tk // 2

    def score(slot, qi, j, c):
        k = k_ref[0, pl.ds(pl.multiple_of(j * tk, tk), tk), :]
        q = q_ref[0, :, pl.ds(pl.multiple_of(qi * tq + c * sw, sw), sw)]
        s = jnp.dot(k, q, preferred_element_type=F32)
        s_ref[slot, c] = s
        mx_ref[slot, :, strips[c]] = jnp.max(s, axis=0, keepdims=True)

    def step(slot, j, nxt_qi, nxt_j, m, acc):
        v = v_ref[0, :, pl.ds(pl.multiple_of(j * tk, tk), tk)]
        ms, accs = [], []
        for c in range(min(2, ns)):
            score(1 - slot, nxt_qi, nxt_j, c)
        for c, sl in enumerate(strips):
            m_old = m[:, sl]
            m_new = jnp.maximum(m_old, mx_ref[slot, :, sl])
            p = jnp.exp2(s_ref[slot, c] - m_new).astype(BF16)
            pv = jnp.dot(v[:, :hk], p[:hk], preferred_element_type=F32)
            if c + 2 < ns:
                score(1 - slot, nxt_qi, nxt_j, c + 2)
            pv = pv + jnp.dot(v[:, hk:], p[hk:], preferred_element_type=F32)
            accs.append(jnp.exp2(m_old - m_new) * acc[:, sl] + pv)
            ms.append(m_new)
        return jnp.concatenate(ms, axis=1), jnp.concatenate(accs, axis=1)

    def body(bi, carry):
        m, acc = carry
        t0 = bi * unroll
        qi = t0 // nkv
        j0 = t0 % nkv
        fresh = j0 == 0
        m = jnp.where(fresh, -jnp.inf, m)
        acc = jnp.where(fresh, 0.0, acc)
        for u in range(unroll):
            if u + 1 < unroll:
                nxt_qi, nxt_j = qi, j0 + u + 1
            else:
                nxt_qi = jnp.minimum(qi + (j0 + unroll) // nkv, nq - 1)
                nxt_j = (j0 + unroll) % nkv
            m, acc = step(u % 2, j0 + u, nxt_qi, nxt_j, m, acc)

        @pl.when(j0 + unroll == nkv)
        def _():
            o_ref[0, :, pl.ds(pl.multiple_of(qi * tq, tq), tq)] = acc[:MLA_V] / acc[MLA_V:MLA_V + 1]

        return m, acc

    for c in range(ns):
        score(0, 0, 0, c)
    init = (jnp.full((1, tq), -jnp.inf, F32), jnp.zeros((MLA_VROWS, tq), F32))
    lax.fori_loop(0, nq * nkv // unroll, body, init)


def _mla_call(qm, km, vm_t, tq, tk, sw, unroll):
    B, T, _ = km.shape
    assert unroll % 2 == 0 and (T // tk) % unroll == 0 and T % tq == 0 and tq % sw == 0
    return pl.pallas_call(
        functools.partial(_mla_kernel, tq=tq, tk=tk, sw=sw, unroll=unroll),
        grid=(B, MLA_HEADS),
        in_specs=[
            pl.BlockSpec((1, LANES, T), lambda b, h: (b, h, 0)),
            pl.BlockSpec((1, T, LANES), lambda b, h: (b, 0, h)),
            pl.BlockSpec((1, MLA_VROWS, T), lambda b, h: (b, h, 0)),
        ],
        out_specs=pl.BlockSpec((1, MLA_V, T), lambda b, h: (b, h, 0)),
        out_shape=jax.ShapeDtypeStruct((B, MLA_HEADS * MLA_V, T), F32),
        scratch_shapes=[pltpu.VMEM((2, tq // sw, tk, sw), F32), pltpu.VMEM((2, 1, tq), F32)],
        compiler_params=_cparams(("parallel", "parallel")),
        name="mla_attn",
    )(qm, km, vm_t)


def _group_rms_rows(x):
    c, tm = x.shape
    xg = x.reshape(c // HEAD_DIM, HEAD_DIM, tm)
    ms = jnp.mean(xg * xg, axis=1, keepdims=True)
    return (xg * lax.rsqrt(ms + EPS)).reshape(c, tm)


def _mix_ffn_kernel(x_ref, mod_ref, yna_ref, ym_ref, u_ref, up_ref, un_ref, gb_ref, cw_ref, cb_ref,
                    og_ref, wout_ref, g2_ref, wg_ref, wu_ref, wd_ref, o_ref, act_ref, *, chunk):
    i = pl.program_id(1)
    last = pl.num_programs(1) - 1
    u = u_ref[0]
    tm = u.shape[1]
    prev = jnp.where(i > 0, up_ref[0], 0.0)
    nxt = jnp.where(i < last, un_ref[0], 0.0)
    ext = jnp.concatenate([prev, u, nxt], axis=1)
    w = ext.shape[1]
    u_m1 = pltpu.roll(ext, 1, 1)[:, LANES:LANES + tm]
    u_p1 = pltpu.roll(ext, w - 1, 1)[:, LANES:LANES + tm]
    y = cw_ref[0] * u_m1 + cw_ref[1] * u + cw_ref[2] * u_p1 + cb_ref[...]
    yc = gb_ref[0] * y
    mixed = jnp.concatenate([_group_rms_rows(yna_ref[0]), _group_rms_rows(ym_ref[0]),
                             _group_rms_rows(yc)], axis=0)
    mixed = (mixed.T * og_ref[...]).astype(BF16)
    x = x_ref[0] + mod_ref[0, 2:3, :] * jnp.dot(mixed, wout_ref[...], preferred_element_type=F32)
    sh = mod_ref[0, 3:4, :]
    sc = mod_ref[0, 4:5, :]
    hb = (_row_rms(x, g2_ref[...]) * (1.0 + sc) + sh).astype(BF16)
    dff = wg_ref.shape[1]
    for c in range(dff // chunk):
        sl = slice(c * chunk, (c + 1) * chunk)
        g = jnp.dot(hb, wg_ref[:, sl], preferred_element_type=F32)
        up = jnp.dot(hb, wu_ref[:, sl], preferred_element_type=F32)
        act_ref[:, sl] = (g * jax.nn.sigmoid(g) * up).astype(BF16)
    out = jnp.dot(act_ref[...], wd_ref[...], preferred_element_type=F32)
    o_ref[0] = x + mod_ref[0, 5:6, :] * out


def _mix_ffn_call(x, mod, yna_t, ym_t, u_t, gb_t, p, tm, chunk):
    B, T, D = x.shape
    nb = tm // LANES
    nlb = T // LANES
    dff = p["w_g"].shape[1]
    chan = lambda c: pl.BlockSpec((1, c, tm), lambda b, i: (b, 0, i))
    consts = [p["conv_w"], p["conv_b"], p["out_g"], p["w_out"], p["g2"], p["w_g"], p["w_u"], p["w_d"]]
    return pl.pallas_call(
        functools.partial(_mix_ffn_kernel, chunk=chunk),
        grid=(B, T // tm),
        in_specs=[
            pl.BlockSpec((1, tm, D), lambda b, i: (b, i, 0)),
            pl.BlockSpec((1, 6, D), lambda b, i: (b, 0, 0)),
            chan(NA_HEADS * HEAD_DIM), chan(MLA_HEADS * MLA_V), chan(CONV_WIDTH),
            pl.BlockSpec((1, CONV_WIDTH, LANES), lambda b, i: (b, 0, jnp.maximum(i * nb - 1, 0))),
            pl.BlockSpec((1, CONV_WIDTH, LANES), lambda b, i: (b, 0, jnp.minimum((i + 1) * nb, nlb - 1))),
            chan(CONV_WIDTH),
        ] + [_const_spec(a.shape) for a in consts],
        out_specs=pl.BlockSpec((1, tm, D), lambda b, i: (b, i, 0)),
        out_shape=jax.ShapeDtypeStruct((B, T, D), F32),
        scratch_shapes=[pltpu.VMEM((tm, dff), BF16)],
        compiler_params=_cparams(("parallel", "parallel")),
        name="mix_ffn",
    )(x, mod, yna_t, ym_t, u_t, u_t, u_t, gb_t, *consts)


def _pad_heads(w, heads, width):
    k = w.shape[0]
    w = w.reshape(k, heads, width)
    return jnp.pad(w, ((0, 0), (0, 0), (0, LANES - width))).reshape(k, heads * LANES)


def _lane_row(parts, repeat=1):
    row = jnp.zeros((1, LANES), F32)
    for off, v in parts:
        row = row.at[0, off:off + v.shape[0]].set(v.astype(F32))
    return jnp.tile(row, (1, repeat))


def _norm_matrices():
    mla = np.zeros((LANES, LANES), np.float32)
    mla[:MLA_NOPE, :MLA_NOPE] = 1.0 / MLA_NOPE
    mla[MLA_NOPE:MLA_NOPE + MLA_ROPE, MLA_NOPE:MLA_NOPE + MLA_ROPE] = 1.0 / MLA_ROPE
    na = np.zeros((LANES, LANES), np.float32)
    na[:HEAD_DIM, :HEAD_DIM] = 1.0 / HEAD_DIM
    na[HEAD_DIM:, HEAD_DIM:] = 1.0 / HEAD_DIM
    z = np.zeros((LANES, LANES), np.float32)
    pair = lambda m: jnp.asarray(np.block([[m, z], [z, m]]), BF16)
    return pair(mla), pair(na)


def _layer_params(l, tm, norm1_g, norm2_g, w_in, na_q_g, na_k_g, mla_q_a_g, mla_kv_a_g, mla_w_uq,
                  mla_w_ukv, mla_qn_g, mla_kn_g, mla_qr_g, mla_kr_g, conv_w, conv_b, out_norm_g,
                  w_out, w_gu, w_down):
    naw = NA_HEADS * HEAD_DIM
    i0 = 3 * naw
    i1 = i0 + MLA_Q_RANK
    i2 = i1 + MLA_KV_RANK
    i3 = i2 + MLA_ROPE
    w = w_in[l]
    d = w.shape[0]
    kr_slab = jnp.pad(w[:, i2:i3], ((0, 0), (MLA_NOPE, LANES - MLA_NOPE - MLA_ROPE)))
    w_in_r = jnp.concatenate([w[:, :i2], kr_slab, w[:, i3:]], axis=1).astype(BF16)
    ukv = mla_w_ukv[l].reshape(MLA_KV_RANK, MLA_HEADS, MLA_NOPE + MLA_V)
    w_uk = _pad_heads(ukv[:, :, :MLA_NOPE].reshape(MLA_KV_RANK, -1), MLA_HEADS, MLA_NOPE).astype(BF16)
    w_uv = ukv[:, :, MLA_NOPE:].reshape(MLA_KV_RANK, -1).astype(BF16)
    w_uq = _pad_heads(mla_w_uq[l], MLA_HEADS, MLA_NOPE + MLA_ROPE).astype(BF16)
    na_scale = HEAD_DIM ** -0.5 * float(np.log2(np.e))
    mla_scale = (MLA_NOPE + MLA_ROPE) ** -0.5 * float(np.log2(np.e))
    dff = w_down.shape[1]
    m_mla, m_na = _norm_matrices()
    return {
        "g1": norm1_g[l].reshape(1, d), "g2": norm2_g[l].reshape(1, d),
        "w_in": w_in_r, "w_uq": w_uq, "w_uk": w_uk, "w_uv": w_uv, "m2": m_mla, "m_na": m_na,
        "g_qna": _lane_row([(0, na_q_g[l] * na_scale), (HEAD_DIM, na_q_g[l] * na_scale)], 2),
        "g_kna": _lane_row([(0, na_k_g[l]), (HEAD_DIM, na_k_g[l])], 2),
        "g_qa": mla_q_a_g[l].reshape(1, -1), "g_kva": mla_kv_a_g[l].reshape(1, -1),
        "g_q": _lane_row([(0, mla_qn_g[l] * mla_scale), (MLA_NOPE, mla_qr_g[l] * mla_scale)], 2),
        "g_k": _lane_row([(0, mla_kn_g[l])], 2),
        "g_kr": _lane_row([(MLA_NOPE, mla_kr_g[l])]),
        "conv_w": jnp.broadcast_to(conv_w[l][:, :, None], (3, CONV_WIDTH, tm)),
        "conv_b": jnp.broadcast_to(conv_b[l][:, None], (CONV_WIDTH, tm)),
        "out_g": out_norm_g[l].reshape(1, -1),
        "w_out": w_out[l].astype(BF16),
        "w_g": w_gu[l][:, :dff].astype(BF16),
        "w_u": w_gu[l][:, dff:].astype(BF16),
        "w_d": w_down[l].astype(BF16),
    }


def kernel(x, c, positions, norm1_g, norm2_g, w_ada, b_ada, w_in, na_q_g, na_k_g, na_rpb, mla_q_a_g,
           mla_kv_a_g, mla_w_uq, mla_w_ukv, mla_qn_g, mla_kn_g, mla_qr_g, mla_kr_g, conv_w, conv_b,
           out_norm_g, w_out, w_gu, w_down):
    B, T, D = x.shape
    depth = w_in.shape[0]
    rows = T // GRID_W
    t = _tiles(T)
    assert T % t.mla_q == 0 and T % t.proj == 0 and T % t.mix == 0 and T % t.rope == 0
    assert rows % NA_Q_ROWS == 0 and rows >= NA_K_ROWS + NA_SUB_ROWS
    mod = _ada_modulation(c, w_ada, b_ada, t.ada_cols)
    cos_t, sin_t = _rope_tables(positions, t.rope)
    na_bias = _na_bias_tables(na_rpb, rows)
    for l in range(depth):
        p = _layer_params(l, t.mix, norm1_g, norm2_g, w_in, na_q_g, na_k_g, mla_q_a_g, mla_kv_a_g,
                          mla_w_uq, mla_w_ukv, mla_qn_g, mla_kn_g, mla_qr_g, mla_kr_g, conv_w,
                          conv_b, out_norm_g, w_out, w_gu, w_down)
        qna, kna, vna_t, qm, km, vm_t, u_t, gb_t = _proj_call(x, mod[l], cos_t, sin_t, p, t.proj, t.proj_sub)
        yna_t = _na_call(qna, kna, vna_t, na_bias, l)
        ym_t = _mla_call(qm, km, vm_t, tq=t.mla_q, tk=t.mla_k, sw=t.strip, unroll=t.mla_unroll)
        x = _mix_ffn_call(x, mod[l], yna_t, ym_t, u_t, gb_t, p, t.mix, t.ffn_chunk)
    return x
```

```python
import functools
from typing import NamedTuple

import jax
import jax.numpy as jnp
import numpy as np
from jax import lax
from jax.experimental import pallas as pl
from jax.experimental.pallas import tpu as pltpu

F32 = jnp.float32
BF16 = jnp.bfloat16

GRID_W = 64
HEAD_DIM = 64
NA_HEADS = 4
NA_KR = 8
NA_KC = 16
MLA_HEADS = 8
MLA_NOPE = 64
MLA_ROPE = 32
MLA_V = 64
MLA_Q_RANK = 384
MLA_KV_RANK = 256
CONV_WIDTH = 256
ROPE_THETA = 10000.0
EPS = 1e-6

LANES = 128
SUBLANES = 8
MXU_WIDTH = 256
NA_Q_ROWS = 32
NA_SUB_ROWS = MXU_WIDTH // GRID_W
ONES_ROWS = 64
MLA_VROWS = MLA_V + ONES_ROWS
NA_VROWS = HEAD_DIM + ONES_ROWS
NA_K_ROWS = 12
MASK_VALUE = -1e30
VMEM_LIMIT = 56 * 1024 * 1024


class _Tiles(NamedTuple):
    proj: int
    proj_sub: int
    mix: int
    mla_q: int
    mla_k: int
    strip: int
    mla_unroll: int
    ffn_chunk: int
    rope: int
    ada_cols: int


def _tiles(T):
    mla_k = 512
    return _Tiles(proj=min(1024, T), proj_sub=MXU_WIDTH, mix=512, mla_q=1024, mla_k=mla_k, strip=MXU_WIDTH,
                  mla_unroll=min(16, T // mla_k), ffn_chunk=MXU_WIDTH, rope=min(2048, T), ada_cols=1536)


def _cparams(sem):
    return pltpu.CompilerParams(dimension_semantics=sem, vmem_limit_bytes=VMEM_LIMIT)


def _const_spec(shape):
    nd = len(shape)
    return pl.BlockSpec(shape, lambda *_: (0,) * nd, pipeline_mode=pl.Buffered(1))


def _split_bf16(x):
    hi = x.astype(BF16)
    lo = (x - hi.astype(F32)).astype(BF16)
    return hi, lo


def _ada_kernel(c_ref, w_ref, b_ref, o_ref):
    c = c_ref[...]
    a = c * jax.nn.sigmoid(c)
    a_hi, a_lo = _split_bf16(a)
    w_hi, w_lo = _split_bf16(w_ref[0])
    acc = jnp.dot(a_hi, w_hi, preferred_element_type=F32)
    acc += jnp.dot(a_lo, w_hi, preferred_element_type=F32)
    acc += jnp.dot(a_hi, w_lo, preferred_element_type=F32)
    o_ref[0] = acc + b_ref[0]


def _ada_modulation(c, w_ada, b_ada, tn):
    L, D, N = w_ada.shape
    B = c.shape[0]
    rows = SUBLANES * pl.cdiv(B, SUBLANES)
    c_pad = jnp.zeros((rows, D), F32).at[:B].set(c)
    out = pl.pallas_call(
        _ada_kernel,
        grid=(L, N // tn),
        in_specs=[
            pl.BlockSpec((rows, D), lambda l, j: (0, 0)),
            pl.BlockSpec((1, D, tn), lambda l, j: (l, 0, j)),
            pl.BlockSpec((1, 1, tn), lambda l, j: (l, 0, j)),
        ],
        out_specs=pl.BlockSpec((1, rows, tn), lambda l, j: (l, 0, j)),
        out_shape=jax.ShapeDtypeStruct((L, rows, N), F32),
        compiler_params=_cparams(("parallel", "parallel")),
        name="ada_mod",
    )(c_pad, w_ada, b_ada.reshape(L, 1, N))
    return out[:, :B].reshape(L, B, 6, D)


def _rope_kernel(pos_ref, inv_ref, cos_ref, sin_ref):
    ang = pos_ref[0].astype(F32) * inv_ref[...]
    c = jnp.cos(ang)
    s = jnp.sin(ang)
    tm = ang.shape[1]
    pad = LANES - MLA_NOPE - MLA_ROPE
    cos_t = jnp.concatenate([jnp.ones((MLA_NOPE, tm), F32), c, c, jnp.ones((pad, tm), F32)], axis=0)
    sin_t = jnp.concatenate([jnp.zeros((MLA_NOPE, tm), F32), -s, s, jnp.zeros((pad, tm), F32)], axis=0)
    cos_ref[0] = cos_t.T
    sin_ref[0] = sin_t.T


def _rope_tables(positions, tm):
    B, T = positions.shape
    half = MLA_ROPE // 2
    inv = ROPE_THETA ** (-jnp.arange(0, MLA_ROPE, 2, dtype=F32) / MLA_ROPE)
    spec = pl.BlockSpec((1, tm, LANES), lambda b, i: (b, i, 0))
    return pl.pallas_call(
        _rope_kernel,
        grid=(B, T // tm),
        in_specs=[
            pl.BlockSpec((1, 1, tm), lambda b, i: (b, 0, i)),
            _const_spec((half, tm)),
        ],
        out_specs=[spec, spec],
        out_shape=[jax.ShapeDtypeStruct((B, T, LANES), F32)] * 2,
        compiler_params=_cparams(("parallel", "parallel")),
        name="rope_tables",
    )(positions.reshape(B, 1, T), jnp.broadcast_to(inv[:, None], (half, tm)))


def _pair_ms(x, m_ref):
    return jnp.dot((x * x).astype(BF16), m_ref[...], preferred_element_type=F32)


def _slab_rope(xs, cos, sin, first_half):
    half = MLA_ROPE // 2
    w = xs.shape[1]
    partner = jnp.where(first_half,
                        pltpu.roll(xs, w - half, 1),
                        pltpu.roll(xs, half, 1))
    return xs * cos + partner * sin


def _store_values(v_ref, tok, vt, heads, width):
    ones = jnp.ones((ONES_ROWS, vt.shape[1]), BF16)
    rows = width + ONES_ROWS
    for hd in range(heads):
        v_ref[0, hd * rows:hd * rows + width, tok] = vt[hd * width:(hd + 1) * width]
        v_ref[0, hd * rows + width:(hd + 1) * rows, tok] = ones


def _row_rms(x, gain):
    ms = jnp.mean(x * x, axis=-1, keepdims=True)
    return x * lax.rsqrt(ms + EPS) * gain


def _proj_kernel(x_ref, mod_ref, g1_ref, win_ref, wuq_ref, wuk_ref, wuv_ref, m2_ref, mna_ref,
                 gqna_ref, gkna_ref, gqa_ref, gkva_ref, gq_ref, gk_ref, gkr_ref, cos_ref, sin_ref,
                 qna_ref, kna_ref, vna_ref, qm_ref, km_ref, vm_ref, u_ref, gb_ref, *, sub):
    for t0 in range(0, x_ref.shape[1], sub):
        _proj_sub_tile(slice(t0, t0 + sub), x_ref, mod_ref, g1_ref, win_ref, wuq_ref, wuk_ref, wuv_ref,
                       m2_ref, mna_ref, gqna_ref, gkna_ref, gqa_ref, gkva_ref, gq_ref, gk_ref, gkr_ref,
                       cos_ref, sin_ref, qna_ref, kna_ref, vna_ref, qm_ref, km_ref, vm_ref, u_ref, gb_ref)


def _proj_sub_tile(tok, x_ref, mod_ref, g1_ref, win_ref, wuq_ref, wuk_ref, wuv_ref, m2_ref, mna_ref,
                   gqna_ref, gkna_ref, gqa_ref, gkva_ref, gq_ref, gk_ref, gkr_ref, cos_ref, sin_ref,
                   qna_ref, kna_ref, vna_ref, qm_ref, km_ref, vm_ref, u_ref, gb_ref):
    x = x_ref[0, tok, :]
    sh = mod_ref[0, 0:1, :]
    sc = mod_ref[0, 1:2, :]
    h = _row_rms(x, g1_ref[...]) * (1.0 + sc) + sh
    hb = h.astype(BF16)

    nq = NA_HEADS * HEAD_DIM
    pair = 2 * LANES
    cw = CONV_WIDTH
    pairs = range(0, MLA_HEADS * LANES, pair)

    pall = jnp.dot(hb, win_ref[...], preferred_element_type=F32)
    o = 0
    pq = pall[:, o:o + nq]; o += nq
    pk = pall[:, o:o + nq]; o += nq
    pv = pall[:, o:o + nq]; o += nq
    cq = pall[:, o:o + MLA_Q_RANK]; o += MLA_Q_RANK
    ckv = pall[:, o:o + MLA_KV_RANK]; o += MLA_KV_RANK
    kr = pall[:, o:o + LANES]; o += LANES
    pc = pall[:, o:o + 3 * cw]

    cq = _row_rms(cq, gqa_ref[...]).astype(BF16)
    ckv = _row_rms(ckv, gkva_ref[...]).astype(BF16)
    q = jnp.dot(cq, wuq_ref[...], preferred_element_type=F32)
    kn = jnp.dot(ckv, wuk_ref[...], preferred_element_type=F32)
    vt = jnp.dot(ckv, wuv_ref[...], preferred_element_type=F32)

    ms_qna = _pair_ms(pq, mna_ref)
    ms_kna = _pair_ms(pk, mna_ref)
    ms_q = [_pair_ms(q[:, s0:s0 + pair], m2_ref) for s0 in pairs]
    ms_k = [_pair_ms(kn[:, s0:s0 + pair], m2_ref) for s0 in pairs]

    qna_ref[0, :, tok] = (pq * lax.rsqrt(ms_qna + EPS) * gqna_ref[...]).T.astype(BF16)
    kna_ref[0, tok, :] = (pk * lax.rsqrt(ms_kna + EPS) * gkna_ref[...]).astype(BF16)
    _store_values(vna_ref, tok, pv.T.astype(BF16), NA_HEADS, HEAD_DIM)

    cos = cos_ref[0, tok, :]
    sin = sin_ref[0, tok, :]
    cos2 = jnp.concatenate([cos, cos], axis=1)
    sin2 = jnp.concatenate([sin, sin], axis=1)
    lane = lax.broadcasted_iota(jnp.int32, (1, pair), 1) % LANES
    first_half = lane < MLA_NOPE + MLA_ROPE // 2

    for s0, ms in zip(pairs, ms_q):
        qs = q[:, s0:s0 + pair] * lax.rsqrt(ms + EPS) * gq_ref[...]
        qm_ref[0, s0:s0 + pair, tok] = _slab_rope(qs, cos2, sin2, first_half).T.astype(BF16)

    kr = kr * lax.rsqrt(jnp.sum(kr * kr, axis=-1, keepdims=True) * (1.0 / MLA_ROPE) + EPS) * gkr_ref[...]
    kr = _slab_rope(kr, cos, sin, first_half[:, :LANES])
    kr2 = jnp.concatenate([kr, kr], axis=1)
    for s0, ms in zip(pairs, ms_k):
        kn_s = kn[:, s0:s0 + pair] * lax.rsqrt(ms + EPS) * gk_ref[...]
        km_ref[0, tok, s0:s0 + pair] = (kn_s + kr2).astype(BF16)
    _store_values(vm_ref, tok, vt.T.astype(BF16), MLA_HEADS, MLA_V)

    u_ref[0, :, tok] = (pc[:, 2 * cw:3 * cw] * pc[:, 0:cw]).T
    gb_ref[0, :, tok] = pc[:, cw:2 * cw].T


def _proj_call(x, mod, cos_t, sin_t, p, tm, sub):
    B, T, D = x.shape
    nq = NA_HEADS * HEAD_DIM
    nm = MLA_HEADS * LANES
    tok = lambda w: pl.BlockSpec((1, tm, w), lambda b, i: (b, i, 0))
    chan = lambda c: pl.BlockSpec((1, c, tm), lambda b, i: (b, 0, i))
    consts = [p["g1"], p["w_in"], p["w_uq"], p["w_uk"], p["w_uv"], p["m2"], p["m_na"], p["g_qna"],
              p["g_kna"], p["g_qa"], p["g_kva"], p["g_q"], p["g_k"], p["g_kr"]]
    in_specs = ([tok(D), pl.BlockSpec((1, 6, D), lambda b, i: (b, 0, 0))]
                + [_const_spec(a.shape) for a in consts] + [tok(LANES), tok(LANES)])
    out_shape = [
        jax.ShapeDtypeStruct((B, nq, T), BF16), jax.ShapeDtypeStruct((B, T, nq), BF16),
        jax.ShapeDtypeStruct((B, NA_HEADS * NA_VROWS, T), BF16),
        jax.ShapeDtypeStruct((B, nm, T), BF16), jax.ShapeDtypeStruct((B, T, nm), BF16),
        jax.ShapeDtypeStruct((B, MLA_HEADS * MLA_VROWS, T), BF16),
        jax.ShapeDtypeStruct((B, CONV_WIDTH, T), F32), jax.ShapeDtypeStruct((B, CONV_WIDTH, T), F32),
    ]
    out_specs = [chan(nq), tok(nq), chan(NA_HEADS * NA_VROWS), chan(nm), tok(nm),
                 chan(MLA_HEADS * MLA_VROWS), chan(CONV_WIDTH), chan(CONV_WIDTH)]
    return pl.pallas_call(
        functools.partial(_proj_kernel, sub=sub),
        grid=(B, T // tm),
        in_specs=in_specs,
        out_specs=out_specs,
        out_shape=out_shape,
        compiler_params=_cparams(("parallel", "parallel")),
        name="in_proj",
    )(x, mod, *consts, cos_t, sin_t)


def _na_variant(sb, nsb):
    return jnp.where(sb == 0, 0, jnp.where(sb == nsb - 1, 2, 1))


def _na_kernel(q_ref, k_ref, v_ref, *refs, rows):
    nsub = NA_Q_ROWS // NA_SUB_ROWS
    bias_first, bias_mid, bias_last, o_ref, s_ref, mx_ref = refs
    bias_refs = [bias_first] + [bias_mid] * (nsub - 2) + [bias_last]
    rb = pl.program_id(1)
    sw = NA_SUB_ROWS * GRID_W
    nk = NA_K_ROWS * GRID_W
    kh = nk // 2
    kstarts = [pl.multiple_of(jnp.clip(rb * NA_Q_ROWS + c * NA_SUB_ROWS - NA_KR // 2, 0, rows - NA_K_ROWS)
                              * GRID_W, MXU_WIDTH) for c in range(nsub)]
    units = [(h, c) for h in range(NA_HEADS) for c in range(nsub)]
    chan = lax.broadcasted_iota(jnp.int32, (LANES, 1), 0)

    def score(i):
        h, c = units[i]
        slab = slice((h // 2) * LANES, (h // 2 + 1) * LANES)
        q = q_ref[0, slab, c * sw:(c + 1) * sw]
        q = jnp.where(chan >= HEAD_DIM if h % 2 else chan < HEAD_DIM, q, jnp.zeros_like(q))
        mx = None
        for r0 in range(0, nk, kh):
            k = k_ref[0, pl.ds(pl.multiple_of(kstarts[c] + r0, LANES), kh), slab]
            s = jnp.dot(k, q, preferred_element_type=F32) + bias_refs[c][0, 0, h, r0:r0 + kh, :]
            s_ref[i % 3, r0:r0 + kh, :] = s
            part = jnp.max(s, axis=0, keepdims=True)
            mx = part if mx is None else jnp.maximum(mx, part)
        mx_ref[i % 3] = mx

    score(0)
    score(1)
    for i, (h, c) in enumerate(units):
        p = jnp.exp2(s_ref[i % 3] - mx_ref[i % 3]).astype(BF16)
        v = v_ref[0, h * NA_VROWS:(h + 1) * NA_VROWS, pl.ds(kstarts[c], nk)]
        acc = jnp.dot(v, p, preferred_element_type=F32)
        o_ref[0, h * HEAD_DIM:(h + 1) * HEAD_DIM, c * sw:(c + 1) * sw] = (
            acc[:HEAD_DIM] / acc[HEAD_DIM:HEAD_DIM + 1])
        if i + 2 < len(units):
            score(i + 2)


def _na_bias_kernel(e_ref, o_ref, *, rows):
    lane = lax.broadcasted_iota(jnp.int32, (GRID_W, LANES), 1)
    masked = jnp.full((GRID_W, LANES), MASK_VALUE, F32)
    variants = ((0, 0), (NA_SUB_ROWS, NA_SUB_ROWS - NA_KR // 2), (rows - NA_SUB_ROWS, rows - NA_K_ROWS))
    for v, (r0, ks) in enumerate(variants):
        for kr in range(NA_K_ROWS):
            for pair in range(NA_SUB_ROWS // 2):
                blocks = []
                for qr in (2 * pair, 2 * pair + 1):
                    r, k = r0 + qr, ks + kr
                    row_start = min(max(r - NA_KR // 2, 0), rows - NA_KR)
                    ok = row_start <= k < row_start + NA_KR
                    blocks.append(e_ref[0, 0, k - r + NA_KR - 1] if ok else masked)
                o_ref[0, v, 0, kr * GRID_W:(kr + 1) * GRID_W, pair * LANES:(pair + 1) * LANES] = (
                    jnp.where(lane < GRID_W, blocks[0], blocks[1]))


def _na_bias_tables(rpb, rows):
    L, H = rpb.shape[:2]
    cols = np.arange(GRID_W)
    col_start = np.clip(cols - NA_KC // 2, 0, GRID_W - NA_KC)
    col_ok = (cols[None, :] >= col_start[:, None]) & (cols[None, :] < col_start[:, None] + NA_KC)
    dc = np.clip(cols[None, :] - cols[:, None] + NA_KC - 1, 0, 2 * NA_KC - 2)
    sel = np.eye(2 * NA_KC - 1, dtype=np.float32)[dc.T]
    sel = np.concatenate([sel, sel], axis=1)
    ok = np.concatenate([col_ok.T, col_ok.T], axis=1)
    e = jnp.einsum("lhij,wqj->lhiwq", rpb.astype(F32) * float(np.log2(np.e)), jnp.asarray(sel),
                   precision=lax.Precision.HIGHEST)
    e = jnp.where(jnp.asarray(ok), e, MASK_VALUE)
    nk, sw = NA_K_ROWS * GRID_W, NA_SUB_ROWS * GRID_W
    return pl.pallas_call(
        functools.partial(_na_bias_kernel, rows=rows),
        grid=(L, H),
        in_specs=[pl.BlockSpec((1, 1, 2 * NA_KR - 1, GRID_W, LANES), lambda l, h: (l, h, 0, 0, 0))],
        out_specs=pl.BlockSpec((1, 3, 1, nk, sw), lambda l, h: (l, 0, h, 0, 0)),
        out_shape=jax.ShapeDtypeStruct((L, 3, H, nk, sw), F32),
        compiler_params=_cparams(("parallel", "parallel")),
        name="na_bias",
    )(e)


def _na_call(qna, kna, vna_t, bias, layer):
    B, T, _ = kna.shape
    rows = T // GRID_W
    nrb = rows // NA_Q_ROWS
    nsub = NA_Q_ROWS // NA_SUB_ROWS
    nq = NA_Q_ROWS * GRID_W
    sw = NA_SUB_ROWS * GRID_W
    nk = NA_K_ROWS * GRID_W
    assert nsub >= 2
    bias_block = (1, 1, NA_HEADS, nk, sw)
    bias_specs = [
        pl.BlockSpec(bias_block, lambda b, rb: (layer, _na_variant(rb * nsub, nrb * nsub), 0, 0, 0)),
        pl.BlockSpec(bias_block, lambda b, rb: (layer, 1, 0, 0, 0), pipeline_mode=pl.Buffered(1)),
        pl.BlockSpec(bias_block, lambda b, rb: (layer, _na_variant(rb * nsub + nsub - 1, nrb * nsub), 0, 0, 0)),
    ]
    return pl.pallas_call(
        functools.partial(_na_kernel, rows=rows),
        grid=(B, nrb),
        in_specs=[
            pl.BlockSpec((1, NA_HEADS * HEAD_DIM, nq), lambda b, rb: (b, 0, rb)),
            pl.BlockSpec((1, T, NA_HEADS * HEAD_DIM), lambda b, rb: (b, 0, 0)),
            pl.BlockSpec((1, NA_HEADS * NA_VROWS, T), lambda b, rb: (b, 0, 0)),
        ] + bias_specs,
        out_specs=pl.BlockSpec((1, NA_HEADS * HEAD_DIM, nq), lambda b, rb: (b, 0, rb)),
        out_shape=jax.ShapeDtypeStruct((B, NA_HEADS * HEAD_DIM, T), F32),
        scratch_shapes=[pltpu.VMEM((3, nk, sw), F32), pltpu.VMEM((3, 1, sw), F32)],
        compiler_params=_cparams(("parallel", "arbitrary")),
        name="na_attn",
    )(qna, kna, vna_t, bias, bias, bias)


def _mla_kernel(q_ref, k_ref, v_ref, o_ref, s_ref, mx_ref, *, tq, tk, sw, unroll):
    T = k_ref.shape[1]
    nq = T // tq
    nkv = T // tk
    strips = [slice(c * sw, (c + 1) * sw) for c in range(tq // sw)]
    ns = len(strips)
    hk = tk // 2

    def score(slot, qi, j, c):
        k = k_ref[0, pl.ds(pl.multiple_of(j * tk, tk), tk), :]
        q = q_ref[0, :, pl.ds(pl.multiple_of(qi * tq + c * sw, sw), sw)]
        s = jnp.dot(k, q, preferred_element_type=F32)
        s_ref[slot, c] = s
        mx_ref[slot, :, strips[c]] = jnp.max(s, axis=0, keepdims=True)

    def step(slot, j, nxt_qi, nxt_j, m, acc):
        v = v_ref[0, :, pl.ds(pl.multiple_of(j * tk, tk), tk)]
        ms, accs = [], []
        for c in range(min(2, ns)):
            score(1 - slot, nxt_qi, nxt_j, c)
        for c, sl in enumerate(strips):
            m_old = m[:, sl]
            m_new = jnp.maximum(m_old, mx_ref[slot, :, sl])
            p = jnp.exp2(s_ref[slot, c] - m_new).astype(BF16)
            pv = jnp.dot(v[:, :hk], p[:hk], preferred_element_type=F32)
            if c + 2 < ns:
                score(1 - slot, nxt_qi, nxt_j, c + 2)
            pv = pv + jnp.dot(v[:, hk:], p[hk:], preferred_element_type=F32)
            accs.append(jnp.exp2(m_old - m_new) * acc[:, sl] + pv)
            ms.append(m_new)
        return jnp.concatenate(ms, axis=1), jnp.concatenate(accs, axis=1)

    def body(bi, carry):
        m, acc = carry
        t0 = bi * unroll
        qi = t0 // nkv
        j0 = t0 % nkv
        fresh = j0 == 0
        m = jnp.where(fresh, -jnp.inf, m)
        acc = jnp.where(fresh, 0.0, acc)
        for u in range(unroll):
            if u + 1 < unroll:
                nxt_qi, nxt_j = qi, j0 + u + 1
            else:
                nxt_qi = jnp.minimum(qi + (j0 + unroll) // nkv, nq - 1)
                nxt_j = (j0 + unroll) % nkv
            m, acc = step(u % 2, j0 + u, nxt_qi, nxt_j, m, acc)

        @pl.when(j0 + unroll == nkv)
        def _():
            o_ref[0, :, pl.ds(pl.multiple_of(qi * tq, tq), tq)] = acc[:MLA_V] / acc[MLA_V:MLA_V + 1]

        return m, acc

    for c in range(ns):
        score(0, 0, 0, c)
    init = (jnp.full((1, tq), -jnp.inf, F32), jnp.zeros((MLA_VROWS, tq), F32))
    lax.fori_loop(0, nq * nkv // unroll, body, init)


def _mla_call(qm, km, vm_t, tq, tk, sw, unroll):
    B, T, _ = km.shape
    assert unroll % 2 == 0 and (T // tk) % unroll == 0 and T % tq == 0 and tq % sw == 0
    return pl.pallas_call(
        functools.partial(_mla_kernel, tq=tq, tk=tk, sw=sw, unroll=unroll),
        grid=(B, MLA_HEADS),
        in_specs=[
            pl.BlockSpec((1, LANES, T), lambda b, h: (b, h, 0)),
            pl.BlockSpec((1, T, LANES), lambda b, h: (b, 0, h)),
            pl.BlockSpec((1, MLA_VROWS, T), lambda b, h: (b, h, 0)),
        ],
        out_specs=pl.BlockSpec((1, MLA_V, T), lambda b, h: (b, h, 0)),
        out_shape=jax.ShapeDtypeStruct((B, MLA_HEADS * MLA_V, T), F32),
        scratch_shapes=[pltpu.VMEM((2, tq // sw, tk, sw), F32), pltpu.VMEM((2, 1, tq), F32)],
        compiler_params=_cparams(("parallel", "parallel")),
        name="mla_attn",
    )(qm, km, vm_t)


def _group_rms_rows(x):
    c, tm = x.shape
    xg = x.reshape(c // HEAD_DIM, HEAD_DIM, tm)
    ms = jnp.mean(xg * xg, axis=1, keepdims=True)
    return (xg * lax.rsqrt(ms + EPS)).reshape(c, tm)


def _mix_ffn_kernel(x_ref, mod_ref, yna_ref, ym_ref, u_ref, up_ref, un_ref, gb_ref, cw_ref, cb_ref,
                    og_ref, wout_ref, g2_ref, wg_ref, wu_ref, wd_ref, o_ref, act_ref, *, chunk):
    i = pl.program_id(1)
    last = pl.num_programs(1) - 1
    u = u_ref[0]
    tm = u.shape[1]
    prev = jnp.where(i > 0, up_ref[0], 0.0)
    nxt = jnp.where(i < last, un_ref[0], 0.0)
    ext = jnp.concatenate([prev, u, nxt], axis=1)
    w = ext.shape[1]
    u_m1 = pltpu.roll(ext, 1, 1)[:, LANES:LANES + tm]
    u_p1 = pltpu.roll(ext, w - 1, 1)[:, LANES:LANES + tm]
    y = cw_ref[0] * u_m1 + cw_ref[1] * u + cw_ref[2] * u_p1 + cb_ref[...]
    yc = gb_ref[0] * y
    mixed = jnp.concatenate([_group_rms_rows(yna_ref[0]), _group_rms_rows(ym_ref[0]),
                             _group_rms_rows(yc)], axis=0)
    mixed = (mixed.T * og_ref[...]).astype(BF16)
    x = x_ref[0] + mod_ref[0, 2:3, :] * jnp.dot(mixed, wout_ref[...], preferred_element_type=F32)
    sh = mod_ref[0, 3:4, :]
    sc = mod_ref[0, 4:5, :]
    hb = (_row_rms(x, g2_ref[...]) * (1.0 + sc) + sh).astype(BF16)
    dff = wg_ref.shape[1]
    for c in range(dff // chunk):
        sl = slice(c * chunk, (c + 1) * chunk)
        g = jnp.dot(hb, wg_ref[:, sl], preferred_element_type=F32)
        up = jnp.dot(hb, wu_ref[:, sl], preferred_element_type=F32)
        act_ref[:, sl] = (g * jax.nn.sigmoid(g) * up).astype(BF16)
    out = jnp.dot(act_ref[...], wd_ref[...], preferred_element_type=F32)
    o_ref[0] = x + mod_ref[0, 5:6, :] * out


def _mix_ffn_call(x, mod, yna_t, ym_t, u_t, gb_t, p, tm, chunk):
    B, T, D = x.shape
    nb = tm // LANES
    nlb = T // LANES
    dff = p["w_g"].shape[1]
    chan = lambda c: pl.BlockSpec((1, c, tm), lambda b, i: (b, 0, i))
    consts = [p["conv_w"], p["conv_b"], p["out_g"], p["w_out"], p["g2"], p["w_g"], p["w_u"], p["w_d"]]
    return pl.pallas_call(
        functools.partial(_mix_ffn_kernel, chunk=chunk),
        grid=(B, T // tm),
        in_specs=[
            pl.BlockSpec((1, tm, D), lambda b, i: (b, i, 0)),
            pl.BlockSpec((1, 6, D), lambda b, i: (b, 0, 0)),
            chan(NA_HEADS * HEAD_DIM), chan(MLA_HEADS * MLA_V), chan(CONV_WIDTH),
            pl.BlockSpec((1, CONV_WIDTH, LANES), lambda b, i: (b, 0, jnp.maximum(i * nb - 1, 0))),
            pl.BlockSpec((1, CONV_WIDTH, LANES), lambda b, i: (b, 0, jnp.minimum((i + 1) * nb, nlb - 1))),
            chan(CONV_WIDTH),
        ] + [_const_spec(a.shape) for a in consts],
        out_specs=pl.BlockSpec((1, tm, D), lambda b, i: (b, i, 0)),
        out_shape=jax.ShapeDtypeStruct((B, T, D), F32),
        scratch_shapes=[pltpu.VMEM((tm, dff), BF16)],
        compiler_params=_cparams(("parallel", "parallel")),
        name="mix_ffn",
    )(x, mod, yna_t, ym_t, u_t, u_t, u_t, gb_t, *consts)


def _pad_heads(w, heads, width):
    k = w.shape[0]
    w = w.reshape(k, heads, width)
    return jnp.pad(w, ((0, 0), (0, 0), (0, LANES - width))).reshape(k, heads * LANES)


def _lane_row(parts, repeat=1):
    row = jnp.zeros((1, LANES), F32)
    for off, v in parts:
        row = row.at[0, off:off + v.shape[0]].set(v.astype(F32))
    return jnp.tile(row, (1, repeat))


def _norm_matrices():
    mla = np.zeros((LANES, LANES), np.float32)
    mla[:MLA_NOPE, :MLA_NOPE] = 1.0 / MLA_NOPE
    mla[MLA_NOPE:MLA_NOPE + MLA_ROPE, MLA_NOPE:MLA_NOPE + MLA_ROPE] = 1.0 / MLA_ROPE
    na = np.zeros((LANES, LANES), np.float32)
    na[:HEAD_DIM, :HEAD_DIM] = 1.0 / HEAD_DIM
    na[HEAD_DIM:, HEAD_DIM:] = 1.0 / HEAD_DIM
    z = np.zeros((LANES, LANES), np.float32)
    pair = lambda m: jnp.asarray(np.block([[m, z], [z, m]]), BF16)
    return pair(mla), pair(na)


def _layer_params(l, tm, norm1_g, norm2_g, w_in, na_q_g, na_k_g, mla_q_a_g, mla_kv_a_g, mla_w_uq,
                  mla_w_ukv, mla_qn_g, mla_kn_g, mla_qr_g, mla_kr_g, conv_w, conv_b, out_norm_g,
                  w_out, w_gu, w_down):
    naw = NA_HEADS * HEAD_DIM
    i0 = 3 * naw
    i1 = i0 + MLA_Q_RANK
    i2 = i1 + MLA_KV_RANK
    i3 = i2 + MLA_ROPE
    w = w_in[l]
    d = w.shape[0]
    kr_slab = jnp.pad(w[:, i2:i3], ((0, 0), (MLA_NOPE, LANES - MLA_NOPE - MLA_ROPE)))
    w_in_r = jnp.concatenate([w[:, :i2], kr_slab, w[:, i3:]], axis=1).astype(BF16)
    ukv = mla_w_ukv[l].reshape(MLA_KV_RANK, MLA_HEADS, MLA_NOPE + MLA_V)
    w_uk = _pad_heads(ukv[:, :, :MLA_NOPE].reshape(MLA_KV_RANK, -1), MLA_HEADS, MLA_NOPE).astype(BF16)
    w_uv = ukv[:, :, MLA_NOPE:].reshape(MLA_KV_RANK, -1).astype(BF16)
    w_uq = _pad_heads(mla_w_uq[l], MLA_HEADS, MLA_NOPE + MLA_ROPE).astype(BF16)
    na_scale = HEAD_DIM ** -0.5 * float(np.log2(np.e))
    mla_scale = (MLA_NOPE + MLA_ROPE) ** -0.5 * float(np.log2(np.e))
    dff = w_down.shape[1]
    m_mla, m_na = _norm_matrices()
    return {
        "g1": norm1_g[l].reshape(1, d), "g2": norm2_g[l].reshape(1, d),
        "w_in": w_in_r, "w_uq": w_uq, "w_uk": w_uk, "w_uv": w_uv, "m2": m_mla, "m_na": m_na,
        "g_qna": _lane_row([(0, na_q_g[l] * na_scale), (HEAD_DIM, na_q_g[l] * na_scale)], 2),
        "g_kna": _lane_row([(0, na_k_g[l]), (HEAD_DIM, na_k_g[l])], 2),
        "g_qa": mla_q_a_g[l].reshape(1, -1), "g_kva": mla_kv_a_g[l].reshape(1, -1),
        "g_q": _lane_row([(0, mla_qn_g[l] * mla_scale), (MLA_NOPE, mla_qr_g[l] * mla_scale)], 2),
        "g_k": _lane_row([(0, mla_kn_g[l])], 2),
        "g_kr": _lane_row([(MLA_NOPE, mla_kr_g[l])]),
        "conv_w": jnp.broadcast_to(conv_w[l][:, :, None], (3, CONV_WIDTH, tm)),
        "conv_b": jnp.broadcast_to(conv_b[l][:, None], (CONV_WIDTH, tm)),
        "out_g": out_norm_g[l].reshape(1, -1),
        "w_out": w_out[l].astype(BF16),
        "w_g": w_gu[l][:, :dff].astype(BF16),
        "w_u": w_gu[l][:, dff:].astype(BF16),
        "w_d": w_down[l].astype(BF16),
    }


def kernel(x, c, positions, norm1_g, norm2_g, w_ada, b_ada, w_in, na_q_g, na_k_g, na_rpb, mla_q_a_g,
           mla_kv_a_g, mla_w_uq, mla_w_ukv, mla_qn_g, mla_kn_g, mla_qr_g, mla_kr_g, conv_w, conv_b,
           out_norm_g, w_out, w_gu, w_down):
    B, T, D = x.shape
    depth = w_in.shape[0]
    rows = T // GRID_W
    t = _tiles(T)
    assert T % t.mla_q == 0 and T % t.proj == 0 and T % t.mix == 0 and T % t.rope == 0
    assert rows % NA_Q_ROWS == 0 and rows >= NA_K_ROWS + NA_SUB_ROWS
    mod = _ada_modulation(c, w_ada, b_ada, t.ada_cols)
    cos_t, sin_t = _rope_tables(positions, t.rope)
    na_bias = _na_bias_tables(na_rpb, rows)
    for l in range(depth):
        p = _layer_params(l, t.mix, norm1_g, norm2_g, w_in, na_q_g, na_k_g, mla_q_a_g, mla_kv_a_g,
                          mla_w_uq, mla_w_ukv, mla_qn_g, mla_kn_g, mla_qr_g, mla_kr_g, conv_w,
                          conv_b, out_norm_g, w_out, w_gu, w_down)
        qna, kna, vna_t, qm, km, vm_t, u_t, gb_t = _proj_call(x, mod[l], cos_t, sin_t, p, t.proj, t.proj_sub)
        yna_t = _na_call(qna, kna, vna_t, na_bias, l)
        ym_t = _mla_call(qm, km, vm_t, tq=t.mla_q, tk=t.mla_k, sw=t.strip, unroll=t.mla_unroll)
        x = _mix_ffn_call(x, mod[l], yna_t, ym_t, u_t, gb_t, p, t.mix, t.ffn_chunk)
    return x
```

```python
import functools
from typing import NamedTuple

import jax
import jax.numpy as jnp
import numpy as np
from jax import lax
from jax.experimental import pallas as pl
from jax.experimental.pallas import tpu as pltpu

F32 = jnp.float32
BF16 = jnp.bfloat16

GRID_W = 64
HEAD_DIM = 64
NA_HEADS = 4
NA_KR = 8
NA_KC = 16
MLA_HEADS = 8
MLA_NOPE = 64
MLA_ROPE = 32
MLA_V = 64
MLA_Q_RANK = 384
MLA_KV_RANK = 256
CONV_WIDTH = 256
ROPE_THETA = 10000.0
EPS = 1e-6

LANES = 128
SUBLANES = 8
MXU_WIDTH = 256
NA_Q_ROWS = 32
NA_SUB_ROWS = MXU_WIDTH // GRID_W
BF16_ROWS = 16
MLA_VROWS = MLA_V + 64
NA_VROWS = HEAD_DIM + BF16_ROWS
NA_K_ROWS = 12
MASK_VALUE = -1e30
VMEM_LIMIT = 56 * 1024 * 1024


class _Tiles(NamedTuple):
    proj: int
    proj_sub: int
    mix: int
    mla_q: int
    mla_k: int
    strip: int
    mla_unroll: int
    ffn_chunk: int
    rope: int
    ada_cols: int


def _tiles(T):
    mla_k = 512
    return _Tiles(proj=min(1024, T), proj_sub=MXU_WIDTH, mix=512, mla_q=1024, mla_k=mla_k, strip=MXU_WIDTH,
                  mla_unroll=min(16, T // mla_k), ffn_chunk=MXU_WIDTH, rope=min(2048, T), ada_cols=1536)


def _cparams(sem):
    return pltpu.CompilerParams(dimension_semantics=sem, vmem_limit_bytes=VMEM_LIMIT)


def _const_spec(shape):
    nd = len(shape)
    return pl.BlockSpec(shape, lambda *_: (0,) * nd, pipeline_mode=pl.Buffered(1))


def _split_bf16(x):
    hi = x.astype(BF16)
    lo = (x - hi.astype(F32)).astype(BF16)
    return hi, lo


def _ada_kernel(c_ref, w_ref, b_ref, o_ref):
    c = c_ref[...]
    a = c * jax.nn.sigmoid(c)
    a_hi, a_lo = _split_bf16(a)
    w_hi, w_lo = _split_bf16(w_ref[0])
    acc = jnp.dot(a_hi, w_hi, preferred_element_type=F32)
    acc += jnp.dot(a_lo, w_hi, preferred_element_type=F32)
    acc += jnp.dot(a_hi, w_lo, preferred_element_type=F32)
    o_ref[0] = acc + b_ref[0]


def _ada_modulation(c, w_ada, b_ada, tn):
    L, D, N = w_ada.shape
    B = c.shape[0]
    rows = SUBLANES * pl.cdiv(B, SUBLANES)
    c_pad = jnp.zeros((rows, D), F32).at[:B].set(c)
    out = pl.pallas_call(
        _ada_kernel,
        grid=(L, N // tn),
        in_specs=[
            pl.BlockSpec((rows, D), lambda l, j: (0, 0)),
            pl.BlockSpec((1, D, tn), lambda l, j: (l, 0, j)),
            pl.BlockSpec((1, 1, tn), lambda l, j: (l, 0, j)),
        ],
        out_specs=pl.BlockSpec((1, rows, tn), lambda l, j: (l, 0, j)),
        out_shape=jax.ShapeDtypeStruct((L, rows, N), F32),
        compiler_params=_cparams(("parallel", "parallel")),
        name="ada_mod",
    )(c_pad, w_ada, b_ada.reshape(L, 1, N))
    return out[:, :B].reshape(L, B, 6, D)


def _rope_kernel(pos_ref, inv_ref, cos_ref, sin_ref):
    ang = pos_ref[0].astype(F32) * inv_ref[...]
    c = jnp.cos(ang)
    s = jnp.sin(ang)
    tm = ang.shape[1]
    pad = LANES - MLA_NOPE - MLA_ROPE
    cos_t = jnp.concatenate([jnp.ones((MLA_NOPE, tm), F32), c, c, jnp.ones((pad, tm), F32)], axis=0)
    sin_t = jnp.concatenate([jnp.zeros((MLA_NOPE, tm), F32), -s, s, jnp.zeros((pad, tm), F32)], axis=0)
    cos_ref[0] = cos_t.T
    sin_ref[0] = sin_t.T


def _rope_tables(positions, tm):
    B, T = positions.shape
    half = MLA_ROPE // 2
    inv = ROPE_THETA ** (-jnp.arange(0, MLA_ROPE, 2, dtype=F32) / MLA_ROPE)
    spec = pl.BlockSpec((1, tm, LANES), lambda b, i: (b, i, 0))
    return pl.pallas_call(
        _rope_kernel,
        grid=(B, T // tm),
        in_specs=[
            pl.BlockSpec((1, 1, tm), lambda b, i: (b, 0, i)),
            _const_spec((half, tm)),
        ],
        out_specs=[spec, spec],
        out_shape=[jax.ShapeDtypeStruct((B, T, LANES), F32)] * 2,
        compiler_params=_cparams(("parallel", "parallel")),
        name="rope_tables",
    )(positions.reshape(B, 1, T), jnp.broadcast_to(inv[:, None], (half, tm)))


def _pair_ms(x, m_ref):
    return jnp.dot((x * x).astype(BF16), m_ref[...], preferred_element_type=F32)


def _slab_rope(xs, cos, sin, first_half):
    half = MLA_ROPE // 2
    w = xs.shape[1]
    partner = jnp.where(first_half,
                        pltpu.roll(xs, w - half, 1),
                        pltpu.roll(xs, half, 1))
    return xs * cos + partner * sin


def _store_values(v_ref, tok, vt, heads, width, rows):
    ones = jnp.ones((rows - width, vt.shape[1]), BF16)
    for hd in range(heads):
        v_ref[0, hd * rows:hd * rows + width, tok] = vt[hd * width:(hd + 1) * width]
        v_ref[0, hd * rows + width:(hd + 1) * rows, tok] = ones


def _row_rms(x, gain):
    ms = jnp.mean(x * x, axis=-1, keepdims=True)
    return x * lax.rsqrt(ms + EPS) * gain


def _proj_kernel(x_ref, mod_ref, g1_ref, win_ref, wuq_ref, wuk_ref, wuv_ref, m2_ref, mna_ref,
                 gqna_ref, gkna_ref, gqa_ref, gkva_ref, gq_ref, gk_ref, gkr_ref, cos_ref, sin_ref,
                 qna_ref, kna_ref, vna_ref, qm_ref, km_ref, vm_ref, u_ref, gb_ref, *, sub):
    for t0 in range(0, x_ref.shape[1], sub):
        _proj_sub_tile(slice(t0, t0 + sub), x_ref, mod_ref, g1_ref, win_ref, wuq_ref, wuk_ref, wuv_ref,
                       m2_ref, mna_ref, gqna_ref, gkna_ref, gqa_ref, gkva_ref, gq_ref, gk_ref, gkr_ref,
                       cos_ref, sin_ref, qna_ref, kna_ref, vna_ref, qm_ref, km_ref, vm_ref, u_ref, gb_ref)


def _proj_sub_tile(tok, x_ref, mod_ref, g1_ref, win_ref, wuq_ref, wuk_ref, wuv_ref, m2_ref, mna_ref,
                   gqna_ref, gkna_ref, gqa_ref, gkva_ref, gq_ref, gk_ref, gkr_ref, cos_ref, sin_ref,
                   qna_ref, kna_ref, vna_ref, qm_ref, km_ref, vm_ref, u_ref, gb_ref):
    x = x_ref[0, tok, :]
    sh = mod_ref[0, 0:1, :]
    sc = mod_ref[0, 1:2, :]
    h = _row_rms(x, g1_ref[...]) * (1.0 + sc) + sh
    hb = h.astype(BF16)

    nq = NA_HEADS * HEAD_DIM
    pair = 2 * LANES
    cw = CONV_WIDTH
    pairs = range(0, MLA_HEADS * LANES, pair)

    pall = jnp.dot(hb, win_ref[...], preferred_element_type=F32)
    o = 0
    pq = pall[:, o:o + nq]; o += nq
    pk = pall[:, o:o + nq]; o += nq
    pv = pall[:, o:o + nq]; o += nq
    cq = pall[:, o:o + MLA_Q_RANK]; o += MLA_Q_RANK
    ckv = pall[:, o:o + MLA_KV_RANK]; o += MLA_KV_RANK
    kr = pall[:, o:o + LANES]; o += LANES
    pc = pall[:, o:o + 3 * cw]

    cq = _row_rms(cq, gqa_ref[...]).astype(BF16)
    ckv = _row_rms(ckv, gkva_ref[...]).astype(BF16)
    q = jnp.dot(cq, wuq_ref[...], preferred_element_type=F32)
    kn = jnp.dot(ckv, wuk_ref[...], preferred_element_type=F32)
    vt = jnp.dot(ckv, wuv_ref[...], preferred_element_type=F32)

    ms_qna = _pair_ms(pq, mna_ref)
    ms_kna = _pair_ms(pk, mna_ref)
    ms_q = [_pair_ms(q[:, s0:s0 + pair], m2_ref) for s0 in pairs]
    ms_k = [_pair_ms(kn[:, s0:s0 + pair], m2_ref) for s0 in pairs]

    qna_ref[0, :, tok] = (pq * lax.rsqrt(ms_qna + EPS) * gqna_ref[...]).T.astype(BF16)
    kna_ref[0, tok, :] = (pk * lax.rsqrt(ms_kna + EPS) * gkna_ref[...]).astype(BF16)
    _store_values(vna_ref, tok, pv.T.astype(BF16), NA_HEADS, HEAD_DIM, NA_VROWS)

    cos = cos_ref[0, tok, :]
    sin = sin_ref[0, tok, :]
    cos2 = jnp.concatenate([cos, cos], axis=1)
    sin2 = jnp.concatenate([sin, sin], axis=1)
    lane = lax.broadcasted_iota(jnp.int32, (1, pair), 1) % LANES
    first_half = lane < MLA_NOPE + MLA_ROPE // 2

    for s0, ms in zip(pairs, ms_q):
        qs = q[:, s0:s0 + pair] * lax.rsqrt(ms + EPS) * gq_ref[...]
        qm_ref[0, s0:s0 + pair, tok] = _slab_rope(qs, cos2, sin2, first_half).T.astype(BF16)

    kr = kr * lax.rsqrt(jnp.sum(kr * kr, axis=-1, keepdims=True) * (1.0 / MLA_ROPE) + EPS) * gkr_ref[...]
    kr = _slab_rope(kr, cos, sin, first_half[:, :LANES])
    kr2 = jnp.concatenate([kr, kr], axis=1)
    for s0, ms in zip(pairs, ms_k):
        kn_s = kn[:, s0:s0 + pair] * lax.rsqrt(ms + EPS) * gk_ref[...]
        km_ref[0, tok, s0:s0 + pair] = (kn_s + kr2).astype(BF16)
    _store_values(vm_ref, tok, vt.T.astype(BF16), MLA_HEADS, MLA_V, MLA_VROWS)

    u_ref[0, :, tok] = (pc[:, 2 * cw:3 * cw] * pc[:, 0:cw]).T
    gb_ref[0, :, tok] = pc[:, cw:2 * cw].T


def _proj_call(x, mod, cos_t, sin_t, p, tm, sub):
    B, T, D = x.shape
    nq = NA_HEADS * HEAD_DIM
    nm = MLA_HEADS * LANES
    tok = lambda w: pl.BlockSpec((1, tm, w), lambda b, i: (b, i, 0))
    chan = lambda c: pl.BlockSpec((1, c, tm), lambda b, i: (b, 0, i))
    consts = [p["g1"], p["w_in"], p["w_uq"], p["w_uk"], p["w_uv"], p["m2"], p["m_na"], p["g_qna"],
              p["g_kna"], p["g_qa"], p["g_kva"], p["g_q"], p["g_k"], p["g_kr"]]
    in_specs = ([tok(D), pl.BlockSpec((1, 6, D), lambda b, i: (b, 0, 0))]
                + [_const_spec(a.shape) for a in consts] + [tok(LANES), tok(LANES)])
    out_shape = [
        jax.ShapeDtypeStruct((B, nq, T), BF16), jax.ShapeDtypeStruct((B, T, nq), BF16),
        jax.ShapeDtypeStruct((B, NA_HEADS * NA_VROWS, T), BF16),
        jax.ShapeDtypeStruct((B, nm, T), BF16), jax.ShapeDtypeStruct((B, T, nm), BF16),
        jax.ShapeDtypeStruct((B, MLA_HEADS * MLA_VROWS, T), BF16),
        jax.ShapeDtypeStruct((B, CONV_WIDTH, T), F32), jax.ShapeDtypeStruct((B, CONV_WIDTH, T), F32),
    ]
    out_specs = [chan(nq), tok(nq), chan(NA_HEADS * NA_VROWS), chan(nm), tok(nm),
                 chan(MLA_HEADS * MLA_VROWS), chan(CONV_WIDTH), chan(CONV_WIDTH)]
    return pl.pallas_call(
        functools.partial(_proj_kernel, sub=sub),
        grid=(B, T // tm),
        in_specs=in_specs,
        out_specs=out_specs,
        out_shape=out_shape,
        compiler_params=_cparams(("parallel", "parallel")),
        name="in_proj",
    )(x, mod, *consts, cos_t, sin_t)


def _na_variant(sb, nsb):
    return jnp.where(sb == 0, 0, jnp.where(sb == nsb - 1, 2, 1))


def _na_kernel(q_ref, k_ref, v_ref, *refs, rows):
    nsub = NA_Q_ROWS // NA_SUB_ROWS
    bias_first, bias_mid, bias_last, o_ref, s_ref, mx_ref = refs
    bias_refs = [bias_first] + [bias_mid] * (nsub - 2) + [bias_last]
    rb = pl.program_id(1)
    sw = NA_SUB_ROWS * GRID_W
    nk = NA_K_ROWS * GRID_W
    kh = nk // 2
    kstarts = [pl.multiple_of(jnp.clip(rb * NA_Q_ROWS + c * NA_SUB_ROWS - NA_KR // 2, 0, rows - NA_K_ROWS)
                              * GRID_W, MXU_WIDTH) for c in range(nsub)]
    units = [(h, c) for h in range(NA_HEADS) for c in range(nsub)]
    chan = lax.broadcasted_iota(jnp.int32, (LANES, 1), 0)

    def score(i):
        h, c = units[i]
        slab = slice((h // 2) * LANES, (h // 2 + 1) * LANES)
        q = q_ref[0, slab, c * sw:(c + 1) * sw]
        q = jnp.where(chan >= HEAD_DIM if h % 2 else chan < HEAD_DIM, q, jnp.zeros_like(q))
        mx = None
        for r0 in range(0, nk, kh):
            k = k_ref[0, pl.ds(pl.multiple_of(kstarts[c] + r0, LANES), kh), slab]
            s = jnp.dot(k, q, preferred_element_type=F32) + bias_refs[c][0, 0, h, r0:r0 + kh, :]
            s_ref[i % 3, r0:r0 + kh, :] = s
            part = jnp.max(s, axis=0, keepdims=True)
            mx = part if mx is None else jnp.maximum(mx, part)
        mx_ref[i % 3] = mx

    score(0)
    score(1)
    for i, (h, c) in enumerate(units):
        p = jnp.exp2(s_ref[i % 3] - mx_ref[i % 3]).astype(BF16)
        v = v_ref[0, h * NA_VROWS:(h + 1) * NA_VROWS, pl.ds(kstarts[c], nk)]
        acc = jnp.dot(v, p, preferred_element_type=F32)
        o_ref[0, h * HEAD_DIM:(h + 1) * HEAD_DIM, c * sw:(c + 1) * sw] = (
            acc[:HEAD_DIM] / acc[HEAD_DIM:HEAD_DIM + 1])
        if i + 2 < len(units):
            score(i + 2)


def _na_bias_kernel(e_ref, o_ref, *, rows):
    lane = lax.broadcasted_iota(jnp.int32, (GRID_W, LANES), 1)
    masked = jnp.full((GRID_W, LANES), MASK_VALUE, F32)
    variants = ((0, 0), (NA_SUB_ROWS, NA_SUB_ROWS - NA_KR // 2), (rows - NA_SUB_ROWS, rows - NA_K_ROWS))
    for v, (r0, ks) in enumerate(variants):
        for kr in range(NA_K_ROWS):
            for pair in range(NA_SUB_ROWS // 2):
                blocks = []
                for qr in (2 * pair, 2 * pair + 1):
                    r, k = r0 + qr, ks + kr
                    row_start = min(max(r - NA_KR // 2, 0), rows - NA_KR)
                    ok = row_start <= k < row_start + NA_KR
                    blocks.append(e_ref[0, 0, k - r + NA_KR - 1] if ok else masked)
                o_ref[0, v, 0, kr * GRID_W:(kr + 1) * GRID_W, pair * LANES:(pair + 1) * LANES] = (
                    jnp.where(lane < GRID_W, blocks[0], blocks[1]))


def _na_bias_tables(rpb, rows):
    L, H = rpb.shape[:2]
    cols = np.arange(GRID_W)
    col_start = np.clip(cols - NA_KC // 2, 0, GRID_W - NA_KC)
    col_ok = (cols[None, :] >= col_start[:, None]) & (cols[None, :] < col_start[:, None] + NA_KC)
    dc = np.clip(cols[None, :] - cols[:, None] + NA_KC - 1, 0, 2 * NA_KC - 2)
    sel = np.eye(2 * NA_KC - 1, dtype=np.float32)[dc.T]
    sel = np.concatenate([sel, sel], axis=1)
    ok = np.concatenate([col_ok.T, col_ok.T], axis=1)
    e = jnp.einsum("lhij,wqj->lhiwq", rpb.astype(F32) * float(np.log2(np.e)), jnp.asarray(sel),
                   precision=lax.Precision.HIGHEST)
    e = jnp.where(jnp.asarray(ok), e, MASK_VALUE)
    nk, sw = NA_K_ROWS * GRID_W, NA_SUB_ROWS * GRID_W
    return pl.pallas_call(
        functools.partial(_na_bias_kernel, rows=rows),
        grid=(L, H),
        in_specs=[pl.BlockSpec((1, 1, 2 * NA_KR - 1, GRID_W, LANES), lambda l, h: (l, h, 0, 0, 0))],
        out_specs=pl.BlockSpec((1, 3, 1, nk, sw), lambda l, h: (l, 0, h, 0, 0)),
        out_shape=jax.ShapeDtypeStruct((L, 3, H, nk, sw), F32),
        compiler_params=_cparams(("parallel", "parallel")),
        name="na_bias",
    )(e)


def _na_call(qna, kna, vna_t, bias, layer):
    B, T, _ = kna.shape
    rows = T // GRID_W
    nrb = rows // NA_Q_ROWS
    nsub = NA_Q_ROWS // NA_SUB_ROWS
    nq = NA_Q_ROWS * GRID_W
    sw = NA_SUB_ROWS * GRID_W
    nk = NA_K_ROWS * GRID_W
    assert nsub >= 2
    bias_block = (1, 1, NA_HEADS, nk, sw)
    bias_specs = [
        pl.BlockSpec(bias_block, lambda b, rb: (layer, _na_variant(rb * nsub, nrb * nsub), 0, 0, 0)),
        pl.BlockSpec(bias_block, lambda b, rb: (layer, 1, 0, 0, 0), pipeline_mode=pl.Buffered(1)),
        pl.BlockSpec(bias_block, lambda b, rb: (layer, _na_variant(rb * nsub + nsub - 1, nrb * nsub), 0, 0, 0)),
    ]
    return pl.pallas_call(
        functools.partial(_na_kernel, rows=rows),
        grid=(B, nrb),
        in_specs=[
            pl.BlockSpec((1, NA_HEADS * HEAD_DIM, nq), lambda b, rb: (b, 0, rb)),
            pl.BlockSpec((1, T, NA_HEADS * HEAD_DIM), lambda b, rb: (b, 0, 0)),
            pl.BlockSpec((1, NA_HEADS * NA_VROWS, T), lambda b, rb: (b, 0, 0)),
        ] + bias_specs,
        out_specs=pl.BlockSpec((1, NA_HEADS * HEAD_DIM, nq), lambda b, rb: (b, 0, rb)),
        out_shape=jax.ShapeDtypeStruct((B, NA_HEADS * HEAD_DIM, T), F32),
        scratch_shapes=[pltpu.VMEM((3, nk, sw), F32), pltpu.VMEM((3, 1, sw), F32)],
        compiler_params=_cparams(("parallel", "arbitrary")),
        name="na_attn",
    )(qna, kna, vna_t, bias, bias, bias)


def _mla_kernel(q_ref, k_ref, v_ref, o_ref, s_ref, mx_ref, *, tq, tk, sw, unroll):
    T = k_ref.shape[1]
    nq = T // tq
    nkv = T // tk
    strips = [slice(c * sw, (c + 1) * sw) for c in range(tq // sw)]
    ns = len(strips)
    hk = tk // 2

    def score(slot, qi, j, c):
        k = k_ref[0, pl.ds(pl.multiple_of(j * tk, tk), tk), :]
        q = q_ref[0, :, pl.ds(pl.multiple_of(qi * tq + c * sw, sw), sw)]
        s = jnp.dot(k, q, preferred_element_type=F32)
        s_ref[slot, c] = s
        mx_ref[slot, :, strips[c]] = jnp.max(s, axis=0, keepdims=True)

    def step(slot, j, nxt_qi, nxt_j, m, acc):
        v = v_ref[0, :, pl.ds(pl.multiple_of(j * tk, tk), tk)]
        ms, accs = [], []
        for c in range(min(2, ns)):
            score(1 - slot, nxt_qi, nxt_j, c)
        for c, sl in enumerate(strips):
            m_old = m[:, sl]
            m_new = jnp.maximum(m_old, mx_ref[slot, :, sl])
            p = jnp.exp2(s_ref[slot, c] - m_new).astype(BF16)
            pv = jnp.dot(v[:, :hk], p[:hk], preferred_element_type=F32)
            if c + 2 < ns:
                score(1 - slot, nxt_qi, nxt_j, c + 2)
            pv = pv + jnp.dot(v[:, hk:], p[hk:], preferred_element_type=F32)
            accs.append(jnp.exp2(m_old - m_new) * acc[:, sl] + pv)
            ms.append(m_new)
        return jnp.concatenate(ms, axis=1), jnp.concatenate(accs, axis=1)

    def body(bi, carry):
        m, acc = carry
        t0 = bi * unroll
        qi = t0 // nkv
        j0 = t0 % nkv
        fresh = j0 == 0
        m = jnp.where(fresh, -jnp.inf, m)
        acc = jnp.where(fresh, 0.0, acc)
        for u in range(unroll):
            if u + 1 < unroll:
                nxt_qi, nxt_j = qi, j0 + u + 1
            else:
                nxt_qi = jnp.minimum(qi + (j0 + unroll) // nkv, nq - 1)
                nxt_j = (j0 + unroll) % nkv
            m, acc = step(u % 2, j0 + u, nxt_qi, nxt_j, m, acc)

        @pl.when(j0 + unroll == nkv)
        def _():
            o_ref[0, :, pl.ds(pl.multiple_of(qi * tq, tq), tq)] = acc[:MLA_V] / acc[MLA_V:MLA_V + 1]

        return m, acc

    for c in range(ns):
        score(0, 0, 0, c)
    init = (jnp.full((1, tq), -jnp.inf, F32), jnp.zeros((MLA_VROWS, tq), F32))
    lax.fori_loop(0, nq * nkv // unroll, body, init)


def _mla_call(qm, km, vm_t, tq, tk, sw, unroll):
    B, T, _ = km.shape
    assert unroll % 2 == 0 and (T // tk) % unroll == 0 and T % tq == 0 and tq % sw == 0
    return pl.pallas_call(
        functools.partial(_mla_kernel, tq=tq, tk=tk, sw=sw, unroll=unroll),
        grid=(B, MLA_HEADS),
        in_specs=[
            pl.BlockSpec((1, LANES, T), lambda b, h: (b, h, 0)),
            pl.BlockSpec((1, T, LANES), lambda b, h: (b, 0, h)),
            pl.BlockSpec((1, MLA_VROWS, T), lambda b, h: (b, h, 0)),
        ],
        out_specs=pl.BlockSpec((1, MLA_V, T), lambda b, h: (b, h, 0)),
        out_shape=jax.ShapeDtypeStruct((B, MLA_HEADS * MLA_V, T), F32),
        scratch_shapes=[pltpu.VMEM((2, tq // sw, tk, sw), F32), pltpu.VMEM((2, 1, tq), F32)],
        compiler_params=_cparams(("parallel", "parallel")),
        name="mla_attn",
    )(qm, km, vm_t)


def _group_rms_rows(x):
    c, tm = x.shape
    xg = x.reshape(c // HEAD_DIM, HEAD_DIM, tm)
    ms = jnp.mean(xg * xg, axis=1, keepdims=True)
    return (xg * lax.rsqrt(ms + EPS)).reshape(c, tm)


def _mix_ffn_kernel(x_ref, mod_ref, yna_ref, ym_ref, u_ref, up_ref, un_ref, gb_ref, cw_ref, cb_ref,
                    og_ref, wout_ref, g2_ref, wg_ref, wu_ref, wd_ref, o_ref, act_ref, *, chunk):
    i = pl.program_id(1)
    last = pl.num_programs(1) - 1
    u = u_ref[0]
    tm = u.shape[1]
    prev = jnp.where(i > 0, up_ref[0], 0.0)
    nxt = jnp.where(i < last, un_ref[0], 0.0)
    ext = jnp.concatenate([prev, u, nxt], axis=1)
    w = ext.shape[1]
    u_m1 = pltpu.roll(ext, 1, 1)[:, LANES:LANES + tm]
    u_p1 = pltpu.roll(ext, w - 1, 1)[:, LANES:LANES + tm]
    y = cw_ref[0] * u_m1 + cw_ref[1] * u + cw_ref[2] * u_p1 + cb_ref[...]
    yc = gb_ref[0] * y
    mixed = jnp.concatenate([_group_rms_rows(yna_ref[0]), _group_rms_rows(ym_ref[0]),
                             _group_rms_rows(yc)], axis=0)
    mixed = (mixed.T * og_ref[...]).astype(BF16)
    x = x_ref[0] + mod_ref[0, 2:3, :] * jnp.dot(mixed, wout_ref[...], preferred_element_type=F32)
    sh = mod_ref[0, 3:4, :]
    sc = mod_ref[0, 4:5, :]
    hb = (_row_rms(x, g2_ref[...]) * (1.0 + sc) + sh).astype(BF16)
    dff = wg_ref.shape[1]
    for c in range(dff // chunk):
        sl = slice(c * chunk, (c + 1) * chunk)
        g = jnp.dot(hb, wg_ref[:, sl], preferred_element_type=F32)
        up = jnp.dot(hb, wu_ref[:, sl], preferred_element_type=F32)
        act_ref[:, sl] = (g * jax.nn.sigmoid(g) * up).astype(BF16)
    out = jnp.dot(act_ref[...], wd_ref[...], preferred_element_type=F32)
    o_ref[0] = x + mod_ref[0, 5:6, :] * out


def _mix_ffn_call(x, mod, yna_t, ym_t, u_t, gb_t, p, tm, chunk):
    B, T, D = x.shape
    nb = tm // LANES
    nlb = T // LANES
    dff = p["w_g"].shape[1]
    chan = lambda c: pl.BlockSpec((1, c, tm), lambda b, i: (b, 0, i))
    consts = [p["conv_w"], p["conv_b"], p["out_g"], p["w_out"], p["g2"], p["w_g"], p["w_u"], p["w_d"]]
    return pl.pallas_call(
        functools.partial(_mix_ffn_kernel, chunk=chunk),
        grid=(B, T // tm),
        in_specs=[
            pl.BlockSpec((1, tm, D), lambda b, i: (b, i, 0)),
            pl.BlockSpec((1, 6, D), lambda b, i: (b, 0, 0)),
            chan(NA_HEADS * HEAD_DIM), chan(MLA_HEADS * MLA_V), chan(CONV_WIDTH),
            pl.BlockSpec((1, CONV_WIDTH, LANES), lambda b, i: (b, 0, jnp.maximum(i * nb - 1, 0))),
            pl.BlockSpec((1, CONV_WIDTH, LANES), lambda b, i: (b, 0, jnp.minimum((i + 1) * nb, nlb - 1))),
            chan(CONV_WIDTH),
        ] + [_const_spec(a.shape) for a in consts],
        out_specs=pl.BlockSpec((1, tm, D), lambda b, i: (b, i, 0)),
        out_shape=jax.ShapeDtypeStruct((B, T, D), F32),
        scratch_shapes=[pltpu.VMEM((tm, dff), BF16)],
        compiler_params=_cparams(("parallel", "parallel")),
        name="mix_ffn",
    )(x, mod, yna_t, ym_t, u_t, u_t, u_t, gb_t, *consts)


def _pad_heads(w, heads, width):
    k = w.shape[0]
    w = w.reshape(k, heads, width)
    return jnp.pad(w, ((0, 0), (0, 0), (0, LANES - width))).reshape(k, heads * LANES)


def _lane_row(parts, repeat=1):
    row = jnp.zeros((1, LANES), F32)
    for off, v in parts:
        row = row.at[0, off:off + v.shape[0]].set(v.astype(F32))
    return jnp.tile(row, (1, repeat))


def _norm_matrices():
    mla = np.zeros((LANES, LANES), np.float32)
    mla[:MLA_NOPE, :MLA_NOPE] = 1.0 / MLA_NOPE
    mla[MLA_NOPE:MLA_NOPE + MLA_ROPE, MLA_NOPE:MLA_NOPE + MLA_ROPE] = 1.0 / MLA_ROPE
    na = np.zeros((LANES, LANES), np.float32)
    na[:HEAD_DIM, :HEAD_DIM] = 1.0 / HEAD_DIM
    na[HEAD_DIM:, HEAD_DIM:] = 1.0 / HEAD_DIM
    z = np.zeros((LANES, LANES), np.float32)
    pair = lambda m: jnp.asarray(np.block([[m, z], [z, m]]), BF16)
    return pair(mla), pair(na)


def _layer_params(l, tm, norm1_g, norm2_g, w_in, na_q_g, na_k_g, mla_q_a_g, mla_kv_a_g, mla_w_uq,
                  mla_w_ukv, mla_qn_g, mla_kn_g, mla_qr_g, mla_kr_g, conv_w, conv_b, out_norm_g,
                  w_out, w_gu, w_down):
    naw = NA_HEADS * HEAD_DIM
    i0 = 3 * naw
    i1 = i0 + MLA_Q_RANK
    i2 = i1 + MLA_KV_RANK
    i3 = i2 + MLA_ROPE
    w = w_in[l]
    d = w.shape[0]
    kr_slab = jnp.pad(w[:, i2:i3], ((0, 0), (MLA_NOPE, LANES - MLA_NOPE - MLA_ROPE)))
    w_in_r = jnp.concatenate([w[:, :i2], kr_slab, w[:, i3:]], axis=1).astype(BF16)
    ukv = mla_w_ukv[l].reshape(MLA_KV_RANK, MLA_HEADS, MLA_NOPE + MLA_V)
    w_uk = _pad_heads(ukv[:, :, :MLA_NOPE].reshape(MLA_KV_RANK, -1), MLA_HEADS, MLA_NOPE).astype(BF16)
    w_uv = ukv[:, :, MLA_NOPE:].reshape(MLA_KV_RANK, -1).astype(BF16)
    w_uq = _pad_heads(mla_w_uq[l], MLA_HEADS, MLA_NOPE + MLA_ROPE).astype(BF16)
    na_scale = HEAD_DIM ** -0.5 * float(np.log2(np.e))
    mla_scale = (MLA_NOPE + MLA_ROPE) ** -0.5 * float(np.log2(np.e))
    dff = w_down.shape[1]
    m_mla, m_na = _norm_matrices()
    return {
        "g1": norm1_g[l].reshape(1, d), "g2": norm2_g[l].reshape(1, d),
        "w_in": w_in_r, "w_uq": w_uq, "w_uk": w_uk, "w_uv": w_uv, "m2": m_mla, "m_na": m_na,
        "g_qna": _lane_row([(0, na_q_g[l] * na_scale), (HEAD_DIM, na_q_g[l] * na_scale)], 2),
        "g_kna": _lane_row([(0, na_k_g[l]), (HEAD_DIM, na_k_g[l])], 2),
        "g_qa": mla_q_a_g[l].reshape(1, -1), "g_kva": mla_kv_a_g[l].reshape(1, -1),
        "g_q": _lane_row([(0, mla_qn_g[l] * mla_scale), (MLA_NOPE, mla_qr_g[l] * mla_scale)], 2),
        "g_k": _lane_row([(0, mla_kn_g[l])], 2),
        "g_kr": _lane_row([(MLA_NOPE, mla_kr_g[l])]),
        "conv_w": jnp.broadcast_to(conv_w[l][:, :, None], (3, CONV_WIDTH, tm)),
        "conv_b": jnp.broadcast_to(conv_b[l][:, None], (CONV_WIDTH, tm)),
        "out_g": out_norm_g[l].reshape(1, -1),
        "w_out": w_out[l].astype(BF16),
        "w_g": w_gu[l][:, :dff].astype(BF16),
        "w_u": w_gu[l][:, dff:].astype(BF16),
        "w_d": w_down[l].astype(BF16),
    }


def kernel(x, c, positions, norm1_g, norm2_g, w_ada, b_ada, w_in, na_q_g, na_k_g, na_rpb, mla_q_a_g,
           mla_kv_a_g, mla_w_uq, mla_w_ukv, mla_qn_g, mla_kn_g, mla_qr_g, mla_kr_g, conv_w, conv_b,
           out_norm_g, w_out, w_gu, w_down):
    B, T, D = x.shape
    depth = w_in.shape[0]
    rows = T // GRID_W
    t = _tiles(T)
    assert T % t.mla_q == 0 and T % t.proj == 0 and T % t.mix == 0 and T % t.rope == 0
    assert rows % NA_Q_ROWS == 0 and rows >= NA_K_ROWS + NA_SUB_ROWS
    mod = _ada_modulation(c, w_ada, b_ada, t.ada_cols)
    cos_t, sin_t = _rope_tables(positions, t.rope)
    na_bias = _na_bias_tables(na_rpb, rows)
    for l in range(depth):
        p = _layer_params(l, t.mix, norm1_g, norm2_g, w_in, na_q_g, na_k_g, mla_q_a_g, mla_kv_a_g,
                          mla_w_uq, mla_w_ukv, mla_qn_g, mla_kn_g, mla_qr_g, mla_kr_g, conv_w,
                          conv_b, out_norm_g, w_out, w_gu, w_down)
        qna, kna, vna_t, qm, km, vm_t, u_t, gb_t = _proj_call(x, mod[l], cos_t, sin_t, p, t.proj, t.proj_sub)
        yna_t = _na_call(qna, kna, vna_t, na_bias, l)
        ym_t = _mla_call(qm, km, vm_t, tq=t.mla_q, tk=t.mla_k, sw=t.strip, unroll=t.mla_unroll)
        x = _mix_ffn_call(x, mod[l], yna_t, ym_t, u_t, gb_t, p, t.mix, t.ffn_chunk)
    return x
```
